```python
import jax
import jax.numpy as jnp
from jax import lax
import numpy as np

D_MODEL = 1024
BATCH = 4
SEQ = 4096
DEPTH = 4

GRID_W = 64
CTX_LEN = 256
D_MIX = D_MODEL
W_LRU = D_MIX // 4
W_POOL = D_MIX // 4
W_SGU = D_MIX // 4
W_CONV = D_MIX - W_LRU - W_POOL - W_SGU
LRU_HEADS = 4
LRU_HEAD_DIM = W_LRU // LRU_HEADS
LRU_CONV_W = 4
LRU_PAD_L = LRU_CONV_W // 2
LRU_PAD_R = LRU_CONV_W - 1 - LRU_PAD_L
LRU_C = 8.0
POOL_WINDOWS = (2, 4, 8, 16)
POOL_GROUP = W_POOL // len(POOL_WINDOWS)
SGU_CHUNK = 128
SGU_HEADS = 4
SGU_HEAD_DIM = W_SGU // SGU_HEADS
CONF_CONV_W = 31
CONF_PAD = (CONF_CONV_W - 1) // 2
D_FF = 4 * D_MODEL
D_IN = 2 * W_LRU + W_POOL + 2 * W_SGU + 2 * W_CONV
N_MOD = 6
EPS = 1e-6
POS_BASE = 10000.0

kernel_name = 'hybrid_parallel_group_flow_backbone'


def rms_norm(x, g):
    xf = x.astype(jnp.float32)
    y = xf * lax.rsqrt(jnp.mean(xf * xf, axis=-1, keepdims=True) + EPS)
    return (y * g.astype(jnp.float32)).astype(x.dtype)


def layer_norm(x, g, b):
    xf = x.astype(jnp.float32)
    mu = jnp.mean(xf, axis=-1, keepdims=True)
    xc = xf - mu
    var = jnp.mean(xc * xc, axis=-1, keepdims=True)
    y = xc * lax.rsqrt(var + EPS) * g.astype(jnp.float32) + b.astype(jnp.float32)
    return y.astype(x.dtype)


def modulate(h, shift, scale):
    return h * (1.0 + scale) + shift


def grid_pos_embed(n_tokens, dim):
    rows = n_tokens // GRID_W
    row = jnp.broadcast_to(jnp.arange(rows)[:, None], (rows, GRID_W)).reshape(-1)
    col = jnp.broadcast_to(jnp.arange(GRID_W)[None, :], (rows, GRID_W)).reshape(-1)
    quarter = dim // 4
    omega = 1.0 / (POS_BASE ** (jnp.arange(quarter, dtype=jnp.float32) / quarter))

    def enc(pos):
        ang = pos.astype(jnp.float32)[:, None] * omega[None, :]
        return jnp.concatenate([jnp.sin(ang), jnp.cos(ang)], axis=-1)

    return jnp.concatenate([enc(row), enc(col)], axis=-1)


def depthwise_conv(x, w, b, pad_l, pad_r):
    y = lax.conv_general_dilated(
        x, w[:, None, :].astype(x.dtype), window_strides=(1,), padding=[(pad_l, pad_r)],
        dimension_numbers=('NWC', 'WIO', 'NWC'), feature_group_count=x.shape[-1])
    return y + b.astype(x.dtype)


def linear_scan(a, b, h0):
    b = b.at[:, 0].add(a[:, 0] * h0)

    def comb(l, r):
        return (l[0] * r[0], r[0] * l[1] + r[1])

    _, h = lax.associative_scan(comb, (a, b), axis=1)
    return h


def rglru_dir(xc, wa, ba, wx, bx, lam, h0, reverse):
    bsz, s, _ = xc.shape
    xf = xc.astype(jnp.float32)
    xh = xf.reshape(bsz, s, LRU_HEADS, LRU_HEAD_DIM)
    r = jax.nn.sigmoid(jnp.einsum('bshi,hij->bshj', xh, wa.astype(jnp.float32)).reshape(bsz, s, W_LRU) + ba.astype(jnp.float32))
    i = jax.nn.sigmoid(jnp.einsum('bshi,hij->bshj', xh, wx.astype(jnp.float32)).reshape(bsz, s, W_LRU) + bx.astype(jnp.float32))
    log_a = -LRU_C * r * jax.nn.softplus(-lam.astype(jnp.float32))
    a = jnp.exp(log_a)
    b = jnp.sqrt(-jnp.expm1(2.0 * log_a)) * (i * xf)
    if reverse:
        h = linear_scan(a[:, ::-1], b[:, ::-1], h0)
        return h[:, ::-1]
    return linear_scan(a, b, h0)


def pool_mixer(z, w, scale):
    bsz, s, _ = z.shape
    zf = z.astype(jnp.float32)
    cs = jnp.concatenate([jnp.zeros((bsz, 1, W_POOL), jnp.float32), jnp.cumsum(zf, axis=1)], axis=1)
    t = jnp.arange(s)
    outs = []
    for gi, win in enumerate(POOL_WINDOWS):
        lo = win // 2
        hi = win - lo - 1
        start = jnp.clip(t - lo, 0, s)
        end = jnp.clip(t + hi + 1, 0, s)
        sl = slice(gi * POOL_GROUP, (gi + 1) * POOL_GROUP)
        cnt = (end - start).astype(jnp.float32)[None, :, None]
        outs.append((cs[:, end, sl] - cs[:, start, sl]) / cnt - zf[..., sl])
    p = jnp.stack(outs, axis=2)
    y = jnp.einsum('bsgi,gij->bsgj', p, w.astype(jnp.float32)).reshape(bsz, s, W_POOL)
    return (y * scale.astype(jnp.float32)).astype(z.dtype)


def spatial_gating(u, v, ln_g, ln_b, ws, bs):
    bsz, s, _ = v.shape
    n_chunks = s // SGU_CHUNK
    vn = layer_norm(v, ln_g, ln_b).reshape(bsz, n_chunks, SGU_CHUNK, SGU_HEADS, SGU_HEAD_DIM)
    z = jnp.einsum('hpq,bnqhc->bnphc', ws.astype(vn.dtype), vn) + bs.T.astype(vn.dtype)[:, :, None]
    return u * z.reshape(bsz, s, W_SGU)


def conformer_conv(val, gate, w, b, ln_g, ln_b):
    y = val * jax.nn.sigmoid(gate)
    y = depthwise_conv(y, w, b, CONF_PAD, CONF_PAD)
    y = layer_norm(y, ln_g, ln_b)
    return jax.nn.silu(y)


def mix_rest(h_lru, rest, pool_w, pool_scale, sgu_ln_g, sgu_ln_b, sgu_w, sgu_b,
             conv_d_w, conv_d_b, conv_ln_g, conv_ln_b):
    idx = [W_LRU, W_LRU + W_POOL, W_LRU + W_POOL + W_SGU, W_LRU + W_POOL + 2 * W_SGU,
           W_LRU + W_POOL + 2 * W_SGU + W_CONV]
    gate, zp, u, v, cv, cg = jnp.split(rest, idx, axis=-1)
    y_a = (h_lru * jax.nn.gelu(gate.astype(jnp.float32))).astype(rest.dtype)
    y_b = pool_mixer(zp, pool_w, pool_scale)
    y_c = spatial_gating(u, v, sgu_ln_g, sgu_ln_b, sgu_w, sgu_b)
    y_d = conformer_conv(cv, cg, conv_d_w, conv_d_b, conv_ln_g, conv_ln_b)
    return jnp.concatenate([y_a, y_b, y_c, y_d], axis=-1)


def sq_relu_mlp(h, w1, w2):
    return jnp.square(jax.nn.relu(h @ w1)) @ w2


def setup_inputs(seed: int = 0) -> dict:
    key = jax.random.key(seed)
    ks = jax.random.split(key, 32)
    f32 = jnp.float32
    L = DEPTH

    def nrm(k, shape, s):
        return jax.random.normal(k, shape, f32) * s

    u = jax.random.uniform(ks[15], (L, 2, W_LRU), f32, minval=0.9, maxval=0.999)
    a_base = u ** (1.0 / LRU_C)
    lru_lambda = jnp.log(a_base) - jnp.log1p(-a_base)
    return {
        'x': nrm(ks[0], (BATCH, SEQ, D_MODEL), 1.0),
        'c': nrm(ks[1], (BATCH, D_MODEL), 1.0),
        'ctx': nrm(ks[2], (BATCH, CTX_LEN, D_MODEL), 1.0),
        'c_ctx': nrm(ks[3], (D_MODEL,), 1.0),
        'w_mod': nrm(ks[4], (L, D_MODEL, N_MOD * D_MODEL), 0.5 * D_MODEL ** -0.5),
        'b_mod': nrm(ks[5], (L, N_MOD * D_MODEL), 0.02),
        'g_norm1': 1.0 + nrm(ks[6], (L, D_MODEL), 0.02),
        'g_norm2': 1.0 + nrm(ks[7], (L, D_MODEL), 0.02),
        'w_in': nrm(ks[8], (L, D_MODEL, D_IN), D_MODEL ** -0.5),
        'conv_a_w': nrm(ks[9], (L, LRU_CONV_W, W_LRU), LRU_CONV_W ** -0.5),
        'conv_a_b': nrm(ks[10], (L, W_LRU), 0.02),
        'lru_wa': nrm(ks[11], (L, 2, LRU_HEADS, LRU_HEAD_DIM, LRU_HEAD_DIM), LRU_HEAD_DIM ** -0.5),
        'lru_ba': nrm(ks[12], (L, 2, W_LRU), 0.02),
        'lru_wx': nrm(ks[13], (L, 2, LRU_HEADS, LRU_HEAD_DIM, LRU_HEAD_DIM), LRU_HEAD_DIM ** -0.5),
        'lru_bx': nrm(ks[14], (L, 2, W_LRU), 0.02),
        'lru_lambda': lru_lambda,
        'pool_w': nrm(ks[16], (L, len(POOL_WINDOWS), POOL_GROUP, POOL_GROUP), POOL_GROUP ** -0.5),
        'pool_scale': 1.0 + nrm(ks[17], (L, W_POOL), 0.1),
        'sgu_ln_g': 1.0 + nrm(ks[18], (L, W_SGU), 0.02),
        'sgu_ln_b': nrm(ks[19], (L, W_SGU), 0.02),
        'sgu_w': nrm(ks[20], (L, SGU_HEADS, SGU_CHUNK, SGU_CHUNK), SGU_CHUNK ** -0.5),
        'sgu_b': 1.0 + nrm(ks[21], (L, SGU_HEADS, SGU_CHUNK), 0.02),
        'conv_d_w': nrm(ks[22], (L, CONF_CONV_W, W_CONV), CONF_CONV_W ** -0.5),
        'conv_d_b': nrm(ks[23], (L, W_CONV), 0.02),
        'conv_ln_g': 1.0 + nrm(ks[24], (L, W_CONV), 0.02),
        'conv_ln_b': nrm(ks[25], (L, W_CONV), 0.02),
        'w_out': nrm(ks[26], (L, D_MIX, D_MODEL), D_MIX ** -0.5),
        'w_mlp1': nrm(ks[27], (L, D_MODEL, D_FF), D_MODEL ** -0.5),
        'w_mlp2': nrm(ks[28], (L, D_FF, D_MODEL), D_FF ** -0.5),
        'g_final': 1.0 + nrm(ks[29], (D_MODEL,), 0.02),
    }


def reference(x, c, ctx, c_ctx, w_mod, b_mod, g_norm1, g_norm2, w_in, conv_a_w, conv_a_b,
              lru_wa, lru_ba, lru_wx, lru_bx, lru_lambda, pool_w, pool_scale, sgu_ln_g, sgu_ln_b,
              sgu_w, sgu_b, conv_d_w, conv_d_b, conv_ln_g, conv_ln_b, w_out, w_mlp1, w_mlp2, g_final):
    n_lat = x.shape[1]
    x = x + grid_pos_embed(n_lat, D_MODEL).astype(x.dtype)
    xc = ctx
    for l in range(DEPTH):
        last = l == DEPTH - 1
        mod_x = jax.nn.silu(c) @ w_mod[l] + b_mod[l]
        mod_c = jax.nn.silu(c_ctx) @ w_mod[l] + b_mod[l]
        shx1, scx1, gtx1, shx2, scx2, gtx2 = jnp.split(mod_x[:, None, :], N_MOD, axis=-1)
        shc1, scc1, gtc1, shc2, scc2, gtc2 = jnp.split(mod_c, N_MOD, axis=-1)
        fwd = (lru_wa[l, 0], lru_ba[l, 0], lru_wx[l, 0], lru_bx[l, 0], lru_lambda[l, 0])
        bwd = (lru_wa[l, 1], lru_ba[l, 1], lru_wx[l, 1], lru_bx[l, 1], lru_lambda[l, 1])
        rest_params = (pool_w[l], pool_scale[l], sgu_ln_g[l], sgu_ln_b[l], sgu_w[l], sgu_b[l],
                       conv_d_w[l], conv_d_b[l], conv_ln_g[l], conv_ln_b[l])

        hc = modulate(rms_norm(xc, g_norm1[l]), shc1, scc1)
        cc = depthwise_conv(hc @ w_in[l][:, :W_LRU], conv_a_w[l], conv_a_b[l], LRU_PAD_L, LRU_PAD_R)
        h0 = jnp.zeros((xc.shape[0], W_LRU), jnp.float32)
        hcf = rglru_dir(cc, *fwd, h0, False)
        hcb = rglru_dir(cc, *bwd, h0, True)

        hx = modulate(rms_norm(x, g_norm1[l]), shx1, scx1)
        px = hx @ w_in[l]
        cx = depthwise_conv(px[..., :W_LRU], conv_a_w[l], conv_a_b[l], LRU_PAD_L, LRU_PAD_R)
        hxf = rglru_dir(cx, *fwd, hcf[:, -1], False)
        hxb = rglru_dir(cx, *bwd, hcb[:, 0], True)
        yx = mix_rest(hxf + hxb, px[..., W_LRU:], *rest_params) @ w_out[l]
        x = x + gtx1 * yx
        x = x + gtx2 * sq_relu_mlp(modulate(rms_norm(x, g_norm2[l]), shx2, scx2), w_mlp1[l], w_mlp2[l])

        if not last:
            yc = mix_rest(hcf + hcb, hc @ w_in[l][:, W_LRU:], *rest_params) @ w_out[l]
            xc = xc + gtc1 * yc
            xc = xc + gtc2 * sq_relu_mlp(modulate(rms_norm(xc, g_norm2[l]), shc2, scc2), w_mlp1[l], w_mlp2[l])
    return rms_norm(x, g_final)
```

```python
import functools
import math

import jax
import jax.numpy as jnp
from jax import lax
from jax.experimental import pallas as pl
from jax.experimental.pallas import tpu as pltpu

F32 = jnp.float32
BF16 = jnp.bfloat16

D_MODEL = 1024
GRID_W = 64
W_LRU = 256
W_POOL = 256
W_SGU = 256
W_CONV = 256
LRU_HEADS = 4
LRU_CONV_W = 4
LRU_PAD_L = 2
LRU_C = 8.0
POOL_WINDOWS = (2, 4, 8, 16)
POOL_GROUP = W_POOL // len(POOL_WINDOWS)
SGU_CHUNK = 128
SGU_HEADS = 4
SGU_HEAD_DIM = W_SGU // SGU_HEADS
CONF_CONV_W = 31
CONF_PAD = 15
D_FF = 4 * D_MODEL
D_IN = 2 * W_LRU + W_POOL + 2 * W_SGU + 2 * W_CONV
N_MOD = 6
EPS = 1e-6
POS_BASE = 10000.0

SUBLANES = 8
LANES = 128
HALO = 16
VMEM_LIMIT = 56 * 1024 * 1024
MOD_ROWS = 8
FF_CHUNK = 512


def _params(*sem):
    return pltpu.CompilerParams(dimension_semantics=sem, vmem_limit_bytes=VMEM_LIMIT)


def _sigmoid(x):
    return 1.0 / (1.0 + jnp.exp(-x))


def _silu(x):
    return x * _sigmoid(x)


def _gelu_tanh(x):
    return 0.5 * x * (1.0 + jnp.tanh(0.7978845608028654 * (x + 0.044715 * (x * x * x))))


def _rms_mod(x, g, shift, scale):
    ms = jnp.mean(x * x, axis=-1, keepdims=True)
    y = x * lax.rsqrt(ms + EPS) * g
    return y * (1.0 + scale) + shift


def _layer_norm(x, g, b):
    mu = jnp.mean(x, axis=-1, keepdims=True)
    xc = x - mu
    var = jnp.mean(xc * xc, axis=-1, keepdims=True)
    return xc * lax.rsqrt(var + EPS) * g + b


def _mod_kernel(c_ref, w_ref, b_ref, o_ref):
    s = _silu(c_ref[...]).astype(BF16)
    o_ref[...] = jnp.dot(s, w_ref[...].astype(BF16), preferred_element_type=F32) + b_ref[...]


def _modulation(cs, w_mod, b_mod):
    depth, _, n = w_mod.shape
    tn = 1536
    return pl.pallas_call(
        _mod_kernel,
        out_shape=jax.ShapeDtypeStruct((depth, MOD_ROWS, n), F32),
        grid=(depth, n // tn),
        in_specs=[
            pl.BlockSpec((MOD_ROWS, D_MODEL), lambda l, j: (0, 0)),
            pl.BlockSpec((None, D_MODEL, tn), lambda l, j: (l, 0, j)),
            pl.BlockSpec((None, 1, tn), lambda l, j: (l, 0, j)),
        ],
        out_specs=pl.BlockSpec((None, MOD_ROWS, tn), lambda l, j: (l, 0, j)),
        compiler_params=_params("arbitrary", "arbitrary"),
        name="modulation",
    )(cs, w_mod, b_mod.reshape(depth, 1, n))


def _pos_table_kernel(o_ref):
    quarter = D_MODEL // 4
    pos = lax.broadcasted_iota(jnp.int32, (GRID_W, quarter), 0).astype(F32)
    j = lax.broadcasted_iota(jnp.int32, (GRID_W, quarter), 1).astype(F32)
    omega = 1.0 / jnp.exp(j * (math.log(POS_BASE) / quarter))
    ang = pos * omega
    o_ref[:, :quarter] = jnp.sin(ang)
    o_ref[:, quarter:] = jnp.cos(ang)


def _pos_table():
    return pl.pallas_call(
        _pos_table_kernel,
        out_shape=jax.ShapeDtypeStruct((GRID_W, D_MODEL // 2), F32),
        name="pos_table",
    )()


def _add_pos_kernel(x_ref, e_ref, o_ref, *, ts):
    i = pl.program_id(1)
    e = e_ref[...]
    col = jnp.concatenate([e] * (ts // GRID_W), axis=0)
    r0 = i * (ts // GRID_W)
    rows = [jnp.broadcast_to(e_ref[pl.ds(r0 + k, 1), :], (GRID_W, D_MODEL // 2))
            for k in range(ts // GRID_W)]
    row = jnp.concatenate(rows, axis=0)
    o_ref[...] = x_ref[...] + jnp.concatenate([row, col], axis=1)


def _add_pos(x, e, ts):
    b, s, d = x.shape
    return pl.pallas_call(
        functools.partial(_add_pos_kernel, ts=ts),
        out_shape=jax.ShapeDtypeStruct(x.shape, F32),
        grid=(b, s // ts),
        in_specs=[
            pl.BlockSpec((None, ts, d), lambda bb, i: (bb, i, 0)),
            pl.BlockSpec((GRID_W, d // 2), lambda bb, i: (0, 0)),
        ],
        out_specs=pl.BlockSpec((None, ts, d), lambda bb, i: (bb, i, 0)),
        compiler_params=_params("arbitrary", "arbitrary"),
        name="add_pos",
    )(x, e)


def _mod_spec(layer, k, ctx_row):
    if ctx_row is None:
        return pl.BlockSpec((None, None, None, 1, D_MODEL), lambda b, i: (layer, b, k, 0, 0))
    return pl.BlockSpec((None, None, None, 1, D_MODEL), lambda b, i: (layer, ctx_row, k, 0, 0))


def _layer_spec(shape, layer):
    zeros = (0,) * len(shape)
    return pl.BlockSpec((None,) + tuple(shape), lambda b, i: (layer,) + zeros)


def _inproj_kernel(x_ref, g_ref, sh_ref, sc_ref, w_ref, o_ref):
    h = _rms_mod(x_ref[...], g_ref[...], sh_ref[...], sc_ref[...])
    o_ref[...] = jnp.dot(h.astype(BF16), w_ref[...], preferred_element_type=F32)


def _inproj(x, mod5, g1, w_in, layer, ctx_row, ts):
    b, s, d = x.shape
    return pl.pallas_call(
        _inproj_kernel,
        out_shape=jax.ShapeDtypeStruct((b, s, D_IN), F32),
        grid=(b, s // ts),
        in_specs=[
            pl.BlockSpec((None, ts, d), lambda bb, i: (bb, i, 0)),
            _layer_spec((1, d), layer),
            _mod_spec(layer, 0, ctx_row),
            _mod_spec(layer, 1, ctx_row),
            _layer_spec((d, D_IN), layer),
        ],
        out_specs=pl.BlockSpec((None, ts, D_IN), lambda bb, i: (bb, i, 0)),
        compiler_params=_params("arbitrary", "arbitrary"),
        name="inproj",
    )(x, g1, mod5, mod5, w_in)


def _scan_tile(a_ref, b_ref, out_ref, carry0, n_rows, reverse):
    width = a_ref.shape[-1]
    n_groups = n_rows // SUBLANES
    row = lax.broadcasted_iota(jnp.int32, (SUBLANES, width), 0)

    def group(gi, carry):
        g = (n_groups - 1 - gi) if reverse else gi
        start = pl.multiple_of(g * SUBLANES, SUBLANES)
        a = a_ref[pl.ds(start, SUBLANES), :]
        h = b_ref[pl.ds(start, SUBLANES), :]
        for s in (1, 2, 4):
            if reverse:
                keep = row < SUBLANES - s
                shift = SUBLANES - s
            else:
                keep = row >= s
                shift = s
            h_sh = jnp.where(keep, pltpu.roll(h, shift, 0), 0.0)
            a_sh = jnp.where(keep, pltpu.roll(a, shift, 0), 1.0)
            h = h + a * h_sh
            a = a * a_sh
        h = h + a * carry
        out_ref[pl.ds(start, SUBLANES), :] = h
        edge = h[0:1, :] if reverse else h[SUBLANES - 1:SUBLANES, :]
        return jnp.broadcast_to(edge, (SUBLANES, width))

    carry = lax.fori_loop(0, n_groups, group, jnp.broadcast_to(carry0, (SUBLANES, width)), unroll=2)
    return carry[0:1, :]


def _lru_fwd_kernel(pa_ref, prev_ref, next_ref, cw_ref, cb_ref, wg_ref, bg_ref, lam_ref, h0_ref,
                    hf_ref, ab_ref, hlast_ref, ext_scr, a_scr, b_scr, carry_scr, *, ts):
    i = pl.program_id(1)
    n_t = pl.num_programs(1)

    @pl.when(i == 0)
    def _():
        carry_scr[...] = h0_ref[...]

    ext_scr[0:SUBLANES, :] = jnp.where(i > 0, prev_ref[...], 0.0)
    ext_scr[SUBLANES:SUBLANES + ts, :] = pa_ref[...]
    ext_scr[SUBLANES + ts:, :] = jnp.where(i < n_t - 1, next_ref[...], 0.0)

    cx = jnp.broadcast_to(cb_ref[...], (ts, W_LRU))
    for k in range(LRU_CONV_W):
        off = SUBLANES + k - LRU_PAD_L
        cx = cx + cw_ref[k:k + 1, :] * ext_scr[off:off + ts, :]

    gates = jnp.dot(cx.astype(BF16), wg_ref[...], preferred_element_type=F32) + bg_ref[...]
    neg_lam = -lam_ref[...]
    sp = jnp.maximum(neg_lam, 0.0) + jnp.log1p(jnp.exp(-jnp.abs(neg_lam)))
    lam_c = -LRU_C * sp
    for d in range(2):
        r = _sigmoid(gates[:, (2 * d) * W_LRU:(2 * d + 1) * W_LRU])
        ig = _sigmoid(gates[:, (2 * d + 1) * W_LRU:(2 * d + 2) * W_LRU])
        log_a = lam_c[:, d * W_LRU:(d + 1) * W_LRU] * r
        a = jnp.exp(log_a)
        t = jnp.tanh(log_a)
        bmul = jnp.sqrt(-2.0 * t / (1.0 - t))
        bb = bmul * (ig * cx)
        if d == 0:
            a_scr[...] = a
            b_scr[...] = bb
        else:
            ab_ref[:, :W_LRU] = a
            ab_ref[:, W_LRU:] = bb

    carry = _scan_tile(a_scr, b_scr, hf_ref, carry_scr[...], ts, reverse=False)
    carry_scr[...] = carry
    hlast_ref[...] = carry


def _lru_fwd(px, conv_w, conv_b, wg, bg, lam, h0, layer, ts):
    b, s, _ = px.shape
    n8 = ts // SUBLANES
    last8 = s // SUBLANES - 1
    kernel = functools.partial(_lru_fwd_kernel, ts=ts)
    return pl.pallas_call(
        kernel,
        out_shape=(
            jax.ShapeDtypeStruct((b, s, W_LRU), F32),
            jax.ShapeDtypeStruct((b, s, 2 * W_LRU), F32),
            jax.ShapeDtypeStruct((b, 1, W_LRU), F32),
        ),
        grid=(b, s // ts),
        in_specs=[
            pl.BlockSpec((None, ts, W_LRU), lambda bb, i: (bb, i, 0)),
            pl.BlockSpec((None, SUBLANES, W_LRU), lambda bb, i: (bb, jnp.maximum(i * n8 - 1, 0), 0)),
            pl.BlockSpec((None, SUBLANES, W_LRU), lambda bb, i: (bb, jnp.minimum((i + 1) * n8, last8), 0)),
            _layer_spec((LRU_CONV_W, W_LRU), layer),
            _layer_spec((1, W_LRU), layer),
            _layer_spec((W_LRU, 4 * W_LRU), layer),
            _layer_spec((1, 4 * W_LRU), layer),
            _layer_spec((1, 2 * W_LRU), layer),
            pl.BlockSpec((None, 1, W_LRU), lambda bb, i: (bb, 0, 0)),
        ],
        out_specs=(
            pl.BlockSpec((None, ts, W_LRU), lambda bb, i: (bb, i, 0)),
            pl.BlockSpec((None, ts, 2 * W_LRU), lambda bb, i: (bb, i, 0)),
            pl.BlockSpec((None, 1, W_LRU), lambda bb, i: (bb, 0, 0)),
        ),
        scratch_shapes=[
            pltpu.VMEM((ts + 2 * SUBLANES, W_LRU), F32),
            pltpu.VMEM((ts, W_LRU), F32),
            pltpu.VMEM((ts, W_LRU), F32),
            pltpu.VMEM((1, W_LRU), F32),
        ],
        compiler_params=_params("arbitrary", "arbitrary"),
        name="lru_fwd",
    )(px, px, px, conv_w, conv_b, wg, bg, lam, h0)


def _mix_kernel(x_ref, px_ref, prev_ref, next_ref, hf_ref, ab_ref, h0_ref, gt_ref,
                pw_ref, ps_ref, slg_ref, slb_ref, sw_ref, sb_ref,
                dw_ref, db_ref, dlg_ref, dlb_ref, wo_ref,
                o_ref, hfirst_ref, hb_scr, carry_scr, zext_scr, yext_scr, *, ts, seq):
    i = pl.program_id(1)
    n_t = pl.num_programs(1)
    tile = n_t - 1 - i
    c_gate, c_pool, c_u, c_v, c_cv, c_cg = (W_LRU * k for k in range(1, 7))

    @pl.when(i == 0)
    def _():
        carry_scr[...] = h0_ref[...]

    carry = _scan_tile(ab_ref.at[:, :W_LRU], ab_ref.at[:, W_LRU:], hb_scr, carry_scr[...], ts,
                       reverse=True)
    carry_scr[...] = carry
    hfirst_ref[...] = carry

    y_a = (hf_ref[...] + hb_scr[...]) * _gelu_tanh(px_ref[:, c_gate:c_gate + W_LRU])

    has_prev = tile > 0
    has_next = tile < n_t - 1

    zext_scr[0:HALO, :] = jnp.where(has_prev, prev_ref[:, c_pool:c_pool + W_POOL], 0.0)
    zext_scr[HALO:HALO + ts, :] = px_ref[:, c_pool:c_pool + W_POOL]
    zext_scr[HALO + ts:, :] = jnp.where(has_next, next_ref[:, c_pool:c_pool + W_POOL], 0.0)
    n_ext = ts + 2 * HALO
    z = zext_scr[...]
    t2 = z[1:, :] + z[:-1, :]
    t4 = t2[2:, :] + t2[:-2, :]
    t8 = t4[4:, :] + t4[:-4, :]
    t16 = t8[8:, :] + t8[:-8, :]
    s2 = t2[HALO - 1:HALO - 1 + ts, :]
    s4 = t4[HALO - 3 + 1:HALO - 3 + 1 + ts, :]
    s8 = t8[HALO - 7 + 3:HALO - 7 + 3 + ts, :]
    s16 = t16[HALO - 15 + 7:HALO - 15 + 7 + ts, :]
    lane = lax.broadcasted_iota(jnp.int32, (ts, W_POOL), 1)
    grp = lane // POOL_GROUP
    wsum = jnp.where(grp == 0, s2, jnp.where(grp == 1, s4, jnp.where(grp == 2, s8, s16)))
    win = jnp.where(grp == 0, 2, jnp.where(grp == 1, 4, jnp.where(grp == 2, 8, 16)))
    lo = win // 2
    hi = win - lo - 1
    tpos = tile * ts + lax.broadcasted_iota(jnp.int32, (ts, W_POOL), 0)
    cnt = (jnp.minimum(tpos + hi + 1, seq) - jnp.maximum(tpos - lo, 0)).astype(F32)
    pooled = wsum / cnt - px_ref[:, c_pool:c_pool + W_POOL]
    y_b = jnp.dot(pooled.astype(BF16), pw_ref[...], preferred_element_type=F32) * ps_ref[...]

    vn = _layer_norm(px_ref[:, c_v:c_v + W_SGU], slg_ref[...], slb_ref[...]).astype(BF16)
    head = lax.broadcasted_iota(jnp.int32, (SGU_CHUNK, W_SGU), 1) // SGU_HEAD_DIM
    y_c_parts = []
    for c in range(ts // SGU_CHUNK):
        rows = slice(c * SGU_CHUNK, (c + 1) * SGU_CHUNK)
        zc = jnp.dot(sw_ref[...], vn[rows, :], preferred_element_type=F32)
        zsel = sb_ref[...]
        for h in range(SGU_HEADS):
            zsel = zsel + jnp.where(head == h, zc[h * SGU_CHUNK:(h + 1) * SGU_CHUNK, :], 0.0)
        y_c_parts.append(px_ref[rows, c_u:c_u + W_SGU] * zsel)
    y_c = jnp.concatenate(y_c_parts, axis=0)

    def glu(ref, rows):
        return ref[rows, c_cv:c_cv + W_CONV] * _sigmoid(ref[rows, c_cg:c_cg + W_CONV])

    yext_scr[0:HALO, :] = jnp.where(has_prev, glu(prev_ref, slice(None)), 0.0)
    yext_scr[HALO:HALO + ts, :] = glu(px_ref, slice(None))
    yext_scr[HALO + ts:, :] = jnp.where(has_next, glu(next_ref, slice(None)), 0.0)
    acc = jnp.broadcast_to(db_ref[...], (ts, W_CONV))
    for k in range(CONF_CONV_W):
        off = HALO + k - CONF_PAD
        acc = acc + dw_ref[k:k + 1, :] * yext_scr[off:off + ts, :]
    y_d = _silu(_layer_norm(acc, dlg_ref[...], dlb_ref[...]))

    y = jnp.concatenate([y_a, y_b, y_c, y_d], axis=1).astype(BF16)
    o_ref[...] = x_ref[...] + gt_ref[...] * jnp.dot(y, wo_ref[...], preferred_element_type=F32)


def _mix(x, px, hf, ab, h0b, mod5, pool_bd, pool_scale, sgu_ln_g, sgu_ln_b, sgu_ws, sgu_bias,
         conv_d_w, conv_d_b, conv_ln_g, conv_ln_b, w_out, layer, ctx_row, ts):
    b, s, d = x.shape
    n_t = s // ts
    nh = ts // HALO
    lasth = s // HALO - 1
    kernel = functools.partial(_mix_kernel, ts=ts, seq=s)

    def rev(bb, i):
        return (bb, n_t - 1 - i, 0)

    return pl.pallas_call(
        kernel,
        out_shape=(
            jax.ShapeDtypeStruct((b, s, d), F32),
            jax.ShapeDtypeStruct((b, 1, W_LRU), F32),
        ),
        grid=(b, n_t),
        in_specs=[
            pl.BlockSpec((None, ts, d), rev),
            pl.BlockSpec((None, ts, D_IN), rev),
            pl.BlockSpec((None, HALO, D_IN),
                         lambda bb, i: (bb, jnp.maximum((n_t - 1 - i) * nh - 1, 0), 0)),
            pl.BlockSpec((None, HALO, D_IN),
                         lambda bb, i: (bb, jnp.minimum((n_t - i) * nh, lasth), 0)),
            pl.BlockSpec((None, ts, W_LRU), rev),
            pl.BlockSpec((None, ts, 2 * W_LRU), rev),
            pl.BlockSpec((None, 1, W_LRU), lambda bb, i: (bb, 0, 0)),
            _mod_spec(layer, 2, ctx_row),
            _layer_spec((W_POOL, W_POOL), layer),
            _layer_spec((1, W_POOL), layer),
            _layer_spec((1, W_SGU), layer),
            _layer_spec((1, W_SGU), layer),
            _layer_spec((SGU_HEADS * SGU_CHUNK, SGU_CHUNK), layer),
            _layer_spec((SGU_CHUNK, W_SGU), layer),
            _layer_spec((CONF_CONV_W, W_CONV), layer),
            _layer_spec((1, W_CONV), layer),
            _layer_spec((1, W_CONV), layer),
            _layer_spec((1, W_CONV), layer),
            _layer_spec((d, d), layer),
        ],
        out_specs=(
            pl.BlockSpec((None, ts, d), rev),
            pl.BlockSpec((None, 1, W_LRU), lambda bb, i: (bb, 0, 0)),
        ),
        scratch_shapes=[
            pltpu.VMEM((ts, W_LRU), F32),
            pltpu.VMEM((1, W_LRU), F32),
            pltpu.VMEM((ts + 2 * HALO, W_POOL), F32),
            pltpu.VMEM((ts + 2 * HALO, W_CONV), F32),
        ],
        compiler_params=_params("arbitrary", "arbitrary"),
        name="mix",
    )(x, px, px, px, hf, ab, h0b, mod5, pool_bd, pool_scale, sgu_ln_g, sgu_ln_b, sgu_ws, sgu_bias,
      conv_d_w, conv_d_b, conv_ln_g, conv_ln_b, w_out)


def _mlp_kernel(x_ref, g_ref, sh_ref, sc_ref, gt_ref, w1_ref, w2_ref, gf_ref, o_ref, acc_scr,
                *, final):
    x = x_ref[...]
    h = _rms_mod(x, g_ref[...], sh_ref[...], sc_ref[...]).astype(BF16)
    for j in range(D_FF // FF_CHUNK):
        cols = slice(j * FF_CHUNK, (j + 1) * FF_CHUNK)
        u = jnp.dot(h, w1_ref[:, cols], preferred_element_type=F32)
        u = jnp.square(jnp.maximum(u, 0.0)).astype(BF16)
        part = jnp.dot(u, w2_ref[cols, :], preferred_element_type=F32)
        if j == 0:
            acc_scr[...] = part
        else:
            acc_scr[...] += part
    y = x + gt_ref[...] * acc_scr[...]
    if final:
        ms = jnp.mean(y * y, axis=-1, keepdims=True)
        y = y * lax.rsqrt(ms + EPS) * gf_ref[...]
    o_ref[...] = y


def _mlp(x, mod5, g2, w1, w2, g_final, layer, ctx_row, tm, final):
    b, s, d = x.shape
    return pl.pallas_call(
        functools.partial(_mlp_kernel, final=final),
        out_shape=jax.ShapeDtypeStruct(x.shape, F32),
        grid=(b, s // tm),
        in_specs=[
            pl.BlockSpec((None, tm, d), lambda bb, i: (bb, i, 0)),
            _layer_spec((1, d), layer),
            _mod_spec(layer, 3, ctx_row),
            _mod_spec(layer, 4, ctx_row),
            _mod_spec(layer, 5, ctx_row),
            _layer_spec((d, D_FF), layer),
            _layer_spec((D_FF, d), layer),
            pl.BlockSpec((1, d), lambda bb, i: (0, 0)),
        ],
        out_specs=pl.BlockSpec((None, tm, d), lambda bb, i: (bb, i, 0)),
        scratch_shapes=[pltpu.VMEM((tm, d), F32)],
        compiler_params=_params("arbitrary", "arbitrary"),
        name="mlp",
    )(x, g2, mod5, mod5, mod5, w1, w2, g_final)


def _block_diag(w):
    *lead, h, n, _ = w.shape
    eye = jnp.eye(h, dtype=w.dtype)
    out = jnp.einsum('...hij,hg->...higj', w, eye)
    return out.reshape(*lead, h * n, h * n)


def _tile_rows(s, want):
    t = min(s, want)
    while s % t:
        t //= 2
    return t


def kernel(x, c, ctx, c_ctx, w_mod, b_mod, g_norm1, g_norm2, w_in, conv_a_w, conv_a_b, lru_wa, lru_ba, lru_wx, lru_bx, lru_lambda, pool_w, pool_scale, sgu_ln_g, sgu_ln_b, sgu_w, sgu_b, conv_d_w, conv_d_b, conv_ln_g, conv_ln_b, w_out, w_mlp1, w_mlp2, g_final):
    depth = w_in.shape[0]
    bsz, seq, d = x.shape
    ctx_len = ctx.shape[1]
    assert d == D_MODEL and bsz < MOD_ROWS
    assert seq % SGU_CHUNK == 0 and ctx_len % SGU_CHUNK == 0 and seq % GRID_W == 0
    ctx_row = bsz

    w_in_b = w_in.astype(BF16)
    w_out_b = w_out.astype(BF16)
    w1_b = w_mlp1.astype(BF16)
    w2_b = w_mlp2.astype(BF16)
    wa_bd = _block_diag(lru_wa)
    wx_bd = _block_diag(lru_wx)
    wg = jnp.concatenate([wa_bd[:, 0], wx_bd[:, 0], wa_bd[:, 1], wx_bd[:, 1]], axis=-1).astype(BF16)
    bg = jnp.concatenate([lru_ba[:, 0], lru_bx[:, 0], lru_ba[:, 1], lru_bx[:, 1]], axis=-1)[:, None, :]
    lam = lru_lambda.reshape(depth, 1, 2 * W_LRU)
    pool_bd = _block_diag(pool_w).astype(BF16)
    sgu_ws = sgu_w.reshape(depth, SGU_HEADS * SGU_CHUNK, SGU_CHUNK).astype(BF16)
    sgu_bias = jnp.repeat(jnp.swapaxes(sgu_b, 1, 2), SGU_HEAD_DIM, axis=2)
    row = lambda v: v[:, None, :]

    cs = jnp.zeros((MOD_ROWS, d), F32).at[:bsz].set(c).at[ctx_row].set(c_ctx)
    mod = _modulation(cs, w_mod, b_mod)
    mod5 = mod.reshape(depth, MOD_ROWS, N_MOD, 1, d)

    ts = _tile_rows(seq, 512)
    ts_c = _tile_rows(ctx_len, 512)
    x = _add_pos(x, _pos_table(), ts)
    xc = ctx
    zeros_h = jnp.zeros((bsz, 1, W_LRU), F32)
    g_fin = g_final.reshape(1, d)

    def mix_args(l):
        return (pool_bd, row(pool_scale), row(sgu_ln_g), row(sgu_ln_b), sgu_ws, sgu_bias,
                conv_d_w, row(conv_d_b), row(conv_ln_g), row(conv_ln_b), w_out_b)

    for l in range(depth):
        last = l == depth - 1
        pc = _inproj(xc, mod5, row(g_norm1), w_in_b, l, ctx_row, ts_c)
        hcf, abc, hcf_last = _lru_fwd(pc, conv_a_w, row(conv_a_b), wg, bg, lam, zeros_h, l, ts_c)
        xc1, hcb_first = _mix(xc, pc, hcf, abc, zeros_h, mod5, *mix_args(l), l, ctx_row, ts_c)
        px = _inproj(x, mod5, row(g_norm1), w_in_b, l, None, ts)
        hf, ab, _ = _lru_fwd(px, conv_a_w, row(conv_a_b), wg, bg, lam, hcf_last, l, ts)
        x, _ = _mix(x, px, hf, ab, hcb_first, mod5, *mix_args(l), l, None, ts)
        x = _mlp(x, mod5, row(g_norm2), w1_b, w2_b, g_fin, l, None, ts, last)
        if not last:
            xc = _mlp(xc1, mod5, row(g_norm2), w1_b, w2_b, g_fin, l, ctx_row, ts_c, False)
    return x
```

```python
import functools
import math

import jax
import jax.numpy as jnp
from jax import lax
from jax.experimental import pallas as pl
from jax.experimental.pallas import tpu as pltpu

F32 = jnp.float32
BF16 = jnp.bfloat16

D_MODEL = 1024
GRID_W = 64
W_LRU = 256
W_POOL = 256
W_SGU = 256
W_CONV = 256
LRU_HEADS = 4
LRU_CONV_W = 4
LRU_PAD_L = 2
LRU_C = 8.0
POOL_WINDOWS = (2, 4, 8, 16)
POOL_GROUP = W_POOL // len(POOL_WINDOWS)
SGU_CHUNK = 128
SGU_HEADS = 4
SGU_HEAD_DIM = W_SGU // SGU_HEADS
CONF_CONV_W = 31
CONF_PAD = 15
D_FF = 4 * D_MODEL
D_IN = 2 * W_LRU + W_POOL + 2 * W_SGU + 2 * W_CONV
N_MOD = 6
EPS = 1e-6
POS_BASE = 10000.0

SUBLANES = 8
HALO = 16
VMEM_LIMIT = 56 * 1024 * 1024
MOD_ROWS = 8
FF_CHUNK = 512
CONV_ROWS = 64


def _params(*sem):
    return pltpu.CompilerParams(dimension_semantics=sem, vmem_limit_bytes=VMEM_LIMIT)


def _sigmoid(x):
    return 0.5 * jnp.tanh(0.5 * x) + 0.5


def _silu(x):
    return x * _sigmoid(x)


def _gelu_tanh(x):
    return 0.5 * x * (1.0 + jnp.tanh(0.7978845608028654 * (x + 0.044715 * (x * x * x))))


def _rms_mod(x, g, shift, scale):
    ms = jnp.mean(x * x, axis=-1, keepdims=True)
    y = x * lax.rsqrt(ms + EPS) * g
    return y * (1.0 + scale) + shift


def _layer_norm(x, g, b):
    mu = jnp.mean(x, axis=-1, keepdims=True)
    xc = x - mu
    var = jnp.mean(xc * xc, axis=-1, keepdims=True)
    return xc * lax.rsqrt(var + EPS) * g + b


def _mod_kernel(c_ref, w_ref, b_ref, o_ref):
    s = _silu(c_ref[...]).astype(BF16)
    o_ref[...] = jnp.dot(s, w_ref[...].astype(BF16), preferred_element_type=F32) + b_ref[...]


def _modulation(cs, w_mod, b_mod):
    depth, _, n = w_mod.shape
    tn = 1536
    return pl.pallas_call(
        _mod_kernel,
        out_shape=jax.ShapeDtypeStruct((depth, MOD_ROWS, n), F32),
        grid=(depth, n // tn),
        in_specs=[
            pl.BlockSpec((MOD_ROWS, D_MODEL), lambda l, j: (0, 0)),
            pl.BlockSpec((None, D_MODEL, tn), lambda l, j: (l, 0, j)),
            pl.BlockSpec((None, 1, tn), lambda l, j: (l, 0, j)),
        ],
        out_specs=pl.BlockSpec((None, MOD_ROWS, tn), lambda l, j: (l, 0, j)),
        compiler_params=_params("arbitrary", "arbitrary"),
        name="modulation",
    )(cs, w_mod, b_mod.reshape(depth, 1, n))


def _pos_table_kernel(o_ref):
    quarter = D_MODEL // 4
    pos = lax.broadcasted_iota(jnp.int32, (GRID_W, quarter), 0).astype(F32)
    j = lax.broadcasted_iota(jnp.int32, (GRID_W, quarter), 1).astype(F32)
    omega = 1.0 / jnp.exp(j * (math.log(POS_BASE) / quarter))
    ang = pos * omega
    o_ref[:, :quarter] = jnp.sin(ang)
    o_ref[:, quarter:] = jnp.cos(ang)


def _pos_table():
    return pl.pallas_call(
        _pos_table_kernel,
        out_shape=jax.ShapeDtypeStruct((GRID_W, D_MODEL // 2), F32),
        name="pos_table",
    )()


def _pos_rows(e_ref, tile, ts):
    reps = ts // GRID_W
    col = jnp.concatenate([e_ref[...]] * reps, axis=0)
    r0 = tile * reps
    rows = [jnp.broadcast_to(e_ref[pl.ds(r0 + k, 1), :], (GRID_W, D_MODEL // 2)) for k in range(reps)]
    return jnp.concatenate([jnp.concatenate(rows, axis=0), col], axis=1)


def _mod_spec(layer, k, ctx_row):
    if ctx_row is None:
        return pl.BlockSpec((None, None, None, 1, D_MODEL), lambda b, i: (layer, b, k, 0, 0))
    return pl.BlockSpec((None, None, None, 1, D_MODEL), lambda b, i: (layer, ctx_row, k, 0, 0))


def _layer_spec(shape, layer):
    zeros = (0,) * len(shape)
    return pl.BlockSpec((None,) + tuple(shape), lambda b, i: (layer,) + zeros)


def _pos_spec():
    return pl.BlockSpec((GRID_W, D_MODEL // 2), lambda b, i: (0, 0))


def _inproj_kernel(x_ref, e_ref, g_ref, sh_ref, sc_ref, w_ref, o_ref, *, ts, add_pos):
    x = x_ref[...]
    if add_pos:
        x = x + _pos_rows(e_ref, pl.program_id(1), ts)
    h = _rms_mod(x, g_ref[...], sh_ref[...], sc_ref[...])
    o_ref[...] = jnp.dot(h.astype(BF16), w_ref[...], preferred_element_type=F32)


def _inproj(x, pos_e, mod5, g1, w_in, layer, ctx_row, ts, add_pos):
    b, s, d = x.shape
    return pl.pallas_call(
        functools.partial(_inproj_kernel, ts=ts, add_pos=add_pos),
        out_shape=jax.ShapeDtypeStruct((b, s, D_IN), F32),
        grid=(b, s // ts),
        in_specs=[
            pl.BlockSpec((None, ts, d), lambda bb, i: (bb, i, 0)),
            _pos_spec(),
            _layer_spec((1, d), layer),
            _mod_spec(layer, 0, ctx_row),
            _mod_spec(layer, 1, ctx_row),
            _layer_spec((d, D_IN), layer),
        ],
        out_specs=pl.BlockSpec((None, ts, D_IN), lambda bb, i: (bb, i, 0)),
        compiler_params=_params("arbitrary", "arbitrary"),
        name="inproj",
    )(x, pos_e, g1, mod5, mod5, w_in)


def _scan_tile(a_ref, b_ref, out_ref, carry0, n_rows, reverse):
    width = a_ref.shape[-1]
    n_groups = n_rows // SUBLANES
    row = lax.broadcasted_iota(jnp.int32, (SUBLANES, width), 0)

    def group(gi, carry):
        g = (n_groups - 1 - gi) if reverse else gi
        start = pl.multiple_of(g * SUBLANES, SUBLANES)
        a = a_ref[pl.ds(start, SUBLANES), :]
        h = b_ref[pl.ds(start, SUBLANES), :]
        for s in (1, 2, 4):
            if reverse:
                keep = row < SUBLANES - s
                shift = SUBLANES - s
            else:
                keep = row >= s
                shift = s
            h_sh = jnp.where(keep, pltpu.roll(h, shift, 0), 0.0)
            a_sh = jnp.where(keep, pltpu.roll(a, shift, 0), 1.0)
            h = h + a * h_sh
            a = a * a_sh
        h = h + a * carry
        out_ref[pl.ds(start, SUBLANES), :] = h
        edge = h[0:1, :] if reverse else h[SUBLANES - 1:SUBLANES, :]
        return jnp.broadcast_to(edge, (SUBLANES, width))

    carry = jnp.broadcast_to(carry0, (SUBLANES, width))
    return lax.fori_loop(0, n_groups, group, carry, unroll=2)[0:1, :]


def _lru_fwd_kernel(pa_ref, prev_ref, next_ref, cw_ref, cb_ref, wg_ref, bg_ref, lam_ref, h0_ref,
                    hf_ref, ab_ref, hlast_ref, ext_scr, a_scr, b_scr, carry_scr, *, ts):
    i = pl.program_id(1)
    n_t = pl.num_programs(1)

    @pl.when(i == 0)
    def _():
        carry_scr[...] = h0_ref[...]

    ext_scr[0:SUBLANES, :] = jnp.where(i > 0, prev_ref[...], 0.0)
    ext_scr[SUBLANES:SUBLANES + ts, :] = pa_ref[...]
    ext_scr[SUBLANES + ts:, :] = jnp.where(i < n_t - 1, next_ref[...], 0.0)

    cx = jnp.broadcast_to(cb_ref[...], (ts, W_LRU))
    for k in range(LRU_CONV_W):
        off = SUBLANES + k - LRU_PAD_L
        cx = cx + cw_ref[k:k + 1, :] * ext_scr[off:off + ts, :]

    gates = jnp.dot(cx.astype(BF16), wg_ref[...], preferred_element_type=F32) + bg_ref[...]
    neg_lam = -lam_ref[...]
    sp = jnp.maximum(neg_lam, 0.0) + jnp.log1p(jnp.exp(-jnp.abs(neg_lam)))
    lam_c = -LRU_C * sp
    for d in range(2):
        r = _sigmoid(gates[:, (2 * d) * W_LRU:(2 * d + 1) * W_LRU])
        ig = _sigmoid(gates[:, (2 * d + 1) * W_LRU:(2 * d + 2) * W_LRU])
        log_a = lam_c[:, d * W_LRU:(d + 1) * W_LRU] * r
        a = jnp.exp(log_a)
        t = jnp.tanh(log_a)
        z = -2.0 * t / (1.0 - t)
        bmul = jnp.where(z > 0.0, z * lax.rsqrt(z), 0.0)
        bb = bmul * (ig * cx)
        if d == 0:
            a_scr[...] = a
            b_scr[...] = bb
        else:
            ab_ref[:, :W_LRU] = a
            ab_ref[:, W_LRU:] = bb

    carry = _scan_tile(a_scr, b_scr, hf_ref, carry_scr[...], ts, reverse=False)
    carry_scr[...] = carry
    hlast_ref[...] = carry


def _lru_fwd(px, conv_w, conv_b, wg, bg, lam, h0, layer, ts):
    b, s, _ = px.shape
    n8 = ts // SUBLANES
    last8 = s // SUBLANES - 1
    kernel = functools.partial(_lru_fwd_kernel, ts=ts)
    return pl.pallas_call(
        kernel,
        out_shape=(
            jax.ShapeDtypeStruct((b, s, W_LRU), F32),
            jax.ShapeDtypeStruct((b, s, 2 * W_LRU), F32),
            jax.ShapeDtypeStruct((b, 1, W_LRU), F32),
        ),
        grid=(b, s // ts),
        in_specs=[
            pl.BlockSpec((None, ts, W_LRU), lambda bb, i: (bb, i, 0)),
            pl.BlockSpec((None, SUBLANES, W_LRU), lambda bb, i: (bb, jnp.maximum(i * n8 - 1, 0), 0)),
            pl.BlockSpec((None, SUBLANES, W_LRU), lambda bb, i: (bb, jnp.minimum((i + 1) * n8, last8), 0)),
            _layer_spec((LRU_CONV_W, W_LRU), layer),
            _layer_spec((1, W_LRU), layer),
            _layer_spec((W_LRU, 4 * W_LRU), layer),
            _layer_spec((1, 4 * W_LRU), layer),
            _layer_spec((1, 2 * W_LRU), layer),
            pl.BlockSpec((None, 1, W_LRU), lambda bb, i: (bb, 0, 0)),
        ],
        out_specs=(
            pl.BlockSpec((None, ts, W_LRU), lambda bb, i: (bb, i, 0)),
            pl.BlockSpec((None, ts, 2 * W_LRU), lambda bb, i: (bb, i, 0)),
            pl.BlockSpec((None, 1, W_LRU), lambda bb, i: (bb, 0, 0)),
        ),
        scratch_shapes=[
            pltpu.VMEM((ts + 2 * SUBLANES, W_LRU), F32),
            pltpu.VMEM((ts, W_LRU), F32),
            pltpu.VMEM((ts, W_LRU), F32),
            pltpu.VMEM((1, W_LRU), F32),
        ],
        compiler_params=_params("arbitrary", "arbitrary"),
        name="lru_fwd",
    )(px, px, px, conv_w, conv_b, wg, bg, lam, h0)


def _mix_kernel(x_ref, e_ref, px_ref, prev_ref, next_ref, hf_ref, ab_ref, h0_ref, gt_ref,
                pw_ref, ps_ref, slg_ref, slb_ref, sw_ref, sb_ref,
                dw_ref, db_ref, dlg_ref, dlb_ref, wo_ref,
                o_ref, hfirst_ref,
                hb_scr, carry_scr, zext_scr, yext_scr, shift_scr, ycat_scr, *, ts, seq, add_pos):
    i = pl.program_id(1)
    n_t = pl.num_programs(1)
    tile = n_t - 1 - i
    c_gate, c_pool, c_u, c_v, c_cv, c_cg = (W_LRU * k for k in range(1, 7))
    has_prev = tile > 0
    has_next = tile < n_t - 1

    @pl.when(i == 0)
    def _():
        carry_scr[...] = h0_ref[...]

    carry = _scan_tile(ab_ref.at[:, :W_LRU], ab_ref.at[:, W_LRU:], hb_scr, carry_scr[...], ts,
                       reverse=True)
    carry_scr[...] = carry
    hfirst_ref[...] = carry
    y_a = (hf_ref[...] + hb_scr[...]) * _gelu_tanh(px_ref[:, c_gate:c_gate + W_LRU])
    ycat_scr[:, :W_LRU] = y_a.astype(BF16)

    zext_scr[0:HALO, :] = jnp.where(has_prev, prev_ref[:, c_pool:c_pool + W_POOL], 0.0)
    zext_scr[HALO:HALO + ts, :] = px_ref[:, c_pool:c_pool + W_POOL]
    zext_scr[HALO + ts:, :] = jnp.where(has_next, next_ref[:, c_pool:c_pool + W_POOL], 0.0)
    z = zext_scr[...]
    t2 = z[1:, :] + z[:-1, :]
    t4 = t2[2:, :] + t2[:-2, :]
    t8 = t4[4:, :] + t4[:-4, :]
    t16 = t8[8:, :] + t8[:-8, :]
    s2 = t2[HALO - 1:HALO - 1 + ts, :]
    s4 = t4[HALO - 2:HALO - 2 + ts, :]
    s8 = t8[HALO - 4:HALO - 4 + ts, :]
    s16 = t16[HALO - 8:HALO - 8 + ts, :]
    lane = lax.broadcasted_iota(jnp.int32, (ts, W_POOL), 1)
    grp = lane // POOL_GROUP
    wsum = jnp.where(grp == 0, s2, jnp.where(grp == 1, s4, jnp.where(grp == 2, s8, s16)))
    win = jnp.where(grp == 0, 2, jnp.where(grp == 1, 4, jnp.where(grp == 2, 8, 16)))
    lo = win // 2
    hi = win - lo - 1
    tpos = tile * ts + lax.broadcasted_iota(jnp.int32, (ts, W_POOL), 0)
    cnt = (jnp.minimum(tpos + hi + 1, seq) - jnp.maximum(tpos - lo, 0)).astype(F32)
    pooled = wsum / cnt - px_ref[:, c_pool:c_pool + W_POOL]
    y_b = jnp.dot(pooled.astype(BF16), pw_ref[...], preferred_element_type=F32) * ps_ref[...]
    ycat_scr[:, W_LRU:W_LRU + W_POOL] = y_b.astype(BF16)

    vn = _layer_norm(px_ref[:, c_v:c_v + W_SGU], slg_ref[...], slb_ref[...]).astype(BF16)
    head = lax.broadcasted_iota(jnp.int32, (SGU_CHUNK, W_SGU), 1) // SGU_HEAD_DIM
    for c in range(ts // SGU_CHUNK):
        rows = slice(c * SGU_CHUNK, (c + 1) * SGU_CHUNK)
        zc = jnp.dot(sw_ref[...], vn[rows, :], preferred_element_type=F32)
        zsel = sb_ref[...]
        for h in range(SGU_HEADS):
            zsel = zsel + jnp.where(head == h, zc[h * SGU_CHUNK:(h + 1) * SGU_CHUNK, :], 0.0)
        y_c = px_ref[rows, c_u:c_u + W_SGU] * zsel
        ycat_scr[rows, W_LRU + W_POOL:W_LRU + W_POOL + W_SGU] = y_c.astype(BF16)

    def gated(ref):
        return ref[:, c_cv:c_cv + W_CONV] * _sigmoid(ref[:, c_cg:c_cg + W_CONV])

    yext_scr[0:HALO, :] = jnp.where(has_prev, gated(prev_ref), 0.0)
    yext_scr[HALO:HALO + ts, :] = gated(px_ref)
    yext_scr[HALO + ts:, :] = jnp.where(has_next, gated(next_ref), 0.0)
    n_shift = ts + 2 * HALO - SUBLANES
    for r in range(1, SUBLANES):
        shift_scr[r - 1] = yext_scr[r:r + n_shift, :]
    for blk in range(ts // CONV_ROWS):
        r0 = blk * CONV_ROWS
        acc = jnp.broadcast_to(db_ref[...], (CONV_ROWS, W_CONV))
        for k in range(CONF_CONV_W):
            off = HALO + k - CONF_PAD
            src = r0 + (off // SUBLANES) * SUBLANES
            if off % SUBLANES:
                tap = shift_scr[off % SUBLANES - 1, src:src + CONV_ROWS, :]
            else:
                tap = yext_scr[src:src + CONV_ROWS, :]
            acc = acc + dw_ref[k:k + 1, :] * tap
        y_d = _silu(_layer_norm(acc, dlg_ref[...], dlb_ref[...]))
        ycat_scr[r0:r0 + CONV_ROWS, W_LRU + W_POOL + W_SGU:] = y_d.astype(BF16)

    x = x_ref[...]
    if add_pos:
        x = x + _pos_rows(e_ref, tile, ts)
    o_ref[...] = x + gt_ref[...] * jnp.dot(ycat_scr[...], wo_ref[...], preferred_element_type=F32)


def _mix(x, pos_e, px, hf, ab, h0b, mod5, pool_bd, pool_scale, sgu_ln_g, sgu_ln_b, sgu_ws,
         sgu_bias, conv_d_w, conv_d_b, conv_ln_g, conv_ln_b, w_out, layer, ctx_row, ts, add_pos):
    b, s, d = x.shape
    n_t = s // ts
    nh = ts // HALO
    lasth = s // HALO - 1
    kernel = functools.partial(_mix_kernel, ts=ts, seq=s, add_pos=add_pos)

    def rev(bb, i):
        return (bb, n_t - 1 - i, 0)

    return pl.pallas_call(
        kernel,
        out_shape=(
            jax.ShapeDtypeStruct((b, s, d), F32),
            jax.ShapeDtypeStruct((b, 1, W_LRU), F32),
        ),
        grid=(b, n_t),
        in_specs=[
            pl.BlockSpec((None, ts, d), rev),
            _pos_spec(),
            pl.BlockSpec((None, ts, D_IN), rev),
            pl.BlockSpec((None, HALO, D_IN),
                         lambda bb, i: (bb, jnp.maximum((n_t - 1 - i) * nh - 1, 0), 0)),
            pl.BlockSpec((None, HALO, D_IN),
                         lambda bb, i: (bb, jnp.minimum((n_t - i) * nh, lasth), 0)),
            pl.BlockSpec((None, ts, W_LRU), rev),
            pl.BlockSpec((None, ts, 2 * W_LRU), rev),
            pl.BlockSpec((None, 1, W_LRU), lambda bb, i: (bb, 0, 0)),
            _mod_spec(layer, 2, ctx_row),
            _layer_spec((W_POOL, W_POOL), layer),
            _layer_spec((1, W_POOL), layer),
            _layer_spec((1, W_SGU), layer),
            _layer_spec((1, W_SGU), layer),
            _layer_spec((SGU_HEADS * SGU_CHUNK, SGU_CHUNK), layer),
            _layer_spec((SGU_CHUNK, W_SGU), layer),
            _layer_spec((CONF_CONV_W, W_CONV), layer),
            _layer_spec((1, W_CONV), layer),
            _layer_spec((1, W_CONV), layer),
            _layer_spec((1, W_CONV), layer),
            _layer_spec((d, d), layer),
        ],
        out_specs=(
            pl.BlockSpec((None, ts, d), rev),
            pl.BlockSpec((None, 1, W_LRU), lambda bb, i: (bb, 0, 0)),
        ),
        scratch_shapes=[
            pltpu.VMEM((ts, W_LRU), F32),
            pltpu.VMEM((1, W_LRU), F32),
            pltpu.VMEM((ts + 2 * HALO, W_POOL), F32),
            pltpu.VMEM((ts + 2 * HALO, W_CONV), F32),
            pltpu.VMEM((SUBLANES - 1, ts + 2 * HALO - SUBLANES, W_CONV), F32),
            pltpu.VMEM((ts, d), BF16),
        ],
        compiler_params=_params("arbitrary", "arbitrary"),
        name="mix",
    )(x, pos_e, px, px, px, hf, ab, h0b, mod5, pool_bd, pool_scale, sgu_ln_g, sgu_ln_b, sgu_ws,
      sgu_bias, conv_d_w, conv_d_b, conv_ln_g, conv_ln_b, w_out)


def _bwd_state_kernel(ab_ref, o_ref, hb_scr, *, rows):
    zero = jnp.zeros((1, W_LRU), F32)
    o_ref[...] = _scan_tile(ab_ref.at[:, :W_LRU], ab_ref.at[:, W_LRU:], hb_scr, zero, rows,
                            reverse=True)


def _bwd_state(ab):
    b, s, _ = ab.shape
    return pl.pallas_call(
        functools.partial(_bwd_state_kernel, rows=s),
        out_shape=jax.ShapeDtypeStruct((b, 1, W_LRU), F32),
        grid=(b,),
        in_specs=[pl.BlockSpec((None, s, 2 * W_LRU), lambda bb: (bb, 0, 0))],
        out_specs=pl.BlockSpec((None, 1, W_LRU), lambda bb: (bb, 0, 0)),
        scratch_shapes=[pltpu.VMEM((s, W_LRU), F32)],
        compiler_params=_params("arbitrary"),
        name="bwd_state",
    )(ab)


def _mlp_kernel(x_ref, g_ref, sh_ref, sc_ref, gt_ref, w1_ref, w2_ref, gf_ref, o_ref, acc_scr,
                *, final):
    x = x_ref[...]
    h = _rms_mod(x, g_ref[...], sh_ref[...], sc_ref[...]).astype(BF16)
    for j in range(D_FF // FF_CHUNK):
        cols = slice(j * FF_CHUNK, (j + 1) * FF_CHUNK)
        u = jnp.dot(h, w1_ref[:, cols], preferred_element_type=F32)
        u = jnp.square(jnp.maximum(u, 0.0)).astype(BF16)
        part = jnp.dot(u, w2_ref[cols, :], preferred_element_type=F32)
        if j == 0:
            acc_scr[...] = part
        else:
            acc_scr[...] += part
    y = x + gt_ref[...] * acc_scr[...]
    if final:
        ms = jnp.mean(y * y, axis=-1, keepdims=True)
        y = y * lax.rsqrt(ms + EPS) * gf_ref[...]
    o_ref[...] = y


def _mlp(x, mod5, g2, w1, w2, g_final, layer, ctx_row, tm, final):
    b, s, d = x.shape
    return pl.pallas_call(
        functools.partial(_mlp_kernel, final=final),
        out_shape=jax.ShapeDtypeStruct(x.shape, F32),
        grid=(b, s // tm),
        in_specs=[
            pl.BlockSpec((None, tm, d), lambda bb, i: (bb, i, 0)),
            _layer_spec((1, d), layer),
            _mod_spec(layer, 3, ctx_row),
            _mod_spec(layer, 4, ctx_row),
            _mod_spec(layer, 5, ctx_row),
            _layer_spec((d, D_FF), layer),
            _layer_spec((D_FF, d), layer),
            pl.BlockSpec((1, d), lambda bb, i: (0, 0)),
        ],
        out_specs=pl.BlockSpec((None, tm, d), lambda bb, i: (bb, i, 0)),
        scratch_shapes=[pltpu.VMEM((tm, d), F32)],
        compiler_params=_params("arbitrary", "arbitrary"),
        name="mlp",
    )(x, g2, mod5, mod5, mod5, w1, w2, g_final)


def _block_diag(w):
    *lead, h, n, _ = w.shape
    eye = jnp.eye(h, dtype=w.dtype)
    out = jnp.einsum('...hij,hg->...higj', w, eye)
    return out.reshape(*lead, h * n, h * n)


def _tile_rows(s, want):
    t = min(s, want)
    while s % t:
        t //= 2
    return t


def kernel(x, c, ctx, c_ctx, w_mod, b_mod, g_norm1, g_norm2, w_in, conv_a_w, conv_a_b, lru_wa, lru_ba, lru_wx, lru_bx, lru_lambda, pool_w, pool_scale, sgu_ln_g, sgu_ln_b, sgu_w, sgu_b, conv_d_w, conv_d_b, conv_ln_g, conv_ln_b, w_out, w_mlp1, w_mlp2, g_final):
    depth = w_in.shape[0]
    bsz, seq, d = x.shape
    ctx_len = ctx.shape[1]
    assert d == D_MODEL and bsz < MOD_ROWS
    assert seq % SGU_CHUNK == 0 and ctx_len % SGU_CHUNK == 0
    ctx_row = bsz

    w_in_b = w_in.astype(BF16)
    w_out_b = w_out.astype(BF16)
    w1_b = w_mlp1.astype(BF16)
    w2_b = w_mlp2.astype(BF16)
    wa_bd = _block_diag(lru_wa)
    wx_bd = _block_diag(lru_wx)
    wg = jnp.concatenate([wa_bd[:, 0], wx_bd[:, 0], wa_bd[:, 1], wx_bd[:, 1]], axis=-1).astype(BF16)
    bg = jnp.concatenate([lru_ba[:, 0], lru_bx[:, 0], lru_ba[:, 1], lru_bx[:, 1]], axis=-1)[:, None, :]
    lam = lru_lambda.reshape(depth, 1, 2 * W_LRU)
    pool_bd = _block_diag(pool_w).astype(BF16)
    sgu_ws = sgu_w.reshape(depth, SGU_HEADS * SGU_CHUNK, SGU_CHUNK).astype(BF16)
    sgu_bias = jnp.repeat(jnp.swapaxes(sgu_b, 1, 2), SGU_HEAD_DIM, axis=2)
    row = lambda v: v[:, None, :]

    cs = jnp.zeros((MOD_ROWS, d), F32).at[:bsz].set(c).at[ctx_row].set(c_ctx)
    mod = _modulation(cs, w_mod, b_mod)
    mod5 = mod.reshape(depth, MOD_ROWS, N_MOD, 1, d)

    ts = _tile_rows(seq, 512)
    ts_c = _tile_rows(ctx_len, 512)
    assert ts % GRID_W == 0 and ts % CONV_ROWS == 0 and ts_c % CONV_ROWS == 0
    pos_e = _pos_table()
    xc = ctx
    zeros_h = jnp.zeros((bsz, 1, W_LRU), F32)
    g_fin = g_final.reshape(1, d)
    mix_w = (pool_bd, row(pool_scale), row(sgu_ln_g), row(sgu_ln_b), sgu_ws, sgu_bias,
             conv_d_w, row(conv_d_b), row(conv_ln_g), row(conv_ln_b), w_out_b)

    for l in range(depth):
        last = l == depth - 1
        first = l == 0
        pc = _inproj(xc, pos_e, mod5, row(g_norm1), w_in_b, l, ctx_row, ts_c, False)
        hcf, abc, hcf_last = _lru_fwd(pc, conv_a_w, row(conv_a_b), wg, bg, lam, zeros_h, l, ts_c)
        if last:
            hcb_first = _bwd_state(abc)
        else:
            xc, hcb_first = _mix(xc, pos_e, pc, hcf, abc, zeros_h, mod5, *mix_w, l, ctx_row, ts_c,
                                 False)
            xc = _mlp(xc, mod5, row(g_norm2), w1_b, w2_b, g_fin, l, ctx_row, ts_c, False)
        px = _inproj(x, pos_e, mod5, row(g_norm1), w_in_b, l, None, ts, first)
        hf, ab, _ = _lru_fwd(px, conv_a_w, row(conv_a_b), wg, bg, lam, hcf_last, l, ts)
        x, _ = _mix(x, pos_e, px, hf, ab, hcb_first, mod5, *mix_w, l, None, ts, first)
        x = _mlp(x, mod5, row(g_norm2), w1_b, w2_b, g_fin, l, None, ts, last)
    return x
```

```python
import functools
import math

import jax
import jax.numpy as jnp
from jax import lax
from jax.experimental import pallas as pl
from jax.experimental.pallas import tpu as pltpu

F32 = jnp.float32
BF16 = jnp.bfloat16

D_MODEL = 1024
GRID_W = 64
W_LRU = 256
W_POOL = 256
W_SGU = 256
W_CONV = 256
LRU_HEADS = 4
LRU_CONV_W = 4
LRU_PAD_L = 2
LRU_C = 8.0
POOL_WINDOWS = (2, 4, 8, 16)
POOL_GROUP = W_POOL // len(POOL_WINDOWS)
SGU_CHUNK = 128
SGU_HEADS = 4
SGU_HEAD_DIM = W_SGU // SGU_HEADS
CONF_CONV_W = 31
CONF_PAD = 15
D_FF = 4 * D_MODEL
D_IN = 2 * W_LRU + W_POOL + 2 * W_SGU + 2 * W_CONV
N_MOD = 6
EPS = 1e-6
POS_BASE = 10000.0

SUBLANES = 8
HALO = 16
VMEM_LIMIT = 56 * 1024 * 1024
MOD_ROWS = 8
FF_CHUNK = 512
CONV_ROWS = 64
POOL_ROWS = 128


def _params(*sem):
    return pltpu.CompilerParams(dimension_semantics=sem, vmem_limit_bytes=VMEM_LIMIT)


def _sigmoid(x):
    return 0.5 * jnp.tanh(0.5 * x) + 0.5


def _silu(x):
    return x * _sigmoid(x)


def _gelu_tanh(x):
    return 0.5 * x * (1.0 + jnp.tanh(0.7978845608028654 * (x + 0.044715 * (x * x * x))))


def _rms_mod(x, g, shift, scale):
    ms = jnp.mean(x * x, axis=-1, keepdims=True)
    y = x * lax.rsqrt(ms + EPS) * g
    return y * (1.0 + scale) + shift


def _layer_norm(x, g, b):
    mu = jnp.mean(x, axis=-1, keepdims=True)
    xc = x - mu
    var = jnp.mean(xc * xc, axis=-1, keepdims=True)
    return xc * lax.rsqrt(var + EPS) * g + b


def _mod_kernel(c_ref, w_ref, b_ref, o_ref):
    s = _silu(c_ref[...]).astype(BF16)
    o_ref[...] = jnp.dot(s, w_ref[...].astype(BF16), preferred_element_type=F32) + b_ref[...]


def _modulation(cs, w_mod, b_mod):
    depth, _, n = w_mod.shape
    tn = 1536
    return pl.pallas_call(
        _mod_kernel,
        out_shape=jax.ShapeDtypeStruct((depth, MOD_ROWS, n), F32),
        grid=(depth, n // tn),
        in_specs=[
            pl.BlockSpec((MOD_ROWS, D_MODEL), lambda l, j: (0, 0)),
            pl.BlockSpec((None, D_MODEL, tn), lambda l, j: (l, 0, j)),
            pl.BlockSpec((None, 1, tn), lambda l, j: (l, 0, j)),
        ],
        out_specs=pl.BlockSpec((None, MOD_ROWS, tn), lambda l, j: (l, 0, j)),
        compiler_params=_params("arbitrary", "arbitrary"),
        name="modulation",
    )(cs, w_mod, b_mod.reshape(depth, 1, n))


def _pos_table_kernel(o_ref):
    quarter = D_MODEL // 4
    pos = lax.broadcasted_iota(jnp.int32, (GRID_W, quarter), 0).astype(F32)
    j = lax.broadcasted_iota(jnp.int32, (GRID_W, quarter), 1).astype(F32)
    omega = 1.0 / jnp.exp(j * (math.log(POS_BASE) / quarter))
    ang = pos * omega
    o_ref[:, :quarter] = jnp.sin(ang)
    o_ref[:, quarter:] = jnp.cos(ang)


def _pos_table():
    return pl.pallas_call(
        _pos_table_kernel,
        out_shape=jax.ShapeDtypeStruct((GRID_W, D_MODEL // 2), F32),
        name="pos_table",
    )()


def _pos_rows(e_ref, tile, ts):
    reps = ts // GRID_W
    col = jnp.concatenate([e_ref[...]] * reps, axis=0)
    r0 = tile * reps
    rows = [jnp.broadcast_to(e_ref[pl.ds(r0 + k, 1), :], (GRID_W, D_MODEL // 2)) for k in range(reps)]
    return jnp.concatenate([jnp.concatenate(rows, axis=0), col], axis=1)


def _mod_spec(layer, k, ctx_row):
    if ctx_row is None:
        return pl.BlockSpec((None, None, None, 1, D_MODEL), lambda b, i: (layer, b, k, 0, 0))
    return pl.BlockSpec((None, None, None, 1, D_MODEL), lambda b, i: (layer, ctx_row, k, 0, 0))


def _layer_spec(shape, layer):
    zeros = (0,) * len(shape)
    return pl.BlockSpec((None,) + tuple(shape), lambda b, i: (layer,) + zeros)


def _pos_spec():
    return pl.BlockSpec((GRID_W, D_MODEL // 2), lambda b, i: (0, 0))


def _inproj_kernel(x_ref, e_ref, g_ref, sh_ref, sc_ref, w_ref, o_ref, *, ts, add_pos):
    x = x_ref[...]
    if add_pos:
        x = x + _pos_rows(e_ref, pl.program_id(1), ts)
    h = _rms_mod(x, g_ref[...], sh_ref[...], sc_ref[...])
    o_ref[...] = jnp.dot(h.astype(BF16), w_ref[...], preferred_element_type=F32)


def _inproj(x, pos_e, mod5, g1, w_in, layer, ctx_row, ts, add_pos):
    b, s, d = x.shape
    return pl.pallas_call(
        functools.partial(_inproj_kernel, ts=ts, add_pos=add_pos),
        out_shape=jax.ShapeDtypeStruct((b, s, D_IN), F32),
        grid=(b, s // ts),
        in_specs=[
            pl.BlockSpec((None, ts, d), lambda bb, i: (bb, i, 0)),
            _pos_spec(),
            _layer_spec((1, d), layer),
            _mod_spec(layer, 0, ctx_row),
            _mod_spec(layer, 1, ctx_row),
            _layer_spec((d, D_IN), layer),
        ],
        out_specs=pl.BlockSpec((None, ts, D_IN), lambda bb, i: (bb, i, 0)),
        compiler_params=_params("arbitrary", "arbitrary"),
        name="inproj",
    )(x, pos_e, g1, mod5, mod5, w_in)


def _scan_tile(a_ref, b_ref, out_ref, carry0, n_rows, reverse, unroll=False):
    width = a_ref.shape[-1]
    n_groups = n_rows // SUBLANES
    row = lax.broadcasted_iota(jnp.int32, (SUBLANES, width), 0)

    def group(gi, carry):
        g = (n_groups - 1 - gi) if reverse else gi
        start = g * SUBLANES if unroll else pl.multiple_of(g * SUBLANES, SUBLANES)
        a = a_ref[pl.ds(start, SUBLANES), :]
        h = b_ref[pl.ds(start, SUBLANES), :]
        for s in (1, 2, 4):
            if reverse:
                keep = row < SUBLANES - s
                shift = SUBLANES - s
            else:
                keep = row >= s
                shift = s
            h_sh = jnp.where(keep, pltpu.roll(h, shift, 0), 0.0)
            a_sh = jnp.where(keep, pltpu.roll(a, shift, 0), 1.0)
            h = h + a * h_sh
            a = a * a_sh
        h = h + a * carry
        out_ref[pl.ds(start, SUBLANES), :] = h
        edge = h[0:1, :] if reverse else h[SUBLANES - 1:SUBLANES, :]
        return jnp.broadcast_to(edge, (SUBLANES, width))

    carry = jnp.broadcast_to(carry0, (SUBLANES, width))
    if unroll:
        for gi in range(n_groups):
            carry = group(gi, carry)
        return carry[0:1, :]
    return lax.fori_loop(0, n_groups, group, carry, unroll=2)[0:1, :]


def _lru_fwd_kernel(pa_ref, prev_ref, next_ref, cw_ref, cb_ref, wg_ref, bg_ref, lam_ref, h0_ref,
                    hf_ref, ab_ref, hlast_ref, ext_scr, a_scr, b_scr, carry_scr, *, ts):
    i = pl.program_id(1)
    n_t = pl.num_programs(1)

    @pl.when(i == 0)
    def _():
        carry_scr[...] = h0_ref[...]

    ext_scr[0:SUBLANES, :] = jnp.where(i > 0, prev_ref[...], 0.0)
    ext_scr[SUBLANES:SUBLANES + ts, :] = pa_ref[...]
    ext_scr[SUBLANES + ts:, :] = jnp.where(i < n_t - 1, next_ref[...], 0.0)

    cx = jnp.broadcast_to(cb_ref[...], (ts, W_LRU))
    for k in range(LRU_CONV_W):
        off = SUBLANES + k - LRU_PAD_L
        cx = cx + cw_ref[k:k + 1, :] * ext_scr[off:off + ts, :]

    gates = jnp.dot(cx.astype(BF16), wg_ref[...], preferred_element_type=F32) + bg_ref[...]
    neg_lam = -lam_ref[...]
    sp = jnp.maximum(neg_lam, 0.0) + jnp.log1p(jnp.exp(-jnp.abs(neg_lam)))
    lam_c = -LRU_C * sp
    for d in range(2):
        r = _sigmoid(gates[:, (2 * d) * W_LRU:(2 * d + 1) * W_LRU])
        ig = _sigmoid(gates[:, (2 * d + 1) * W_LRU:(2 * d + 2) * W_LRU])
        log_a = lam_c[:, d * W_LRU:(d + 1) * W_LRU] * r
        a = jnp.exp(log_a)
        t = jnp.tanh(log_a)
        z = -2.0 * t / (1.0 - t)
        bmul = jnp.where(z > 0.0, z * lax.rsqrt(z), 0.0)
        bb = bmul * (ig * cx)
        if d == 0:
            a_scr[...] = a
            b_scr[...] = bb
        else:
            ab_ref[:, :W_LRU] = a
            ab_ref[:, W_LRU:] = bb

    carry = _scan_tile(a_scr, b_scr, hf_ref, carry_scr[...], ts, reverse=False, unroll=True)
    carry_scr[...] = carry
    hlast_ref[...] = carry


def _lru_fwd(px, conv_w, conv_b, wg, bg, lam, h0, layer, ts):
    b, s, _ = px.shape
    n8 = ts // SUBLANES
    last8 = s // SUBLANES - 1
    kernel = functools.partial(_lru_fwd_kernel, ts=ts)
    return pl.pallas_call(
        kernel,
        out_shape=(
            jax.ShapeDtypeStruct((b, s, W_LRU), F32),
            jax.ShapeDtypeStruct((b, s, 2 * W_LRU), F32),
            jax.ShapeDtypeStruct((b, 1, W_LRU), F32),
        ),
        grid=(b, s // ts),
        in_specs=[
            pl.BlockSpec((None, ts, W_LRU), lambda bb, i: (bb, i, 0)),
            pl.BlockSpec((None, SUBLANES, W_LRU), lambda bb, i: (bb, jnp.maximum(i * n8 - 1, 0), 0)),
            pl.BlockSpec((None, SUBLANES, W_LRU), lambda bb, i: (bb, jnp.minimum((i + 1) * n8, last8), 0)),
            _layer_spec((LRU_CONV_W, W_LRU), layer),
            _layer_spec((1, W_LRU), layer),
            _layer_spec((W_LRU, 4 * W_LRU), layer),
            _layer_spec((1, 4 * W_LRU), layer),
            _layer_spec((1, 2 * W_LRU), layer),
            pl.BlockSpec((None, 1, W_LRU), lambda bb, i: (bb, 0, 0)),
        ],
        out_specs=(
            pl.BlockSpec((None, ts, W_LRU), lambda bb, i: (bb, i, 0)),
            pl.BlockSpec((None, ts, 2 * W_LRU), lambda bb, i: (bb, i, 0)),
            pl.BlockSpec((None, 1, W_LRU), lambda bb, i: (bb, 0, 0)),
        ),
        scratch_shapes=[
            pltpu.VMEM((ts + 2 * SUBLANES, W_LRU), F32),
            pltpu.VMEM((ts, W_LRU), F32),
            pltpu.VMEM((ts, W_LRU), F32),
            pltpu.VMEM((1, W_LRU), F32),
        ],
        compiler_params=_params("arbitrary", "arbitrary"),
        name="lru_fwd",
    )(px, px, px, conv_w, conv_b, wg, bg, lam, h0)


def _pool_windows(shape):
    grp = lax.broadcasted_iota(jnp.int32, shape, 1) // POOL_GROUP
    win = jnp.full(shape, POOL_WINDOWS[-1], jnp.int32)
    for g in range(len(POOL_WINDOWS) - 2, -1, -1):
        win = jnp.where(grp == g, POOL_WINDOWS[g], win)
    return win


def _pool_icnt_kernel(o_ref, *, ts, seq):
    win = _pool_windows((ts, W_POOL))
    lo = win // 2
    hi = win - lo - 1
    tpos = pl.program_id(0) * ts + lax.broadcasted_iota(jnp.int32, (ts, W_POOL), 0)
    cnt = (jnp.minimum(tpos + hi + 1, seq) - jnp.maximum(tpos - lo, 0)).astype(F32)
    o_ref[...] = 1.0 / cnt


def _pool_icnt(seq, ts):
    return pl.pallas_call(
        functools.partial(_pool_icnt_kernel, ts=ts, seq=seq),
        out_shape=jax.ShapeDtypeStruct((seq, W_POOL), F32),
        grid=(seq // ts,),
        out_specs=pl.BlockSpec((ts, W_POOL), lambda i: (i, 0)),
        compiler_params=_params("arbitrary"),
        name="pool_icnt",
    )()


def _pool_bands():
    t = jnp.arange(POOL_ROWS)[:, None]
    tok = jnp.arange(POOL_ROWS + 2 * HALO)[None, :] - HALO
    bands = []
    for w in POOL_WINDOWS:
        lo = w // 2
        hi = w - lo - 1
        bands.append((tok >= t - lo) & (tok <= t + hi))
    return jnp.stack(bands).astype(BF16)


def _mix_kernel(x_ref, e_ref, px_ref, prev_ref, next_ref, hf_ref, ab_ref, h0_ref, gt_ref,
                icnt_ref, pb_ref, pw_ref, ps_ref, slg_ref, slb_ref, sw_ref, sb_ref,
                dw_ref, db_ref, dlg_ref, dlb_ref, wo_ref,
                o_ref, hfirst_ref,
                hb_scr, carry_scr, zext_scr, yext_scr, shift_scr, ycat_scr, *, ts, seq, add_pos):
    i = pl.program_id(1)
    n_t = pl.num_programs(1)
    tile = n_t - 1 - i
    c_gate, c_pool, c_u, c_v, c_cv, c_cg = (W_LRU * k for k in range(1, 7))
    has_prev = tile > 0
    has_next = tile < n_t - 1

    @pl.when(i == 0)
    def _():
        carry_scr[...] = h0_ref[...]

    carry = _scan_tile(ab_ref.at[:, :W_LRU], ab_ref.at[:, W_LRU:], hb_scr, carry_scr[...], ts,
                       reverse=True, unroll=True)
    carry_scr[...] = carry
    hfirst_ref[...] = carry
    y_a = (hf_ref[...] + hb_scr[...]) * _gelu_tanh(px_ref[:, c_gate:c_gate + W_LRU])
    ycat_scr[:, :W_LRU] = y_a.astype(BF16)

    zext_scr[0:HALO, :] = jnp.where(has_prev, prev_ref[:, c_pool:c_pool + W_POOL], 0.0).astype(BF16)
    zext_scr[HALO:HALO + ts, :] = px_ref[:, c_pool:c_pool + W_POOL].astype(BF16)
    zext_scr[HALO + ts:, :] = jnp.where(has_next, next_ref[:, c_pool:c_pool + W_POOL], 0.0).astype(BF16)
    grp = lax.broadcasted_iota(jnp.int32, (POOL_ROWS, W_POOL), 1) // POOL_GROUP
    for blk in range(ts // POOL_ROWS):
        r0 = blk * POOL_ROWS
        zb = zext_scr[r0:r0 + POOL_ROWS + 2 * HALO, :]
        wsum = jnp.dot(pb_ref[len(POOL_WINDOWS) - 1], zb, preferred_element_type=F32)
        for g in range(len(POOL_WINDOWS) - 2, -1, -1):
            wsum = jnp.where(grp == g, jnp.dot(pb_ref[g], zb, preferred_element_type=F32), wsum)
        rows = slice(r0, r0 + POOL_ROWS)
        pooled = wsum * icnt_ref[rows, :] - px_ref[rows, c_pool:c_pool + W_POOL]
        y_b = jnp.dot(pooled.astype(BF16), pw_ref[...], preferred_element_type=F32) * ps_ref[...]
        ycat_scr[rows, W_LRU:W_LRU + W_POOL] = y_b.astype(BF16)

    vn = _layer_norm(px_ref[:, c_v:c_v + W_SGU], slg_ref[...], slb_ref[...]).astype(BF16)
    head = lax.broadcasted_iota(jnp.int32, (SGU_CHUNK, W_SGU), 1) // SGU_HEAD_DIM
    zero = jnp.zeros((SGU_CHUNK, W_SGU), BF16)
    for c in range(ts // SGU_CHUNK):
        rows = slice(c * SGU_CHUNK, (c + 1) * SGU_CHUNK)
        vh = jnp.concatenate([jnp.where(head == h, vn[rows, :], zero) for h in range(SGU_HEADS)],
                             axis=0)
        zc = jnp.dot(sw_ref[...], vh, preferred_element_type=F32) + sb_ref[...]
        y_c = px_ref[rows, c_u:c_u + W_SGU] * zc
        ycat_scr[rows, W_LRU + W_POOL:W_LRU + W_POOL + W_SGU] = y_c.astype(BF16)

    def gated(ref):
        return ref[:, c_cv:c_cv + W_CONV] * _sigmoid(ref[:, c_cg:c_cg + W_CONV])

    yext_scr[0:HALO, :] = jnp.where(has_prev, gated(prev_ref), 0.0)
    yext_scr[HALO:HALO + ts, :] = gated(px_ref)
    yext_scr[HALO + ts:, :] = jnp.where(has_next, gated(next_ref), 0.0)
    n_shift = ts + 2 * HALO - SUBLANES
    for r in range(1, SUBLANES):
        shift_scr[r - 1] = yext_scr[r:r + n_shift, :]
    for blk in range(ts // CONV_ROWS):
        r0 = blk * CONV_ROWS
        acc = jnp.broadcast_to(db_ref[...], (CONV_ROWS, W_CONV))
        for k in range(CONF_CONV_W):
            off = HALO + k - CONF_PAD
            src = r0 + (off // SUBLANES) * SUBLANES
            if off % SUBLANES:
                tap = shift_scr[off % SUBLANES - 1, src:src + CONV_ROWS, :]
            else:
                tap = yext_scr[src:src + CONV_ROWS, :]
            acc = acc + dw_ref[k:k + 1, :] * tap
        y_d = _silu(_layer_norm(acc, dlg_ref[...], dlb_ref[...]))
        ycat_scr[r0:r0 + CONV_ROWS, W_LRU + W_POOL + W_SGU:] = y_d.astype(BF16)

    x = x_ref[...]
    if add_pos:
        x = x + _pos_rows(e_ref, tile, ts)
    o_ref[...] = x + gt_ref[...] * jnp.dot(ycat_scr[...], wo_ref[...], preferred_element_type=F32)


def _mix(x, pos_e, px, hf, ab, h0b, mod5, icnt, pool_band, pool_bd, pool_scale, sgu_ln_g, sgu_ln_b,
         sgu_ws, sgu_bias, conv_d_w, conv_d_b, conv_ln_g, conv_ln_b, w_out, layer, ctx_row, ts,
         add_pos):
    b, s, d = x.shape
    n_t = s // ts
    nh = ts // HALO
    lasth = s // HALO - 1
    kernel = functools.partial(_mix_kernel, ts=ts, seq=s, add_pos=add_pos)

    def rev(bb, i):
        return (bb, n_t - 1 - i, 0)

    return pl.pallas_call(
        kernel,
        out_shape=(
            jax.ShapeDtypeStruct((b, s, d), F32),
            jax.ShapeDtypeStruct((b, 1, W_LRU), F32),
        ),
        grid=(b, n_t),
        in_specs=[
            pl.BlockSpec((None, ts, d), rev),
            _pos_spec(),
            pl.BlockSpec((None, ts, D_IN), rev),
            pl.BlockSpec((None, HALO, D_IN),
                         lambda bb, i: (bb, jnp.maximum((n_t - 1 - i) * nh - 1, 0), 0)),
            pl.BlockSpec((None, HALO, D_IN),
                         lambda bb, i: (bb, jnp.minimum((n_t - i) * nh, lasth), 0)),
            pl.BlockSpec((None, ts, W_LRU), rev),
            pl.BlockSpec((None, ts, 2 * W_LRU), rev),
            pl.BlockSpec((None, 1, W_LRU), lambda bb, i: (bb, 0, 0)),
            _mod_spec(layer, 2, ctx_row),
            pl.BlockSpec((ts, W_POOL), lambda bb, i: (n_t - 1 - i, 0)),
            pl.BlockSpec(pool_band.shape, lambda bb, i: (0, 0, 0)),
            _layer_spec((W_POOL, W_POOL), layer),
            _layer_spec((1, W_POOL), layer),
            _layer_spec((1, W_SGU), layer),
            _layer_spec((1, W_SGU), layer),
            _layer_spec((SGU_CHUNK, SGU_HEADS * SGU_CHUNK), layer),
            _layer_spec((SGU_CHUNK, W_SGU), layer),
            _layer_spec((CONF_CONV_W, W_CONV), layer),
            _layer_spec((1, W_CONV), layer),
            _layer_spec((1, W_CONV), layer),
            _layer_spec((1, W_CONV), layer),
            _layer_spec((d, d), layer),
        ],
        out_specs=(
            pl.BlockSpec((None, ts, d), rev),
            pl.BlockSpec((None, 1, W_LRU), lambda bb, i: (bb, 0, 0)),
        ),
        scratch_shapes=[
            pltpu.VMEM((ts, W_LRU), F32),
            pltpu.VMEM((1, W_LRU), F32),
            pltpu.VMEM((ts + 2 * HALO, W_POOL), BF16),
            pltpu.VMEM((ts + 2 * HALO, W_CONV), F32),
            pltpu.VMEM((SUBLANES - 1, ts + 2 * HALO - SUBLANES, W_CONV), F32),
            pltpu.VMEM((ts, d), BF16),
        ],
        compiler_params=_params("arbitrary", "arbitrary"),
        name="mix",
    )(x, pos_e, px, px, px, hf, ab, h0b, mod5, icnt, pool_band, pool_bd, pool_scale, sgu_ln_g,
      sgu_ln_b, sgu_ws, sgu_bias, conv_d_w, conv_d_b, conv_ln_g, conv_ln_b, w_out)


def _bwd_state_kernel(ab_ref, o_ref, hb_scr, *, rows):
    zero = jnp.zeros((1, W_LRU), F32)
    o_ref[...] = _scan_tile(ab_ref.at[:, :W_LRU], ab_ref.at[:, W_LRU:], hb_scr, zero, rows,
                            reverse=True)


def _bwd_state(ab):
    b, s, _ = ab.shape
    return pl.pallas_call(
        functools.partial(_bwd_state_kernel, rows=s),
        out_shape=jax.ShapeDtypeStruct((b, 1, W_LRU), F32),
        grid=(b,),
        in_specs=[pl.BlockSpec((None, s, 2 * W_LRU), lambda bb: (bb, 0, 0))],
        out_specs=pl.BlockSpec((None, 1, W_LRU), lambda bb: (bb, 0, 0)),
        scratch_shapes=[pltpu.VMEM((s, W_LRU), F32)],
        compiler_params=_params("arbitrary"),
        name="bwd_state",
    )(ab)


def _mlp_kernel(x_ref, g_ref, sh_ref, sc_ref, gt_ref, w1_ref, w2_ref, gf_ref, o_ref, acc_scr,
                *, final):
    x = x_ref[...]
    h = _rms_mod(x, g_ref[...], sh_ref[...], sc_ref[...]).astype(BF16)
    for j in range(D_FF // FF_CHUNK):
        cols = slice(j * FF_CHUNK, (j + 1) * FF_CHUNK)
        u = jnp.dot(h, w1_ref[:, cols], preferred_element_type=F32)
        u = jnp.square(jnp.maximum(u, 0.0)).astype(BF16)
        part = jnp.dot(u, w2_ref[cols, :], preferred_element_type=F32)
        if j == 0:
            acc_scr[...] = part
        else:
            acc_scr[...] += part
    y = x + gt_ref[...] * acc_scr[...]
    if final:
        ms = jnp.mean(y * y, axis=-1, keepdims=True)
        y = y * lax.rsqrt(ms + EPS) * gf_ref[...]
    o_ref[...] = y


def _mlp(x, mod5, g2, w1, w2, g_final, layer, ctx_row, tm, final):
    b, s, d = x.shape
    return pl.pallas_call(
        functools.partial(_mlp_kernel, final=final),
        out_shape=jax.ShapeDtypeStruct(x.shape, F32),
        grid=(b, s // tm),
        in_specs=[
            pl.BlockSpec((None, tm, d), lambda bb, i: (bb, i, 0)),
            _layer_spec((1, d), layer),
            _mod_spec(layer, 3, ctx_row),
            _mod_spec(layer, 4, ctx_row),
            _mod_spec(layer, 5, ctx_row),
            _layer_spec((d, D_FF), layer),
            _layer_spec((D_FF, d), layer),
            pl.BlockSpec((1, d), lambda bb, i: (0, 0)),
        ],
        out_specs=pl.BlockSpec((None, tm, d), lambda bb, i: (bb, i, 0)),
        scratch_shapes=[pltpu.VMEM((tm, d), F32)],
        compiler_params=_params("arbitrary", "arbitrary"),
        name="mlp",
    )(x, g2, mod5, mod5, mod5, w1, w2, g_final)


def _block_diag(w):
    *lead, h, n, _ = w.shape
    eye = jnp.eye(h, dtype=w.dtype)
    out = jnp.einsum('...hij,hg->...higj', w, eye)
    return out.reshape(*lead, h * n, h * n)


def _tile_rows(s, want):
    t = min(s, want)
    while s % t:
        t //= 2
    return t


def kernel(x, c, ctx, c_ctx, w_mod, b_mod, g_norm1, g_norm2, w_in, conv_a_w, conv_a_b, lru_wa, lru_ba, lru_wx, lru_bx, lru_lambda, pool_w, pool_scale, sgu_ln_g, sgu_ln_b, sgu_w, sgu_b, conv_d_w, conv_d_b, conv_ln_g, conv_ln_b, w_out, w_mlp1, w_mlp2, g_final):
    depth = w_in.shape[0]
    bsz, seq, d = x.shape
    ctx_len = ctx.shape[1]
    assert d == D_MODEL and bsz < MOD_ROWS
    assert seq % SGU_CHUNK == 0 and ctx_len % SGU_CHUNK == 0
    ctx_row = bsz

    w_in_b = w_in.astype(BF16)
    w_out_b = w_out.astype(BF16)
    w1_b = w_mlp1.astype(BF16)
    w2_b = w_mlp2.astype(BF16)
    wa_bd = _block_diag(lru_wa)
    wx_bd = _block_diag(lru_wx)
    wg = jnp.concatenate([wa_bd[:, 0], wx_bd[:, 0], wa_bd[:, 1], wx_bd[:, 1]], axis=-1).astype(BF16)
    bg = jnp.concatenate([lru_ba[:, 0], lru_bx[:, 0], lru_ba[:, 1], lru_bx[:, 1]], axis=-1)[:, None, :]
    lam = lru_lambda.reshape(depth, 1, 2 * W_LRU)
    pool_bd = _block_diag(pool_w).astype(BF16)
    sgu_ws = jnp.swapaxes(sgu_w, 1, 2).reshape(depth, SGU_CHUNK, SGU_HEADS * SGU_CHUNK).astype(BF16)
    sgu_bias = jnp.repeat(jnp.swapaxes(sgu_b, 1, 2), SGU_HEAD_DIM, axis=2)
    row = lambda v: v[:, None, :]

    cs = jnp.zeros((MOD_ROWS, d), F32).at[:bsz].set(c).at[ctx_row].set(c_ctx)
    mod = _modulation(cs, w_mod, b_mod)
    mod5 = mod.reshape(depth, MOD_ROWS, N_MOD, 1, d)

    ts = _tile_rows(seq, 512)
    ts_c = _tile_rows(ctx_len, 512)
    assert ts % GRID_W == 0 and ts % CONV_ROWS == 0 and ts_c % CONV_ROWS == 0
    pos_e = _pos_table()
    xc = ctx
    zeros_h = jnp.zeros((bsz, 1, W_LRU), F32)
    g_fin = g_final.reshape(1, d)
    mix_w = (_pool_bands(), pool_bd, row(pool_scale), row(sgu_ln_g), row(sgu_ln_b), sgu_ws,
             sgu_bias, conv_d_w, row(conv_d_b), row(conv_ln_g), row(conv_ln_b), w_out_b)
    icnt = _pool_icnt(seq, ts)
    icnt_c = _pool_icnt(ctx_len, ts_c)

    for l in range(depth):
        last = l == depth - 1
        first = l == 0
        pc = _inproj(xc, pos_e, mod5, row(g_norm1), w_in_b, l, ctx_row, ts_c, False)
        hcf, abc, hcf_last = _lru_fwd(pc, conv_a_w, row(conv_a_b), wg, bg, lam, zeros_h, l, ts_c)
        if last:
            hcb_first = _bwd_state(abc)
        else:
            xc, hcb_first = _mix(xc, pos_e, pc, hcf, abc, zeros_h, mod5, icnt_c, *mix_w, l, ctx_row,
                                 ts_c, False)
            xc = _mlp(xc, mod5, row(g_norm2), w1_b, w2_b, g_fin, l, ctx_row, ts_c, False)
        px = _inproj(x, pos_e, mod5, row(g_norm1), w_in_b, l, None, ts, first)
        hf, ab, _ = _lru_fwd(px, conv_a_w, row(conv_a_b), wg, bg, lam, hcf_last, l, ts)
        x, _ = _mix(x, pos_e, px, hf, ab, hcb_first, mod5, icnt, *mix_w, l, None, ts, first)
        x = _mlp(x, mod5, row(g_norm2), w1_b, w2_b, g_fin, l, None, ts, last)
    return x
```

```python
import functools
import math

import jax
import jax.numpy as jnp
from jax import lax
from jax.experimental import pallas as pl
from jax.experimental.pallas import tpu as pltpu

F32 = jnp.float32
BF16 = jnp.bfloat16

D_MODEL = 1024
GRID_W = 64
W_LRU = 256
W_POOL = 256
W_SGU = 256
W_CONV = 256
LRU_HEADS = 4
LRU_CONV_W = 4
LRU_PAD_L = 2
LRU_C = 8.0
POOL_WINDOWS = (2, 4, 8, 16)
POOL_GROUP = W_POOL // len(POOL_WINDOWS)
SGU_CHUNK = 128
SGU_HEADS = 4
SGU_HEAD_DIM = W_SGU // SGU_HEADS
CONF_CONV_W = 31
CONF_PAD = 15
D_FF = 4 * D_MODEL
D_IN = 2 * W_LRU + W_POOL + 2 * W_SGU + 2 * W_CONV
N_MOD = 6
EPS = 1e-6
POS_BASE = 10000.0

SUBLANES = 8
HALO = 16
VMEM_LIMIT = 56 * 1024 * 1024
MOD_ROWS = 8
FF_CHUNK = 512
CONV_ROWS = 64
POOL_ROWS = 128
INPROJ_CHUNKS = 2


def _params(*sem):
    return pltpu.CompilerParams(dimension_semantics=sem, vmem_limit_bytes=VMEM_LIMIT)


def _sigmoid(x):
    return 0.5 * jnp.tanh(0.5 * x) + 0.5


def _silu(x):
    return x * _sigmoid(x)


def _gelu_tanh(x):
    return 0.5 * x * (1.0 + jnp.tanh(0.7978845608028654 * (x + 0.044715 * (x * x * x))))


def _rms_mod(x, g, shift, scale):
    ms = jnp.mean(x * x, axis=-1, keepdims=True)
    y = x * lax.rsqrt(ms + EPS) * g
    return y * (1.0 + scale) + shift


def _layer_norm(x, g, b):
    mu = jnp.mean(x, axis=-1, keepdims=True)
    xc = x - mu
    var = jnp.mean(xc * xc, axis=-1, keepdims=True)
    return xc * lax.rsqrt(var + EPS) * g + b


def _mod_kernel(c_ref, w_ref, b_ref, o_ref):
    s = _silu(c_ref[...]).astype(BF16)
    o_ref[...] = jnp.dot(s, w_ref[...].astype(BF16), preferred_element_type=F32) + b_ref[...]


def _modulation(cs, w_mod, b_mod):
    depth, _, n = w_mod.shape
    tn = 1536
    return pl.pallas_call(
        _mod_kernel,
        out_shape=jax.ShapeDtypeStruct((depth, MOD_ROWS, n), F32),
        grid=(depth, n // tn),
        in_specs=[
            pl.BlockSpec((MOD_ROWS, D_MODEL), lambda l, j: (0, 0)),
            pl.BlockSpec((None, D_MODEL, tn), lambda l, j: (l, 0, j)),
            pl.BlockSpec((None, 1, tn), lambda l, j: (l, 0, j)),
        ],
        out_specs=pl.BlockSpec((None, MOD_ROWS, tn), lambda l, j: (l, 0, j)),
        compiler_params=_params("arbitrary", "arbitrary"),
        name="modulation",
    )(cs, w_mod, b_mod.reshape(depth, 1, n))


def _pos_table_kernel(o_ref):
    quarter = D_MODEL // 4
    pos = lax.broadcasted_iota(jnp.int32, (GRID_W, quarter), 0).astype(F32)
    j = lax.broadcasted_iota(jnp.int32, (GRID_W, quarter), 1).astype(F32)
    omega = 1.0 / jnp.exp(j * (math.log(POS_BASE) / quarter))
    ang = pos * omega
    o_ref[:, :quarter] = jnp.sin(ang)
    o_ref[:, quarter:] = jnp.cos(ang)


def _pos_table():
    return pl.pallas_call(
        _pos_table_kernel,
        out_shape=jax.ShapeDtypeStruct((GRID_W, D_MODEL // 2), F32),
        name="pos_table",
    )()


def _pos_rows(e_ref, tile, ts):
    reps = ts // GRID_W
    col = jnp.concatenate([e_ref[...]] * reps, axis=0)
    r0 = tile * reps
    rows = [jnp.broadcast_to(e_ref[pl.ds(r0 + k, 1), :], (GRID_W, D_MODEL // 2)) for k in range(reps)]
    return jnp.concatenate([jnp.concatenate(rows, axis=0), col], axis=1)


def _mod_spec(layer, k, ctx_row):
    if ctx_row is None:
        return pl.BlockSpec((None, None, None, 1, D_MODEL), lambda b, i: (layer, b, k, 0, 0))
    return pl.BlockSpec((None, None, None, 1, D_MODEL), lambda b, i: (layer, ctx_row, k, 0, 0))


def _layer_spec(shape, layer, single_buffer=False):
    zeros = (0,) * len(shape)
    kwargs = dict(pipeline_mode=pl.Buffered(1)) if single_buffer else {}
    return pl.BlockSpec((None,) + tuple(shape), lambda b, i: (layer,) + zeros, **kwargs)


def _pos_spec():
    return pl.BlockSpec((GRID_W, D_MODEL // 2), lambda b, i: (0, 0))


def _inproj_kernel(x_ref, e_ref, g_ref, sh_ref, sc_ref, w_ref, o_ref, *, ts, add_pos):
    x = x_ref[...]
    if add_pos:
        x = x + _pos_rows(e_ref, pl.program_id(1), ts)
    rows = ts // INPROJ_CHUNKS
    for c in range(INPROJ_CHUNKS):
        sl = slice(c * rows, (c + 1) * rows)
        h = _rms_mod(x[sl, :], g_ref[...], sh_ref[...], sc_ref[...])
        o_ref[sl, :] = jnp.dot(h.astype(BF16), w_ref[...], preferred_element_type=F32)


def _inproj(x, pos_e, mod5, g1, w_in, layer, ctx_row, ts, add_pos):
    b, s, d = x.shape
    return pl.pallas_call(
        functools.partial(_inproj_kernel, ts=ts, add_pos=add_pos),
        out_shape=jax.ShapeDtypeStruct((b, s, D_IN), F32),
        grid=(b, s // ts),
        in_specs=[
            pl.BlockSpec((None, ts, d), lambda bb, i: (bb, i, 0)),
            _pos_spec(),
            _layer_spec((1, d), layer),
            _mod_spec(layer, 0, ctx_row),
            _mod_spec(layer, 1, ctx_row),
            _layer_spec((d, D_IN), layer),
        ],
        out_specs=pl.BlockSpec((None, ts, D_IN), lambda bb, i: (bb, i, 0)),
        compiler_params=_params("arbitrary", "arbitrary"),
        name="inproj",
    )(x, pos_e, g1, mod5, mod5, w_in)


def _scan_tile(a_ref, b_ref, out_ref, carry0, n_rows, reverse, unroll=False):
    width = a_ref.shape[-1]
    n_groups = n_rows // SUBLANES
    row = lax.broadcasted_iota(jnp.int32, (SUBLANES, width), 0)

    def group(gi, carry):
        g = (n_groups - 1 - gi) if reverse else gi
        start = g * SUBLANES if unroll else pl.multiple_of(g * SUBLANES, SUBLANES)
        a = a_ref[pl.ds(start, SUBLANES), :]
        h = b_ref[pl.ds(start, SUBLANES), :]
        for s in (1, 2, 4):
            if reverse:
                keep = row < SUBLANES - s
                shift = SUBLANES - s
            else:
                keep = row >= s
                shift = s
            h_sh = jnp.where(keep, pltpu.roll(h, shift, 0), 0.0)
            a_sh = jnp.where(keep, pltpu.roll(a, shift, 0), 1.0)
            h = h + a * h_sh
            a = a * a_sh
        h = h + a * carry
        out_ref[pl.ds(start, SUBLANES), :] = h
        edge = h[0:1, :] if reverse else h[SUBLANES - 1:SUBLANES, :]
        return jnp.broadcast_to(edge, (SUBLANES, width))

    carry = jnp.broadcast_to(carry0, (SUBLANES, width))
    if unroll:
        for gi in range(n_groups):
            carry = group(gi, carry)
        return carry[0:1, :]
    return lax.fori_loop(0, n_groups, group, carry, unroll=2)[0:1, :]


def _lru_fwd_kernel(pa_ref, prev_ref, next_ref, cw_ref, cb_ref, wg_ref, bg_ref, lam_ref, h0_ref,
                    hf_ref, ab_ref, hlast_ref, ext_scr, a_scr, b_scr, carry_scr, *, ts):
    i = pl.program_id(1)
    n_t = pl.num_programs(1)

    @pl.when(i == 0)
    def _():
        carry_scr[...] = h0_ref[...]

    ext_scr[0:SUBLANES, :] = jnp.where(i > 0, prev_ref[...], 0.0)
    ext_scr[SUBLANES:SUBLANES + ts, :] = pa_ref[...]
    ext_scr[SUBLANES + ts:, :] = jnp.where(i < n_t - 1, next_ref[...], 0.0)

    cx = jnp.broadcast_to(cb_ref[...], (ts, W_LRU))
    for k in range(LRU_CONV_W):
        off = SUBLANES + k - LRU_PAD_L
        cx = cx + cw_ref[k:k + 1, :] * ext_scr[off:off + ts, :]

    gates = jnp.dot(cx.astype(BF16), wg_ref[...], preferred_element_type=F32) + bg_ref[...]
    neg_lam = -lam_ref[...]
    sp = jnp.maximum(neg_lam, 0.0) + jnp.log1p(jnp.exp(-jnp.abs(neg_lam)))
    lam_c = -LRU_C * sp
    for d in range(2):
        r = _sigmoid(gates[:, (2 * d) * W_LRU:(2 * d + 1) * W_LRU])
        ig = _sigmoid(gates[:, (2 * d + 1) * W_LRU:(2 * d + 2) * W_LRU])
        log_a = lam_c[:, d * W_LRU:(d + 1) * W_LRU] * r
        a = jnp.exp(log_a)
        t = jnp.tanh(log_a)
        z = -2.0 * t / (1.0 - t)
        bmul = jnp.where(z > 0.0, z * lax.rsqrt(z), 0.0)
        bb = bmul * (ig * cx)
        if d == 0:
            a_scr[...] = a
            b_scr[...] = bb
        else:
            ab_ref[:, :W_LRU] = a
            ab_ref[:, W_LRU:] = bb

    carry = _scan_tile(a_scr, b_scr, hf_ref, carry_scr[...], ts, reverse=False, unroll=True)
    carry_scr[...] = carry
    hlast_ref[...] = carry


def _lru_fwd(px, conv_w, conv_b, wg, bg, lam, h0, layer, ts):
    b, s, _ = px.shape
    n8 = ts // SUBLANES
    last8 = s // SUBLANES - 1
    kernel = functools.partial(_lru_fwd_kernel, ts=ts)
    return pl.pallas_call(
        kernel,
        out_shape=(
            jax.ShapeDtypeStruct((b, s, W_LRU), F32),
            jax.ShapeDtypeStruct((b, s, 2 * W_LRU), F32),
            jax.ShapeDtypeStruct((b, 1, W_LRU), F32),
        ),
        grid=(b, s // ts),
        in_specs=[
            pl.BlockSpec((None, ts, W_LRU), lambda bb, i: (bb, i, 0)),
            pl.BlockSpec((None, SUBLANES, W_LRU), lambda bb, i: (bb, jnp.maximum(i * n8 - 1, 0), 0)),
            pl.BlockSpec((None, SUBLANES, W_LRU), lambda bb, i: (bb, jnp.minimum((i + 1) * n8, last8), 0)),
            _layer_spec((LRU_CONV_W, W_LRU), layer),
            _layer_spec((1, W_LRU), layer),
            _layer_spec((W_LRU, 4 * W_LRU), layer),
            _layer_spec((1, 4 * W_LRU), layer),
            _layer_spec((1, 2 * W_LRU), layer),
            pl.BlockSpec((None, 1, W_LRU), lambda bb, i: (bb, 0, 0)),
        ],
        out_specs=(
            pl.BlockSpec((None, ts, W_LRU), lambda bb, i: (bb, i, 0)),
            pl.BlockSpec((None, ts, 2 * W_LRU), lambda bb, i: (bb, i, 0)),
            pl.BlockSpec((None, 1, W_LRU), lambda bb, i: (bb, 0, 0)),
        ),
        scratch_shapes=[
            pltpu.VMEM((ts + 2 * SUBLANES, W_LRU), F32),
            pltpu.VMEM((ts, W_LRU), F32),
            pltpu.VMEM((ts, W_LRU), F32),
            pltpu.VMEM((1, W_LRU), F32),
        ],
        compiler_params=_params("arbitrary", "arbitrary"),
        name="lru_fwd",
    )(px, px, px, conv_w, conv_b, wg, bg, lam, h0)


def _pool_windows(shape):
    grp = lax.broadcasted_iota(jnp.int32, shape, 1) // POOL_GROUP
    win = jnp.full(shape, POOL_WINDOWS[-1], jnp.int32)
    for g in range(len(POOL_WINDOWS) - 2, -1, -1):
        win = jnp.where(grp == g, POOL_WINDOWS[g], win)
    return win


def _pool_icnt_kernel(o_ref, *, ts, seq):
    win = _pool_windows((ts, W_POOL))
    lo = win // 2
    hi = win - lo - 1
    tpos = pl.program_id(0) * ts + lax.broadcasted_iota(jnp.int32, (ts, W_POOL), 0)
    cnt = (jnp.minimum(tpos + hi + 1, seq) - jnp.maximum(tpos - lo, 0)).astype(F32)
    o_ref[...] = 1.0 / cnt


def _pool_icnt(seq, ts):
    return pl.pallas_call(
        functools.partial(_pool_icnt_kernel, ts=ts, seq=seq),
        out_shape=jax.ShapeDtypeStruct((seq, W_POOL), F32),
        grid=(seq // ts,),
        out_specs=pl.BlockSpec((ts, W_POOL), lambda i: (i, 0)),
        compiler_params=_params("arbitrary"),
        name="pool_icnt",
    )()


def _pool_bands():
    t = jnp.arange(POOL_ROWS)[:, None]
    tok = jnp.arange(POOL_ROWS + 2 * HALO)[None, :] - HALO
    bands = []
    for w in POOL_WINDOWS:
        lo = w // 2
        hi = w - lo - 1
        bands.append((tok >= t - lo) & (tok <= t + hi))
    return jnp.stack(bands).astype(BF16)


def _mix_kernel(x_ref, e_ref, px_ref, prev_ref, next_ref, hf_ref, ab_ref, h0_ref, gt_ref,
                icnt_ref, pb_ref, pw_ref, ps_ref, slg_ref, slb_ref, sw_ref, sb_ref,
                dw_ref, db_ref, dlg_ref, dlb_ref, wo_ref,
                o_ref, hfirst_ref,
                hb_scr, carry_scr, zext_scr, yext_scr, shift_scr, ycat_scr, *, ts, seq, add_pos):
    i = pl.program_id(1)
    n_t = pl.num_programs(1)
    tile = n_t - 1 - i
    c_gate, c_pool, c_u, c_v, c_cv, c_cg = (W_LRU * k for k in range(1, 7))
    has_prev = tile > 0
    has_next = tile < n_t - 1

    @pl.when(i == 0)
    def _():
        carry_scr[...] = h0_ref[...]

    carry = _scan_tile(ab_ref.at[:, :W_LRU], ab_ref.at[:, W_LRU:], hb_scr, carry_scr[...], ts,
                       reverse=True, unroll=True)
    carry_scr[...] = carry
    hfirst_ref[...] = carry
    y_a = (hf_ref[...] + hb_scr[...]) * _gelu_tanh(px_ref[:, c_gate:c_gate + W_LRU])
    ycat_scr[:, :W_LRU] = y_a.astype(BF16)

    zext_scr[0:HALO, :] = jnp.where(has_prev, prev_ref[:, c_pool:c_pool + W_POOL], 0.0).astype(BF16)
    zext_scr[HALO:HALO + ts, :] = px_ref[:, c_pool:c_pool + W_POOL].astype(BF16)
    zext_scr[HALO + ts:, :] = jnp.where(has_next, next_ref[:, c_pool:c_pool + W_POOL], 0.0).astype(BF16)
    grp = lax.broadcasted_iota(jnp.int32, (POOL_ROWS, W_POOL), 1) // POOL_GROUP
    for blk in range(ts // POOL_ROWS):
        r0 = blk * POOL_ROWS
        zb = zext_scr[r0:r0 + POOL_ROWS + 2 * HALO, :]
        wsum = jnp.dot(pb_ref[len(POOL_WINDOWS) - 1], zb, preferred_element_type=F32)
        for g in range(len(POOL_WINDOWS) - 2, -1, -1):
            wsum = jnp.where(grp == g, jnp.dot(pb_ref[g], zb, preferred_element_type=F32), wsum)
        rows = slice(r0, r0 + POOL_ROWS)
        pooled = wsum * icnt_ref[rows, :] - px_ref[rows, c_pool:c_pool + W_POOL]
        y_b = jnp.dot(pooled.astype(BF16), pw_ref[...], preferred_element_type=F32) * ps_ref[...]
        ycat_scr[rows, W_LRU:W_LRU + W_POOL] = y_b.astype(BF16)

    vn = _layer_norm(px_ref[:, c_v:c_v + W_SGU], slg_ref[...], slb_ref[...]).astype(BF16)
    head = lax.broadcasted_iota(jnp.int32, (SGU_CHUNK, W_SGU), 1) // SGU_HEAD_DIM
    zero = jnp.zeros((SGU_CHUNK, W_SGU), BF16)
    for c in range(ts // SGU_CHUNK):
        rows = slice(c * SGU_CHUNK, (c + 1) * SGU_CHUNK)
        vh = jnp.concatenate([jnp.where(head == h, vn[rows, :], zero) for h in range(SGU_HEADS)],
                             axis=0)
        zc = jnp.dot(sw_ref[...], vh, preferred_element_type=F32) + sb_ref[...]
        y_c = px_ref[rows, c_u:c_u + W_SGU] * zc
        ycat_scr[rows, W_LRU + W_POOL:W_LRU + W_POOL + W_SGU] = y_c.astype(BF16)

    def gated(ref):
        return ref[:, c_cv:c_cv + W_CONV] * _sigmoid(ref[:, c_cg:c_cg + W_CONV])

    yext_scr[0:HALO, :] = jnp.where(has_prev, gated(prev_ref), 0.0)
    yext_scr[HALO:HALO + ts, :] = gated(px_ref)
    yext_scr[HALO + ts:, :] = jnp.where(has_next, gated(next_ref), 0.0)
    n_shift = ts + 2 * HALO - SUBLANES
    for r in range(1, SUBLANES):
        shift_scr[r - 1] = yext_scr[r:r + n_shift, :]
    for blk in range(ts // CONV_ROWS):
        r0 = blk * CONV_ROWS
        acc = jnp.broadcast_to(db_ref[...], (CONV_ROWS, W_CONV))
        for k in range(CONF_CONV_W):
            off = HALO + k - CONF_PAD
            src = r0 + (off // SUBLANES) * SUBLANES
            if off % SUBLANES:
                tap = shift_scr[off % SUBLANES - 1, src:src + CONV_ROWS, :]
            else:
                tap = yext_scr[src:src + CONV_ROWS, :]
            acc = acc + dw_ref[k:k + 1, :] * tap
        y_d = _silu(_layer_norm(acc, dlg_ref[...], dlb_ref[...]))
        ycat_scr[r0:r0 + CONV_ROWS, W_LRU + W_POOL + W_SGU:] = y_d.astype(BF16)

    x = x_ref[...]
    if add_pos:
        x = x + _pos_rows(e_ref, tile, ts)
    o_ref[...] = x + gt_ref[...] * jnp.dot(ycat_scr[...], wo_ref[...], preferred_element_type=F32)


def _mix(x, pos_e, px, hf, ab, h0b, mod5, icnt, pool_band, pool_bd, pool_scale, sgu_ln_g, sgu_ln_b,
         sgu_ws, sgu_bias, conv_d_w, conv_d_b, conv_ln_g, conv_ln_b, w_out, layer, ctx_row, ts,
         add_pos):
    b, s, d = x.shape
    n_t = s // ts
    nh = ts // HALO
    lasth = s // HALO - 1
    kernel = functools.partial(_mix_kernel, ts=ts, seq=s, add_pos=add_pos)

    def rev(bb, i):
        return (bb, n_t - 1 - i, 0)

    return pl.pallas_call(
        kernel,
        out_shape=(
            jax.ShapeDtypeStruct((b, s, d), F32),
            jax.ShapeDtypeStruct((b, 1, W_LRU), F32),
        ),
        grid=(b, n_t),
        in_specs=[
            pl.BlockSpec((None, ts, d), rev),
            _pos_spec(),
            pl.BlockSpec((None, ts, D_IN), rev),
            pl.BlockSpec((None, HALO, D_IN),
                         lambda bb, i: (bb, jnp.maximum((n_t - 1 - i) * nh - 1, 0), 0)),
            pl.BlockSpec((None, HALO, D_IN),
                         lambda bb, i: (bb, jnp.minimum((n_t - i) * nh, lasth), 0)),
            pl.BlockSpec((None, ts, W_LRU), rev),
            pl.BlockSpec((None, ts, 2 * W_LRU), rev),
            pl.BlockSpec((None, 1, W_LRU), lambda bb, i: (bb, 0, 0)),
            _mod_spec(layer, 2, ctx_row),
            pl.BlockSpec((ts, W_POOL), lambda bb, i: (n_t - 1 - i, 0)),
            pl.BlockSpec(pool_band.shape, lambda bb, i: (0, 0, 0)),
            _layer_spec((W_POOL, W_POOL), layer),
            _layer_spec((1, W_POOL), layer),
            _layer_spec((1, W_SGU), layer),
            _layer_spec((1, W_SGU), layer),
            _layer_spec((SGU_CHUNK, SGU_HEADS * SGU_CHUNK), layer),
            _layer_spec((SGU_CHUNK, W_SGU), layer),
            _layer_spec((CONF_CONV_W, W_CONV), layer),
            _layer_spec((1, W_CONV), layer),
            _layer_spec((1, W_CONV), layer),
            _layer_spec((1, W_CONV), layer),
            _layer_spec((d, d), layer),
        ],
        out_specs=(
            pl.BlockSpec((None, ts, d), rev),
            pl.BlockSpec((None, 1, W_LRU), lambda bb, i: (bb, 0, 0)),
        ),
        scratch_shapes=[
            pltpu.VMEM((ts, W_LRU), F32),
            pltpu.VMEM((1, W_LRU), F32),
            pltpu.VMEM((ts + 2 * HALO, W_POOL), BF16),
            pltpu.VMEM((ts + 2 * HALO, W_CONV), F32),
            pltpu.VMEM((SUBLANES - 1, ts + 2 * HALO - SUBLANES, W_CONV), F32),
            pltpu.VMEM((ts, d), BF16),
        ],
        compiler_params=_params("arbitrary", "arbitrary"),
        name="mix",
    )(x, pos_e, px, px, px, hf, ab, h0b, mod5, icnt, pool_band, pool_bd, pool_scale, sgu_ln_g,
      sgu_ln_b, sgu_ws, sgu_bias, conv_d_w, conv_d_b, conv_ln_g, conv_ln_b, w_out)


def _bwd_state_kernel(ab_ref, o_ref, hb_scr, *, rows):
    zero = jnp.zeros((1, W_LRU), F32)
    o_ref[...] = _scan_tile(ab_ref.at[:, :W_LRU], ab_ref.at[:, W_LRU:], hb_scr, zero, rows,
                            reverse=True)


def _bwd_state(ab):
    b, s, _ = ab.shape
    return pl.pallas_call(
        functools.partial(_bwd_state_kernel, rows=s),
        out_shape=jax.ShapeDtypeStruct((b, 1, W_LRU), F32),
        grid=(b,),
        in_specs=[pl.BlockSpec((None, s, 2 * W_LRU), lambda bb: (bb, 0, 0))],
        out_specs=pl.BlockSpec((None, 1, W_LRU), lambda bb: (bb, 0, 0)),
        scratch_shapes=[pltpu.VMEM((s, W_LRU), F32)],
        compiler_params=_params("arbitrary"),
        name="bwd_state",
    )(ab)


def _mlp_kernel(x_ref, g_ref, sh_ref, sc_ref, gt_ref, w1_ref, w2_ref, gf_ref, o_ref, acc_scr,
                *, final):
    x = x_ref[...]
    h = _rms_mod(x, g_ref[...], sh_ref[...], sc_ref[...]).astype(BF16)
    for j in range(D_FF // FF_CHUNK):
        cols = slice(j * FF_CHUNK, (j + 1) * FF_CHUNK)
        u = jnp.dot(h, w1_ref[:, cols], preferred_element_type=F32)
        u = jnp.square(jnp.maximum(u, 0.0)).astype(BF16)
        part = jnp.dot(u, w2_ref[cols, :], preferred_element_type=F32)
        if j == 0:
            acc_scr[...] = part
        else:
            acc_scr[...] += part
    y = x + gt_ref[...] * acc_scr[...]
    if final:
        ms = jnp.mean(y * y, axis=-1, keepdims=True)
        y = y * lax.rsqrt(ms + EPS) * gf_ref[...]
    o_ref[...] = y


def _mlp(x, mod5, g2, w1, w2, g_final, layer, ctx_row, tm, final):
    b, s, d = x.shape
    return pl.pallas_call(
        functools.partial(_mlp_kernel, final=final),
        out_shape=jax.ShapeDtypeStruct(x.shape, F32),
        grid=(b, s // tm),
        in_specs=[
            pl.BlockSpec((None, tm, d), lambda bb, i: (bb, i, 0)),
            _layer_spec((1, d), layer),
            _mod_spec(layer, 3, ctx_row),
            _mod_spec(layer, 4, ctx_row),
            _mod_spec(layer, 5, ctx_row),
            _layer_spec((d, D_FF), layer, single_buffer=True),
            _layer_spec((D_FF, d), layer, single_buffer=True),
            pl.BlockSpec((1, d), lambda bb, i: (0, 0)),
        ],
        out_specs=pl.BlockSpec((None, tm, d), lambda bb, i: (bb, i, 0)),
        scratch_shapes=[pltpu.VMEM((tm, d), F32)],
        compiler_params=_params("arbitrary", "arbitrary"),
        name="mlp",
    )(x, g2, mod5, mod5, mod5, w1, w2, g_final)


def _block_diag(w):
    *lead, h, n, _ = w.shape
    eye = jnp.eye(h, dtype=w.dtype)
    out = jnp.einsum('...hij,hg->...higj', w, eye)
    return out.reshape(*lead, h * n, h * n)


def _tile_rows(s, want):
    t = min(s, want)
    while s % t:
        t //= 2
    return t


def kernel(x, c, ctx, c_ctx, w_mod, b_mod, g_norm1, g_norm2, w_in, conv_a_w, conv_a_b, lru_wa, lru_ba, lru_wx, lru_bx, lru_lambda, pool_w, pool_scale, sgu_ln_g, sgu_ln_b, sgu_w, sgu_b, conv_d_w, conv_d_b, conv_ln_g, conv_ln_b, w_out, w_mlp1, w_mlp2, g_final):
    depth = w_in.shape[0]
    bsz, seq, d = x.shape
    ctx_len = ctx.shape[1]
    assert d == D_MODEL and bsz < MOD_ROWS
    assert seq % SGU_CHUNK == 0 and ctx_len % SGU_CHUNK == 0
    ctx_row = bsz

    w_in_b = w_in.astype(BF16)
    w_out_b = w_out.astype(BF16)
    w1_b = w_mlp1.astype(BF16)
    w2_b = w_mlp2.astype(BF16)
    wa_bd = _block_diag(lru_wa)
    wx_bd = _block_diag(lru_wx)
    wg = jnp.concatenate([wa_bd[:, 0], wx_bd[:, 0], wa_bd[:, 1], wx_bd[:, 1]], axis=-1).astype(BF16)
    bg = jnp.concatenate([lru_ba[:, 0], lru_bx[:, 0], lru_ba[:, 1], lru_bx[:, 1]], axis=-1)[:, None, :]
    lam = lru_lambda.reshape(depth, 1, 2 * W_LRU)
    pool_bd = _block_diag(pool_w).astype(BF16)
    sgu_ws = jnp.swapaxes(sgu_w, 1, 2).reshape(depth, SGU_CHUNK, SGU_HEADS * SGU_CHUNK).astype(BF16)
    sgu_bias = jnp.repeat(jnp.swapaxes(sgu_b, 1, 2), SGU_HEAD_DIM, axis=2)
    row = lambda v: v[:, None, :]

    cs = jnp.zeros((MOD_ROWS, d), F32).at[:bsz].set(c).at[ctx_row].set(c_ctx)
    mod = _modulation(cs, w_mod, b_mod)
    mod5 = mod.reshape(depth, MOD_ROWS, N_MOD, 1, d)

    ts = _tile_rows(seq, 512)
    ts_c = _tile_rows(ctx_len, 512)
    tm = _tile_rows(seq, 1024)
    tm_c = _tile_rows(bsz * ctx_len, 1024)
    assert ts % GRID_W == 0 and ts % CONV_ROWS == 0 and ts_c % CONV_ROWS == 0
    pos_e = _pos_table()
    xc = ctx
    zeros_h = jnp.zeros((bsz, 1, W_LRU), F32)
    g_fin = g_final.reshape(1, d)
    mix_w = (_pool_bands(), pool_bd, row(pool_scale), row(sgu_ln_g), row(sgu_ln_b), sgu_ws,
             sgu_bias, conv_d_w, row(conv_d_b), row(conv_ln_g), row(conv_ln_b), w_out_b)
    icnt = _pool_icnt(seq, ts)
    icnt_c = _pool_icnt(ctx_len, ts_c)

    for l in range(depth):
        last = l == depth - 1
        first = l == 0
        pc = _inproj(xc.reshape(1, -1, d), pos_e, mod5, row(g_norm1), w_in_b, l, ctx_row, tm_c, False)
        pc = pc.reshape(bsz, ctx_len, D_IN)
        hcf, abc, hcf_last = _lru_fwd(pc, conv_a_w, row(conv_a_b), wg, bg, lam, zeros_h, l, ts_c)
        if last:
            hcb_first = _bwd_state(abc)
        else:
            xc, hcb_first = _mix(xc, pos_e, pc, hcf, abc, zeros_h, mod5, icnt_c, *mix_w, l, ctx_row,
                                 ts_c, False)
            xc = _mlp(xc.reshape(1, -1, d), mod5, row(g_norm2), w1_b, w2_b, g_fin, l, ctx_row, tm_c,
                      False).reshape(bsz, ctx_len, d)
        px = _inproj(x, pos_e, mod5, row(g_norm1), w_in_b, l, None, ts, first)
        hf, ab, _ = _lru_fwd(px, conv_a_w, row(conv_a_b), wg, bg, lam, hcf_last, l, ts)
        x, _ = _mix(x, pos_e, px, hf, ab, hcb_first, mod5, icnt, *mix_w, l, None, ts, first)
        x = _mlp(x, mod5, row(g_norm2), w1_b, w2_b, g_fin, l, None, tm, last)
    return x
```

```python
import functools
import math

import jax
import jax.numpy as jnp
from jax import lax
from jax.experimental import pallas as pl
from jax.experimental.pallas import tpu as pltpu

F32 = jnp.float32
BF16 = jnp.bfloat16

D_MODEL = 1024
GRID_W = 64
W_LRU = 256
W_POOL = 256
W_SGU = 256
W_CONV = 256
LRU_HEADS = 4
LRU_CONV_W = 4
LRU_PAD_L = 2
LRU_C = 8.0
POOL_WINDOWS = (2, 4, 8, 16)
POOL_GROUP = W_POOL // len(POOL_WINDOWS)
SGU_CHUNK = 128
SGU_HEADS = 4
SGU_HEAD_DIM = W_SGU // SGU_HEADS
CONF_CONV_W = 31
CONF_PAD = 15
D_FF = 4 * D_MODEL
D_IN = 2 * W_LRU + W_POOL + 2 * W_SGU + 2 * W_CONV
N_MOD = 6
EPS = 1e-6
POS_BASE = 10000.0

SUBLANES = 8
HALO = 16
VMEM_LIMIT = 56 * 1024 * 1024
MOD_ROWS = 8
FF_CHUNK = 512
CONV_ROWS = 64
POOL_ROWS = 128
INPROJ_CHUNKS = 2


def _params(*sem):
    return pltpu.CompilerParams(dimension_semantics=sem, vmem_limit_bytes=VMEM_LIMIT)


def _sigmoid(x):
    return 0.5 * jnp.tanh(0.5 * x) + 0.5


def _silu(x):
    return x * _sigmoid(x)


def _gelu_tanh(x):
    return 0.5 * x * (1.0 + jnp.tanh(0.7978845608028654 * (x + 0.044715 * (x * x * x))))


def _rms_mod(x, g, shift, scale):
    ms = jnp.mean(x * x, axis=-1, keepdims=True)
    y = x * lax.rsqrt(ms + EPS) * g
    return y * (1.0 + scale) + shift


def _layer_norm(x, g, b):
    mu = jnp.mean(x, axis=-1, keepdims=True)
    xc = x - mu
    var = jnp.mean(xc * xc, axis=-1, keepdims=True)
    return xc * lax.rsqrt(var + EPS) * g + b


def _mod_kernel(c_ref, w_ref, b_ref, o_ref):
    s = _silu(c_ref[...]).astype(BF16)
    o_ref[...] = jnp.dot(s, w_ref[...].astype(BF16), preferred_element_type=F32) + b_ref[...]


def _modulation(cs, w_mod, b_mod):
    depth, _, n = w_mod.shape
    tn = 1536
    return pl.pallas_call(
        _mod_kernel,
        out_shape=jax.ShapeDtypeStruct((depth, MOD_ROWS, n), F32),
        grid=(depth, n // tn),
        in_specs=[
            pl.BlockSpec((MOD_ROWS, D_MODEL), lambda l, j: (0, 0)),
            pl.BlockSpec((None, D_MODEL, tn), lambda l, j: (l, 0, j)),
            pl.BlockSpec((None, 1, tn), lambda l, j: (l, 0, j)),
        ],
        out_specs=pl.BlockSpec((None, MOD_ROWS, tn), lambda l, j: (l, 0, j)),
        compiler_params=_params("arbitrary", "arbitrary"),
        name="modulation",
    )(cs, w_mod, b_mod.reshape(depth, 1, n))


def _pos_table_kernel(o_ref):
    quarter = D_MODEL // 4
    pos = lax.broadcasted_iota(jnp.int32, (GRID_W, quarter), 0).astype(F32)
    j = lax.broadcasted_iota(jnp.int32, (GRID_W, quarter), 1).astype(F32)
    omega = 1.0 / jnp.exp(j * (math.log(POS_BASE) / quarter))
    ang = pos * omega
    o_ref[:, :quarter] = jnp.sin(ang)
    o_ref[:, quarter:] = jnp.cos(ang)


def _pos_table():
    return pl.pallas_call(
        _pos_table_kernel,
        out_shape=jax.ShapeDtypeStruct((GRID_W, D_MODEL // 2), F32),
        name="pos_table",
    )()


def _pos_rows(e_ref, tile, ts):
    reps = ts // GRID_W
    col = jnp.concatenate([e_ref[...]] * reps, axis=0)
    r0 = tile * reps
    rows = [jnp.broadcast_to(e_ref[pl.ds(r0 + k, 1), :], (GRID_W, D_MODEL // 2)) for k in range(reps)]
    return jnp.concatenate([jnp.concatenate(rows, axis=0), col], axis=1)


def _mod_spec(layer, k, ctx_row):
    if ctx_row is None:
        return pl.BlockSpec((None, None, None, 1, D_MODEL), lambda b, i: (layer, b, k, 0, 0))
    return pl.BlockSpec((None, None, None, 1, D_MODEL), lambda b, i: (layer, ctx_row, k, 0, 0))


def _layer_spec(shape, layer, single_buffer=False):
    zeros = (0,) * len(shape)
    kwargs = dict(pipeline_mode=pl.Buffered(1)) if single_buffer else {}
    return pl.BlockSpec((None,) + tuple(shape), lambda b, i: (layer,) + zeros, **kwargs)


def _pos_spec():
    return pl.BlockSpec((GRID_W, D_MODEL // 2), lambda b, i: (0, 0))


def _inproj_kernel(x_ref, e_ref, g_ref, sh_ref, sc_ref, w_ref, o_ref, *, ts, add_pos):
    x = x_ref[...]
    if add_pos:
        x = x + _pos_rows(e_ref, pl.program_id(1), ts)
    rows = ts // INPROJ_CHUNKS
    for c in range(INPROJ_CHUNKS):
        sl = slice(c * rows, (c + 1) * rows)
        h = _rms_mod(x[sl, :], g_ref[...], sh_ref[...], sc_ref[...])
        o_ref[sl, :] = jnp.dot(h.astype(BF16), w_ref[...],
                               preferred_element_type=F32).astype(o_ref.dtype)


def _inproj(x, pos_e, mod5, g1, w_in, layer, ctx_row, ts, add_pos):
    b, s, d = x.shape
    return pl.pallas_call(
        functools.partial(_inproj_kernel, ts=ts, add_pos=add_pos),
        out_shape=jax.ShapeDtypeStruct((b, s, D_IN), BF16),
        grid=(b, s // ts),
        in_specs=[
            pl.BlockSpec((None, ts, d), lambda bb, i: (bb, i, 0)),
            _pos_spec(),
            _layer_spec((1, d), layer),
            _mod_spec(layer, 0, ctx_row),
            _mod_spec(layer, 1, ctx_row),
            _layer_spec((d, D_IN), layer),
        ],
        out_specs=pl.BlockSpec((None, ts, D_IN), lambda bb, i: (bb, i, 0)),
        compiler_params=_params("arbitrary", "arbitrary"),
        name="inproj",
    )(x, pos_e, g1, mod5, mod5, w_in)


def _scan_tile(a_ref, b_ref, out_ref, carry0, n_rows, reverse, unroll=False):
    width = a_ref.shape[-1]
    n_groups = n_rows // SUBLANES
    row = lax.broadcasted_iota(jnp.int32, (SUBLANES, width), 0)

    def group(gi, carry):
        g = (n_groups - 1 - gi) if reverse else gi
        start = g * SUBLANES if unroll else pl.multiple_of(g * SUBLANES, SUBLANES)
        a = a_ref[pl.ds(start, SUBLANES), :]
        h = b_ref[pl.ds(start, SUBLANES), :]
        for s in (1, 2, 4):
            if reverse:
                keep = row < SUBLANES - s
                shift = SUBLANES - s
            else:
                keep = row >= s
                shift = s
            h_sh = jnp.where(keep, pltpu.roll(h, shift, 0), 0.0)
            a_sh = jnp.where(keep, pltpu.roll(a, shift, 0), 1.0)
            h = h + a * h_sh
            a = a * a_sh
        h = h + a * carry
        out_ref[pl.ds(start, SUBLANES), :] = h
        edge = h[0:1, :] if reverse else h[SUBLANES - 1:SUBLANES, :]
        return jnp.broadcast_to(edge, (SUBLANES, width))

    carry = jnp.broadcast_to(carry0, (SUBLANES, width))
    if unroll:
        for gi in range(n_groups):
            carry = group(gi, carry)
        return carry[0:1, :]
    return lax.fori_loop(0, n_groups, group, carry, unroll=2)[0:1, :]


def _lru_fwd_kernel(pa_ref, prev_ref, next_ref, cw_ref, cb_ref, wg_ref, bg_ref, lam_ref, h0_ref,
                    hf_ref, ab_ref, hlast_ref, ext_scr, a_scr, b_scr, carry_scr, *, ts):
    i = pl.program_id(1)
    n_t = pl.num_programs(1)

    @pl.when(i == 0)
    def _():
        carry_scr[...] = h0_ref[...]

    ext_scr[0:HALO, :] = jnp.where(i > 0, prev_ref[...], 0.0).astype(F32)
    ext_scr[HALO:HALO + ts, :] = pa_ref[...].astype(F32)
    ext_scr[HALO + ts:, :] = jnp.where(i < n_t - 1, next_ref[...], 0.0).astype(F32)

    cx = jnp.broadcast_to(cb_ref[...], (ts, W_LRU))
    for k in range(LRU_CONV_W):
        off = HALO + k - LRU_PAD_L
        cx = cx + cw_ref[k:k + 1, :] * ext_scr[off:off + ts, :]

    gates = jnp.dot(cx.astype(BF16), wg_ref[...], preferred_element_type=F32) + bg_ref[...]
    neg_lam = -lam_ref[...]
    sp = jnp.maximum(neg_lam, 0.0) + jnp.log1p(jnp.exp(-jnp.abs(neg_lam)))
    lam_c = -LRU_C * sp
    for d in range(2):
        r = _sigmoid(gates[:, (2 * d) * W_LRU:(2 * d + 1) * W_LRU])
        ig = _sigmoid(gates[:, (2 * d + 1) * W_LRU:(2 * d + 2) * W_LRU])
        log_a = lam_c[:, d * W_LRU:(d + 1) * W_LRU] * r
        a = jnp.exp(log_a)
        t = jnp.tanh(log_a)
        z = -2.0 * t / (1.0 - t)
        bmul = jnp.where(z > 0.0, z * lax.rsqrt(z), 0.0)
        bb = bmul * (ig * cx)
        if d == 0:
            a_scr[...] = a
            b_scr[...] = bb
        else:
            ab_ref[:, :W_LRU] = a
            ab_ref[:, W_LRU:] = bb

    carry = _scan_tile(a_scr, b_scr, hf_ref, carry_scr[...], ts, reverse=False, unroll=True)
    carry_scr[...] = carry
    hlast_ref[...] = carry


def _lru_fwd(px, conv_w, conv_b, wg, bg, lam, h0, layer, ts):
    b, s, _ = px.shape
    nh = ts // HALO
    lasth = s // HALO - 1
    kernel = functools.partial(_lru_fwd_kernel, ts=ts)
    return pl.pallas_call(
        kernel,
        out_shape=(
            jax.ShapeDtypeStruct((b, s, W_LRU), F32),
            jax.ShapeDtypeStruct((b, s, 2 * W_LRU), F32),
            jax.ShapeDtypeStruct((b, 1, W_LRU), F32),
        ),
        grid=(b, s // ts),
        in_specs=[
            pl.BlockSpec((None, ts, W_LRU), lambda bb, i: (bb, i, 0)),
            pl.BlockSpec((None, HALO, W_LRU), lambda bb, i: (bb, jnp.maximum(i * nh - 1, 0), 0)),
            pl.BlockSpec((None, HALO, W_LRU), lambda bb, i: (bb, jnp.minimum((i + 1) * nh, lasth), 0)),
            _layer_spec((LRU_CONV_W, W_LRU), layer),
            _layer_spec((1, W_LRU), layer),
            _layer_spec((W_LRU, 4 * W_LRU), layer),
            _layer_spec((1, 4 * W_LRU), layer),
            _layer_spec((1, 2 * W_LRU), layer),
            pl.BlockSpec((None, 1, W_LRU), lambda bb, i: (bb, 0, 0)),
        ],
        out_specs=(
            pl.BlockSpec((None, ts, W_LRU), lambda bb, i: (bb, i, 0)),
            pl.BlockSpec((None, ts, 2 * W_LRU), lambda bb, i: (bb, i, 0)),
            pl.BlockSpec((None, 1, W_LRU), lambda bb, i: (bb, 0, 0)),
        ),
        scratch_shapes=[
            pltpu.VMEM((ts + 2 * HALO, W_LRU), F32),
            pltpu.VMEM((ts, W_LRU), F32),
            pltpu.VMEM((ts, W_LRU), F32),
            pltpu.VMEM((1, W_LRU), F32),
        ],
        compiler_params=_params("arbitrary", "arbitrary"),
        name="lru_fwd",
    )(px, px, px, conv_w, conv_b, wg, bg, lam, h0)


def _pool_windows(shape):
    grp = lax.broadcasted_iota(jnp.int32, shape, 1) // POOL_GROUP
    win = jnp.full(shape, POOL_WINDOWS[-1], jnp.int32)
    for g in range(len(POOL_WINDOWS) - 2, -1, -1):
        win = jnp.where(grp == g, POOL_WINDOWS[g], win)
    return win


def _pool_icnt_kernel(o_ref, *, ts, seq):
    win = _pool_windows((ts, W_POOL))
    lo = win // 2
    hi = win - lo - 1
    tpos = pl.program_id(0) * ts + lax.broadcasted_iota(jnp.int32, (ts, W_POOL), 0)
    cnt = (jnp.minimum(tpos + hi + 1, seq) - jnp.maximum(tpos - lo, 0)).astype(F32)
    o_ref[...] = 1.0 / cnt


def _pool_icnt(seq, ts):
    return pl.pallas_call(
        functools.partial(_pool_icnt_kernel, ts=ts, seq=seq),
        out_shape=jax.ShapeDtypeStruct((seq, W_POOL), F32),
        grid=(seq // ts,),
        out_specs=pl.BlockSpec((ts, W_POOL), lambda i: (i, 0)),
        compiler_params=_params("arbitrary"),
        name="pool_icnt",
    )()


def _pool_bands():
    t = jnp.arange(POOL_ROWS)[:, None]
    tok = jnp.arange(POOL_ROWS + 2 * HALO)[None, :] - HALO
    bands = []
    for w in POOL_WINDOWS:
        lo = w // 2
        hi = w - lo - 1
        bands.append((tok >= t - lo) & (tok <= t + hi))
    return jnp.stack(bands).astype(BF16)


def _mix_kernel(x_ref, e_ref, px_ref, prev_ref, next_ref, hf_ref, ab_ref, h0_ref, gt_ref,
                icnt_ref, pb_ref, pw_ref, ps_ref, slg_ref, slb_ref, sw_ref, sb_ref,
                dw_ref, db_ref, dlg_ref, dlb_ref, wo_ref,
                o_ref, hfirst_ref,
                hb_scr, carry_scr, zext_scr, yext_scr, shift_scr, ycat_scr, *, ts, seq, add_pos):
    i = pl.program_id(1)
    n_t = pl.num_programs(1)
    tile = n_t - 1 - i
    c_gate, c_pool, c_u, c_v, c_cv, c_cg = (W_LRU * k for k in range(1, 7))
    has_prev = tile > 0
    has_next = tile < n_t - 1

    @pl.when(i == 0)
    def _():
        carry_scr[...] = h0_ref[...]

    carry = _scan_tile(ab_ref.at[:, :W_LRU], ab_ref.at[:, W_LRU:], hb_scr, carry_scr[...], ts,
                       reverse=True, unroll=True)
    carry_scr[...] = carry
    hfirst_ref[...] = carry
    y_a = (hf_ref[...] + hb_scr[...]) * _gelu_tanh(px_ref[:, c_gate:c_gate + W_LRU].astype(F32))
    ycat_scr[:, :W_LRU] = y_a.astype(BF16)

    zext_scr[0:HALO, :] = jnp.where(has_prev, prev_ref[:, c_pool:c_pool + W_POOL], 0.0)
    zext_scr[HALO:HALO + ts, :] = px_ref[:, c_pool:c_pool + W_POOL]
    zext_scr[HALO + ts:, :] = jnp.where(has_next, next_ref[:, c_pool:c_pool + W_POOL], 0.0)
    grp = lax.broadcasted_iota(jnp.int32, (POOL_ROWS, W_POOL), 1) // POOL_GROUP
    for blk in range(ts // POOL_ROWS):
        r0 = blk * POOL_ROWS
        zb = zext_scr[r0:r0 + POOL_ROWS + 2 * HALO, :]
        wsum = jnp.dot(pb_ref[len(POOL_WINDOWS) - 1], zb, preferred_element_type=F32)
        for g in range(len(POOL_WINDOWS) - 2, -1, -1):
            wsum = jnp.where(grp == g, jnp.dot(pb_ref[g], zb, preferred_element_type=F32), wsum)
        rows = slice(r0, r0 + POOL_ROWS)
        pooled = wsum * icnt_ref[rows, :] - px_ref[rows, c_pool:c_pool + W_POOL].astype(F32)
        y_b = jnp.dot(pooled.astype(BF16), pw_ref[...], preferred_element_type=F32) * ps_ref[...]
        ycat_scr[rows, W_LRU:W_LRU + W_POOL] = y_b.astype(BF16)

    vn = _layer_norm(px_ref[:, c_v:c_v + W_SGU].astype(F32), slg_ref[...], slb_ref[...]).astype(BF16)
    head = lax.broadcasted_iota(jnp.int32, (SGU_CHUNK, W_SGU), 1) // SGU_HEAD_DIM
    zero = jnp.zeros((SGU_CHUNK, W_SGU), BF16)
    for c in range(ts // SGU_CHUNK):
        rows = slice(c * SGU_CHUNK, (c + 1) * SGU_CHUNK)
        vh = jnp.concatenate([jnp.where(head == h, vn[rows, :], zero) for h in range(SGU_HEADS)],
                             axis=0)
        zc = jnp.dot(sw_ref[...], vh, preferred_element_type=F32) + sb_ref[...]
        y_c = px_ref[rows, c_u:c_u + W_SGU].astype(F32) * zc
        ycat_scr[rows, W_LRU + W_POOL:W_LRU + W_POOL + W_SGU] = y_c.astype(BF16)

    def gated(ref):
        return (ref[:, c_cv:c_cv + W_CONV].astype(F32)
                * _sigmoid(ref[:, c_cg:c_cg + W_CONV].astype(F32)))

    yext_scr[0:HALO, :] = jnp.where(has_prev, gated(prev_ref), 0.0)
    yext_scr[HALO:HALO + ts, :] = gated(px_ref)
    yext_scr[HALO + ts:, :] = jnp.where(has_next, gated(next_ref), 0.0)
    n_shift = ts + 2 * HALO - SUBLANES
    for r in range(1, SUBLANES):
        shift_scr[r - 1] = yext_scr[r:r + n_shift, :]
    for blk in range(ts // CONV_ROWS):
        r0 = blk * CONV_ROWS
        acc = jnp.broadcast_to(db_ref[...], (CONV_ROWS, W_CONV))
        for k in range(CONF_CONV_W):
            off = HALO + k - CONF_PAD
            src = r0 + (off // SUBLANES) * SUBLANES
            if off % SUBLANES:
                tap = shift_scr[off % SUBLANES - 1, src:src + CONV_ROWS, :]
            else:
                tap = yext_scr[src:src + CONV_ROWS, :]
            acc = acc + dw_ref[k:k + 1, :] * tap
        y_d = _silu(_layer_norm(acc, dlg_ref[...], dlb_ref[...]))
        ycat_scr[r0:r0 + CONV_ROWS, W_LRU + W_POOL + W_SGU:] = y_d.astype(BF16)

    x = x_ref[...]
    if add_pos:
        x = x + _pos_rows(e_ref, tile, ts)
    o_ref[...] = x + gt_ref[...] * jnp.dot(ycat_scr[...], wo_ref[...], preferred_element_type=F32)


def _mix(x, pos_e, px, hf, ab, h0b, mod5, icnt, pool_band, pool_bd, pool_scale, sgu_ln_g, sgu_ln_b,
         sgu_ws, sgu_bias, conv_d_w, conv_d_b, conv_ln_g, conv_ln_b, w_out, layer, ctx_row, ts,
         add_pos):
    b, s, d = x.shape
    n_t = s // ts
    nh = ts // HALO
    lasth = s // HALO - 1
    kernel = functools.partial(_mix_kernel, ts=ts, seq=s, add_pos=add_pos)

    def rev(bb, i):
        return (bb, n_t - 1 - i, 0)

    return pl.pallas_call(
        kernel,
        out_shape=(
            jax.ShapeDtypeStruct((b, s, d), F32),
            jax.ShapeDtypeStruct((b, 1, W_LRU), F32),
        ),
        grid=(b, n_t),
        in_specs=[
            pl.BlockSpec((None, ts, d), rev),
            _pos_spec(),
            pl.BlockSpec((None, ts, D_IN), rev),
            pl.BlockSpec((None, HALO, D_IN),
                         lambda bb, i: (bb, jnp.maximum((n_t - 1 - i) * nh - 1, 0), 0)),
            pl.BlockSpec((None, HALO, D_IN),
                         lambda bb, i: (bb, jnp.minimum((n_t - i) * nh, lasth), 0)),
            pl.BlockSpec((None, ts, W_LRU), rev),
            pl.BlockSpec((None, ts, 2 * W_LRU), rev),
            pl.BlockSpec((None, 1, W_LRU), lambda bb, i: (bb, 0, 0)),
            _mod_spec(layer, 2, ctx_row),
            pl.BlockSpec((ts, W_POOL), lambda bb, i: (n_t - 1 - i, 0)),
            pl.BlockSpec(pool_band.shape, lambda bb, i: (0, 0, 0)),
            _layer_spec((W_POOL, W_POOL), layer),
            _layer_spec((1, W_POOL), layer),
            _layer_spec((1, W_SGU), layer),
            _layer_spec((1, W_SGU), layer),
            _layer_spec((SGU_CHUNK, SGU_HEADS * SGU_CHUNK), layer),
            _layer_spec((SGU_CHUNK, W_SGU), layer),
            _layer_spec((CONF_CONV_W, W_CONV), layer),
            _layer_spec((1, W_CONV), layer),
            _layer_spec((1, W_CONV), layer),
            _layer_spec((1, W_CONV), layer),
            _layer_spec((d, d), layer),
        ],
        out_specs=(
            pl.BlockSpec((None, ts, d), rev),
            pl.BlockSpec((None, 1, W_LRU), lambda bb, i: (bb, 0, 0)),
        ),
        scratch_shapes=[
            pltpu.VMEM((ts, W_LRU), F32),
            pltpu.VMEM((1, W_LRU), F32),
            pltpu.VMEM((ts + 2 * HALO, W_POOL), BF16),
            pltpu.VMEM((ts + 2 * HALO, W_CONV), F32),
            pltpu.VMEM((SUBLANES - 1, ts + 2 * HALO - SUBLANES, W_CONV), F32),
            pltpu.VMEM((ts, d), BF16),
        ],
        compiler_params=_params("arbitrary", "arbitrary"),
        name="mix",
    )(x, pos_e, px, px, px, hf, ab, h0b, mod5, icnt, pool_band, pool_bd, pool_scale, sgu_ln_g,
      sgu_ln_b, sgu_ws, sgu_bias, conv_d_w, conv_d_b, conv_ln_g, conv_ln_b, w_out)


def _bwd_state_kernel(ab_ref, o_ref, hb_scr, *, rows):
    zero = jnp.zeros((1, W_LRU), F32)
    o_ref[...] = _scan_tile(ab_ref.at[:, :W_LRU], ab_ref.at[:, W_LRU:], hb_scr, zero, rows,
                            reverse=True)


def _bwd_state(ab):
    b, s, _ = ab.shape
    return pl.pallas_call(
        functools.partial(_bwd_state_kernel, rows=s),
        out_shape=jax.ShapeDtypeStruct((b, 1, W_LRU), F32),
        grid=(b,),
        in_specs=[pl.BlockSpec((None, s, 2 * W_LRU), lambda bb: (bb, 0, 0))],
        out_specs=pl.BlockSpec((None, 1, W_LRU), lambda bb: (bb, 0, 0)),
        scratch_shapes=[pltpu.VMEM((s, W_LRU), F32)],
        compiler_params=_params("arbitrary"),
        name="bwd_state",
    )(ab)


def _mlp_kernel(x_ref, g_ref, sh_ref, sc_ref, gt_ref, w1_ref, w2_ref, gf_ref, o_ref, acc_scr,
                *, final):
    x = x_ref[...]
    h = _rms_mod(x, g_ref[...], sh_ref[...], sc_ref[...]).astype(BF16)
    for j in range(D_FF // FF_CHUNK):
        cols = slice(j * FF_CHUNK, (j + 1) * FF_CHUNK)
        u = jnp.dot(h, w1_ref[:, cols], preferred_element_type=F32)
        u = jnp.square(jnp.maximum(u, 0.0)).astype(BF16)
        part = jnp.dot(u, w2_ref[cols, :], preferred_element_type=F32)
        if j == 0:
            acc_scr[...] = part
        else:
            acc_scr[...] += part
    y = x + gt_ref[...] * acc_scr[...]
    if final:
        ms = jnp.mean(y * y, axis=-1, keepdims=True)
        y = y * lax.rsqrt(ms + EPS) * gf_ref[...]
    o_ref[...] = y


def _mlp(x, mod5, g2, w1, w2, g_final, layer, ctx_row, tm, final):
    b, s, d = x.shape
    return pl.pallas_call(
        functools.partial(_mlp_kernel, final=final),
        out_shape=jax.ShapeDtypeStruct(x.shape, F32),
        grid=(b, s // tm),
        in_specs=[
            pl.BlockSpec((None, tm, d), lambda bb, i: (bb, i, 0)),
            _layer_spec((1, d), layer),
            _mod_spec(layer, 3, ctx_row),
            _mod_spec(layer, 4, ctx_row),
            _mod_spec(layer, 5, ctx_row),
            _layer_spec((d, D_FF), layer, single_buffer=True),
            _layer_spec((D_FF, d), layer, single_buffer=True),
            pl.BlockSpec((1, d), lambda bb, i: (0, 0)),
        ],
        out_specs=pl.BlockSpec((None, tm, d), lambda bb, i: (bb, i, 0)),
        scratch_shapes=[pltpu.VMEM((tm, d), F32)],
        compiler_params=_params("arbitrary", "arbitrary"),
        name="mlp",
    )(x, g2, mod5, mod5, mod5, w1, w2, g_final)


def _block_diag(w):
    *lead, h, n, _ = w.shape
    eye = jnp.eye(h, dtype=w.dtype)
    out = jnp.einsum('...hij,hg->...higj', w, eye)
    return out.reshape(*lead, h * n, h * n)


def _tile_rows(s, want):
    t = min(s, want)
    while s % t:
        t //= 2
    return t


def kernel(x, c, ctx, c_ctx, w_mod, b_mod, g_norm1, g_norm2, w_in, conv_a_w, conv_a_b, lru_wa, lru_ba, lru_wx, lru_bx, lru_lambda, pool_w, pool_scale, sgu_ln_g, sgu_ln_b, sgu_w, sgu_b, conv_d_w, conv_d_b, conv_ln_g, conv_ln_b, w_out, w_mlp1, w_mlp2, g_final):
    depth = w_in.shape[0]
    bsz, seq, d = x.shape
    ctx_len = ctx.shape[1]
    assert d == D_MODEL and bsz < MOD_ROWS
    assert seq % SGU_CHUNK == 0 and ctx_len % SGU_CHUNK == 0
    ctx_row = bsz

    w_in_b = w_in.astype(BF16)
    w_out_b = w_out.astype(BF16)
    w1_b = w_mlp1.astype(BF16)
    w2_b = w_mlp2.astype(BF16)
    wa_bd = _block_diag(lru_wa)
    wx_bd = _block_diag(lru_wx)
    wg = jnp.concatenate([wa_bd[:, 0], wx_bd[:, 0], wa_bd[:, 1], wx_bd[:, 1]], axis=-1).astype(BF16)
    bg = jnp.concatenate([lru_ba[:, 0], lru_bx[:, 0], lru_ba[:, 1], lru_bx[:, 1]], axis=-1)[:, None, :]
    lam = lru_lambda.reshape(depth, 1, 2 * W_LRU)
    pool_bd = _block_diag(pool_w).astype(BF16)
    sgu_ws = jnp.swapaxes(sgu_w, 1, 2).reshape(depth, SGU_CHUNK, SGU_HEADS * SGU_CHUNK).astype(BF16)
    sgu_bias = jnp.repeat(jnp.swapaxes(sgu_b, 1, 2), SGU_HEAD_DIM, axis=2)
    row = lambda v: v[:, None, :]

    cs = jnp.zeros((MOD_ROWS, d), F32).at[:bsz].set(c).at[ctx_row].set(c_ctx)
    mod = _modulation(cs, w_mod, b_mod)
    mod5 = mod.reshape(depth, MOD_ROWS, N_MOD, 1, d)

    ts = _tile_rows(seq, 512)
    ts_c = _tile_rows(ctx_len, 512)
    tm = _tile_rows(seq, 1024)
    tm_c = _tile_rows(bsz * ctx_len, 1024)
    assert ts % GRID_W == 0 and ts % CONV_ROWS == 0 and ts_c % CONV_ROWS == 0
    pos_e = _pos_table()
    xc = ctx
    zeros_h = jnp.zeros((bsz, 1, W_LRU), F32)
    g_fin = g_final.reshape(1, d)
    mix_w = (_pool_bands(), pool_bd, row(pool_scale), row(sgu_ln_g), row(sgu_ln_b), sgu_ws,
             sgu_bias, conv_d_w, row(conv_d_b), row(conv_ln_g), row(conv_ln_b), w_out_b)
    icnt = _pool_icnt(seq, ts)
    icnt_c = _pool_icnt(ctx_len, ts_c)

    for l in range(depth):
        last = l == depth - 1
        first = l == 0
        pc = _inproj(xc.reshape(1, -1, d), pos_e, mod5, row(g_norm1), w_in_b, l, ctx_row, tm_c, False)
        pc = pc.reshape(bsz, ctx_len, D_IN)
        hcf, abc, hcf_last = _lru_fwd(pc, conv_a_w, row(conv_a_b), wg, bg, lam, zeros_h, l, ts_c)
        if last:
            hcb_first = _bwd_state(abc)
        else:
            xc, hcb_first = _mix(xc, pos_e, pc, hcf, abc, zeros_h, mod5, icnt_c, *mix_w, l, ctx_row,
                                 ts_c, False)
            xc = _mlp(xc.reshape(1, -1, d), mod5, row(g_norm2), w1_b, w2_b, g_fin, l, ctx_row, tm_c,
                      False).reshape(bsz, ctx_len, d)
        px = _inproj(x, pos_e, mod5, row(g_norm1), w_in_b, l, None, ts, first)
        hf, ab, _ = _lru_fwd(px, conv_a_w, row(conv_a_b), wg, bg, lam, hcf_last, l, ts)
        x, _ = _mix(x, pos_e, px, hf, ab, hcb_first, mod5, icnt, *mix_w, l, None, ts, first)
        x = _mlp(x, mod5, row(g_norm2), w1_b, w2_b, g_fin, l, None, tm, last)
    return x
```

```python
import functools
import math

import jax
import jax.numpy as jnp
from jax import lax
from jax.experimental import pallas as pl
from jax.experimental.pallas import tpu as pltpu

F32 = jnp.float32
BF16 = jnp.bfloat16

D_MODEL = 1024
GRID_W = 64
W_LRU = 256
W_POOL = 256
W_SGU = 256
W_CONV = 256
LRU_HEADS = 4
LRU_CONV_W = 4
LRU_PAD_L = 2
LRU_C = 8.0
POOL_WINDOWS = (2, 4, 8, 16)
POOL_GROUP = W_POOL // len(POOL_WINDOWS)
SGU_CHUNK = 128
SGU_HEADS = 4
SGU_HEAD_DIM = W_SGU // SGU_HEADS
CONF_CONV_W = 31
CONF_PAD = 15
D_FF = 4 * D_MODEL
D_IN = 2 * W_LRU + W_POOL + 2 * W_SGU + 2 * W_CONV
N_MOD = 6
EPS = 1e-6
POS_BASE = 10000.0

SUBLANES = 8
HALO = 16
VMEM_LIMIT = 56 * 1024 * 1024
MOD_ROWS = 8
FF_CHUNK = 512
CONV_ROWS = 64
POOL_ROWS = 128
INPROJ_CHUNKS = 2


def _params(*sem):
    return pltpu.CompilerParams(dimension_semantics=sem, vmem_limit_bytes=VMEM_LIMIT)


def _sigmoid(x):
    return 0.5 * jnp.tanh(0.5 * x) + 0.5


def _silu(x):
    return x * _sigmoid(x)


def _gelu_tanh(x):
    return 0.5 * x * (1.0 + jnp.tanh(0.7978845608028654 * (x + 0.044715 * (x * x * x))))


def _rms_mod(x, g, shift, scale):
    ms = jnp.mean(x * x, axis=-1, keepdims=True)
    y = x * lax.rsqrt(ms + EPS) * g
    return y * (1.0 + scale) + shift


def _layer_norm(x, g, b):
    mu = jnp.mean(x, axis=-1, keepdims=True)
    xc = x - mu
    var = jnp.mean(xc * xc, axis=-1, keepdims=True)
    return xc * lax.rsqrt(var + EPS) * g + b


def _mod_kernel(c_ref, w_ref, b_ref, o_ref):
    s = _silu(c_ref[...]).astype(BF16)
    o_ref[...] = jnp.dot(s, w_ref[...].astype(BF16), preferred_element_type=F32) + b_ref[...]


def _modulation(cs, w_mod, b_mod):
    depth, _, n = w_mod.shape
    tn = 1536
    return pl.pallas_call(
        _mod_kernel,
        out_shape=jax.ShapeDtypeStruct((depth, MOD_ROWS, n), F32),
        grid=(depth, n // tn),
        in_specs=[
            pl.BlockSpec((MOD_ROWS, D_MODEL), lambda l, j: (0, 0)),
            pl.BlockSpec((None, D_MODEL, tn), lambda l, j: (l, 0, j)),
            pl.BlockSpec((None, 1, tn), lambda l, j: (l, 0, j)),
        ],
        out_specs=pl.BlockSpec((None, MOD_ROWS, tn), lambda l, j: (l, 0, j)),
        compiler_params=_params("arbitrary", "arbitrary"),
        name="modulation",
    )(cs, w_mod, b_mod.reshape(depth, 1, n))


def _pos_table_kernel(o_ref):
    quarter = D_MODEL // 4
    pos = lax.broadcasted_iota(jnp.int32, (GRID_W, quarter), 0).astype(F32)
    j = lax.broadcasted_iota(jnp.int32, (GRID_W, quarter), 1).astype(F32)
    omega = 1.0 / jnp.exp(j * (math.log(POS_BASE) / quarter))
    ang = pos * omega
    o_ref[:, :quarter] = jnp.sin(ang)
    o_ref[:, quarter:] = jnp.cos(ang)


def _pos_table():
    return pl.pallas_call(
        _pos_table_kernel,
        out_shape=jax.ShapeDtypeStruct((GRID_W, D_MODEL // 2), F32),
        name="pos_table",
    )()


def _pos_rows(e_ref, tile, ts):
    reps = ts // GRID_W
    col = jnp.concatenate([e_ref[...]] * reps, axis=0)
    r0 = tile * reps
    rows = [jnp.broadcast_to(e_ref[pl.ds(r0 + k, 1), :], (GRID_W, D_MODEL // 2)) for k in range(reps)]
    return jnp.concatenate([jnp.concatenate(rows, axis=0), col], axis=1)


def _mod_spec(layer, k, ctx_row):
    if ctx_row is None:
        return pl.BlockSpec((None, None, None, 1, D_MODEL), lambda b, i: (layer, b, k, 0, 0))
    return pl.BlockSpec((None, None, None, 1, D_MODEL), lambda b, i: (layer, ctx_row, k, 0, 0))


def _layer_spec(shape, layer, single_buffer=False):
    zeros = (0,) * len(shape)
    kwargs = dict(pipeline_mode=pl.Buffered(1)) if single_buffer else {}
    return pl.BlockSpec((None,) + tuple(shape), lambda b, i: (layer,) + zeros, **kwargs)


def _pos_spec():
    return pl.BlockSpec((GRID_W, D_MODEL // 2), lambda b, i: (0, 0))


def _inproj_kernel(x_ref, e_ref, g_ref, sh_ref, sc_ref, w_ref, o_ref, *, ts, add_pos):
    x = x_ref[...]
    if add_pos:
        x = x + _pos_rows(e_ref, pl.program_id(1), ts)
    rows = ts // INPROJ_CHUNKS
    for c in range(INPROJ_CHUNKS):
        sl = slice(c * rows, (c + 1) * rows)
        h = _rms_mod(x[sl, :], g_ref[...], sh_ref[...], sc_ref[...])
        o_ref[sl, :] = jnp.dot(h.astype(BF16), w_ref[...],
                               preferred_element_type=F32).astype(o_ref.dtype)


def _inproj(x, pos_e, mod5, g1, w_in, layer, ctx_row, ts, add_pos):
    b, s, d = x.shape
    return pl.pallas_call(
        functools.partial(_inproj_kernel, ts=ts, add_pos=add_pos),
        out_shape=jax.ShapeDtypeStruct((b, s, D_IN), BF16),
        grid=(b, s // ts),
        in_specs=[
            pl.BlockSpec((None, ts, d), lambda bb, i: (bb, i, 0)),
            _pos_spec(),
            _layer_spec((1, d), layer),
            _mod_spec(layer, 0, ctx_row),
            _mod_spec(layer, 1, ctx_row),
            _layer_spec((d, D_IN), layer),
        ],
        out_specs=pl.BlockSpec((None, ts, D_IN), lambda bb, i: (bb, i, 0)),
        compiler_params=_params("arbitrary", "arbitrary"),
        name="inproj",
    )(x, pos_e, g1, mod5, mod5, w_in)


def _scan_tile(a_ref, b_ref, out_ref, carry0, n_rows, reverse, unroll=False):
    width = a_ref.shape[-1]
    n_groups = n_rows // SUBLANES
    row = lax.broadcasted_iota(jnp.int32, (SUBLANES, width), 0)

    def group(gi, carry):
        g = (n_groups - 1 - gi) if reverse else gi
        start = g * SUBLANES if unroll else pl.multiple_of(g * SUBLANES, SUBLANES)
        a = a_ref[pl.ds(start, SUBLANES), :]
        h = b_ref[pl.ds(start, SUBLANES), :]
        for s in (1, 2, 4):
            if reverse:
                keep = row < SUBLANES - s
                shift = SUBLANES - s
            else:
                keep = row >= s
                shift = s
            h_sh = jnp.where(keep, pltpu.roll(h, shift, 0), 0.0)
            a_sh = jnp.where(keep, pltpu.roll(a, shift, 0), 1.0)
            h = h + a * h_sh
            a = a * a_sh
        h = h + a * carry
        out_ref[pl.ds(start, SUBLANES), :] = h
        edge = h[0:1, :] if reverse else h[SUBLANES - 1:SUBLANES, :]
        return jnp.broadcast_to(edge, (SUBLANES, width))

    carry = jnp.broadcast_to(carry0, (SUBLANES, width))
    if unroll:
        for gi in range(n_groups):
            carry = group(gi, carry)
        return carry[0:1, :]
    return lax.fori_loop(0, n_groups, group, carry, unroll=2)[0:1, :]


def _lru_fwd_kernel(pa_ref, prev_ref, next_ref, cw_ref, cb_ref, wg_ref, bg_ref, lam_ref, h0_ref,
                    hf_ref, ab_ref, hlast_ref, ext_scr, a_scr, b_scr, carry_scr, *, ts):
    i = pl.program_id(1)
    n_t = pl.num_programs(1)

    @pl.when(i == 0)
    def _():
        carry_scr[...] = h0_ref[...]

    ext_scr[0:HALO, :] = jnp.where(i > 0, prev_ref[...], 0.0).astype(F32)
    ext_scr[HALO:HALO + ts, :] = pa_ref[...].astype(F32)
    ext_scr[HALO + ts:, :] = jnp.where(i < n_t - 1, next_ref[...], 0.0).astype(F32)

    cx = jnp.broadcast_to(cb_ref[...], (ts, W_LRU))
    for k in range(LRU_CONV_W):
        off = HALO + k - LRU_PAD_L
        cx = cx + cw_ref[k:k + 1, :] * ext_scr[off:off + ts, :]

    gates = jnp.dot(cx.astype(BF16), wg_ref[...], preferred_element_type=F32) + bg_ref[...]
    neg_lam = -lam_ref[...]
    sp = jnp.maximum(neg_lam, 0.0) + jnp.log1p(jnp.exp(-jnp.abs(neg_lam)))
    lam_c = -LRU_C * sp
    for d in range(2):
        r = _sigmoid(gates[:, (2 * d) * W_LRU:(2 * d + 1) * W_LRU])
        ig = _sigmoid(gates[:, (2 * d + 1) * W_LRU:(2 * d + 2) * W_LRU])
        log_a = lam_c[:, d * W_LRU:(d + 1) * W_LRU] * r
        a = jnp.exp(log_a)
        t = jnp.tanh(log_a)
        z = -2.0 * t / (1.0 - t)
        bmul = jnp.where(z > 0.0, z * lax.rsqrt(z), 0.0)
        bb = bmul * (ig * cx)
        if d == 0:
            a_scr[...] = a
            b_scr[...] = bb
        else:
            ab_ref[:, :W_LRU] = a
            ab_ref[:, W_LRU:] = bb

    carry = _scan_tile(a_scr, b_scr, hf_ref, carry_scr[...], ts, reverse=False, unroll=True)
    carry_scr[...] = carry
    hlast_ref[...] = carry


def _lru_fwd(px, conv_w, conv_b, wg, bg, lam, h0, layer, ts):
    b, s, _ = px.shape
    nh = ts // HALO
    lasth = s // HALO - 1
    kernel = functools.partial(_lru_fwd_kernel, ts=ts)
    return pl.pallas_call(
        kernel,
        out_shape=(
            jax.ShapeDtypeStruct((b, s, W_LRU), F32),
            jax.ShapeDtypeStruct((b, s, 2 * W_LRU), F32),
            jax.ShapeDtypeStruct((b, 1, W_LRU), F32),
        ),
        grid=(b, s // ts),
        in_specs=[
            pl.BlockSpec((None, ts, W_LRU), lambda bb, i: (bb, i, 0)),
            pl.BlockSpec((None, HALO, W_LRU), lambda bb, i: (bb, jnp.maximum(i * nh - 1, 0), 0)),
            pl.BlockSpec((None, HALO, W_LRU), lambda bb, i: (bb, jnp.minimum((i + 1) * nh, lasth), 0)),
            _layer_spec((LRU_CONV_W, W_LRU), layer),
            _layer_spec((1, W_LRU), layer),
            _layer_spec((W_LRU, 4 * W_LRU), layer),
            _layer_spec((1, 4 * W_LRU), layer),
            _layer_spec((1, 2 * W_LRU), layer),
            pl.BlockSpec((None, 1, W_LRU), lambda bb, i: (bb, 0, 0)),
        ],
        out_specs=(
            pl.BlockSpec((None, ts, W_LRU), lambda bb, i: (bb, i, 0)),
            pl.BlockSpec((None, ts, 2 * W_LRU), lambda bb, i: (bb, i, 0)),
            pl.BlockSpec((None, 1, W_LRU), lambda bb, i: (bb, 0, 0)),
        ),
        scratch_shapes=[
            pltpu.VMEM((ts + 2 * HALO, W_LRU), F32),
            pltpu.VMEM((ts, W_LRU), F32),
            pltpu.VMEM((ts, W_LRU), F32),
            pltpu.VMEM((1, W_LRU), F32),
        ],
        compiler_params=_params("arbitrary", "arbitrary"),
        name="lru_fwd",
    )(px, px, px, conv_w, conv_b, wg, bg, lam, h0)


def _pool_windows(shape):
    grp = lax.broadcasted_iota(jnp.int32, shape, 1) // POOL_GROUP
    win = jnp.full(shape, POOL_WINDOWS[-1], jnp.int32)
    for g in range(len(POOL_WINDOWS) - 2, -1, -1):
        win = jnp.where(grp == g, POOL_WINDOWS[g], win)
    return win


def _pool_icnt_kernel(o_ref, *, ts, seq):
    win = _pool_windows((ts, W_POOL))
    lo = win // 2
    hi = win - lo - 1
    tpos = pl.program_id(0) * ts + lax.broadcasted_iota(jnp.int32, (ts, W_POOL), 0)
    cnt = (jnp.minimum(tpos + hi + 1, seq) - jnp.maximum(tpos - lo, 0)).astype(F32)
    o_ref[...] = 1.0 / cnt


def _pool_icnt(seq, ts):
    return pl.pallas_call(
        functools.partial(_pool_icnt_kernel, ts=ts, seq=seq),
        out_shape=jax.ShapeDtypeStruct((seq, W_POOL), F32),
        grid=(seq // ts,),
        out_specs=pl.BlockSpec((ts, W_POOL), lambda i: (i, 0)),
        compiler_params=_params("arbitrary"),
        name="pool_icnt",
    )()


def _pool_bands():
    t = jnp.arange(POOL_ROWS)[:, None]
    tok = jnp.arange(POOL_ROWS + 2 * HALO)[None, :] - HALO
    bands = []
    for w in POOL_WINDOWS:
        lo = w // 2
        hi = w - lo - 1
        bands.append((tok >= t - lo) & (tok <= t + hi))
    return jnp.stack(bands).astype(BF16)


def _mix_kernel(x_ref, e_ref, px_ref, prev_ref, next_ref, hf_ref, ab_ref, h0_ref, gt_ref,
                icnt_ref, pb_ref, pw_ref, ps_ref, slg_ref, slb_ref, sw_ref, sb_ref,
                dw_ref, db_ref, dlg_ref, dlb_ref, wo_ref,
                o_ref, hfirst_ref,
                hb_scr, carry_scr, zext_scr, yext_scr, shift_scr, ycat_scr, *, ts, seq, add_pos):
    i = pl.program_id(1)
    n_t = pl.num_programs(1)
    tile = n_t - 1 - i
    c_gate, c_pool, c_u, c_v, c_cv, c_cg = (W_LRU * k for k in range(1, 7))
    has_prev = tile > 0
    has_next = tile < n_t - 1

    @pl.when(i == 0)
    def _():
        carry_scr[...] = h0_ref[...]

    carry = _scan_tile(ab_ref.at[:, :W_LRU], ab_ref.at[:, W_LRU:], hb_scr, carry_scr[...], ts,
                       reverse=True, unroll=True)
    carry_scr[...] = carry
    hfirst_ref[...] = carry
    y_a = (hf_ref[...] + hb_scr[...]) * _gelu_tanh(px_ref[:, c_gate:c_gate + W_LRU].astype(F32))
    ycat_scr[:, :W_LRU] = y_a.astype(BF16)

    zext_scr[0:HALO, :] = jnp.where(has_prev, prev_ref[:, c_pool:c_pool + W_POOL], 0.0)
    zext_scr[HALO:HALO + ts, :] = px_ref[:, c_pool:c_pool + W_POOL]
    zext_scr[HALO + ts:, :] = jnp.where(has_next, next_ref[:, c_pool:c_pool + W_POOL], 0.0)
    grp = lax.broadcasted_iota(jnp.int32, (POOL_ROWS, W_POOL), 1) // POOL_GROUP
    for blk in range(ts // POOL_ROWS):
        r0 = blk * POOL_ROWS
        zb = zext_scr[r0:r0 + POOL_ROWS + 2 * HALO, :]
        wsum = jnp.dot(pb_ref[len(POOL_WINDOWS) - 1], zb, preferred_element_type=F32)
        for g in range(len(POOL_WINDOWS) - 2, -1, -1):
            wsum = jnp.where(grp == g, jnp.dot(pb_ref[g], zb, preferred_element_type=F32), wsum)
        rows = slice(r0, r0 + POOL_ROWS)
        pooled = wsum * icnt_ref[rows, :] - px_ref[rows, c_pool:c_pool + W_POOL].astype(F32)
        y_b = jnp.dot(pooled.astype(BF16), pw_ref[...], preferred_element_type=F32) * ps_ref[...]
        ycat_scr[rows, W_LRU:W_LRU + W_POOL] = y_b.astype(BF16)

    vn = _layer_norm(px_ref[:, c_v:c_v + W_SGU].astype(F32), slg_ref[...], slb_ref[...]).astype(BF16)
    head = lax.broadcasted_iota(jnp.int32, (SGU_CHUNK, W_SGU), 1) // SGU_HEAD_DIM
    zero = jnp.zeros((SGU_CHUNK, W_SGU), BF16)
    for c in range(ts // SGU_CHUNK):
        rows = slice(c * SGU_CHUNK, (c + 1) * SGU_CHUNK)
        vh = jnp.concatenate([jnp.where(head == h, vn[rows, :], zero) for h in range(SGU_HEADS)],
                             axis=0)
        zc = jnp.dot(sw_ref[...], vh, preferred_element_type=F32) + sb_ref[...]
        y_c = px_ref[rows, c_u:c_u + W_SGU].astype(F32) * zc
        ycat_scr[rows, W_LRU + W_POOL:W_LRU + W_POOL + W_SGU] = y_c.astype(BF16)

    def gated(ref):
        return (ref[:, c_cv:c_cv + W_CONV].astype(F32)
                * _sigmoid(ref[:, c_cg:c_cg + W_CONV].astype(F32)))

    yext_scr[0:HALO, :] = jnp.where(has_prev, gated(prev_ref), 0.0)
    yext_scr[HALO:HALO + ts, :] = gated(px_ref)
    yext_scr[HALO + ts:, :] = jnp.where(has_next, gated(next_ref), 0.0)
    n_shift = ts + 2 * HALO - SUBLANES
    for r in range(1, SUBLANES):
        shift_scr[r - 1] = yext_scr[r:r + n_shift, :]
    for blk in range(ts // CONV_ROWS):
        r0 = blk * CONV_ROWS
        acc = jnp.broadcast_to(db_ref[...], (CONV_ROWS, W_CONV))
        for k in range(CONF_CONV_W):
            off = HALO + k - CONF_PAD
            src = r0 + (off // SUBLANES) * SUBLANES
            if off % SUBLANES:
                tap = shift_scr[off % SUBLANES - 1, src:src + CONV_ROWS, :]
            else:
                tap = yext_scr[src:src + CONV_ROWS, :]
            acc = acc + dw_ref[k:k + 1, :] * tap
        y_d = _silu(_layer_norm(acc, dlg_ref[...], dlb_ref[...]))
        ycat_scr[r0:r0 + CONV_ROWS, W_LRU + W_POOL + W_SGU:] = y_d.astype(BF16)

    x = x_ref[...]
    if add_pos:
        x = x + _pos_rows(e_ref, tile, ts)
    o_ref[...] = x + gt_ref[...] * jnp.dot(ycat_scr[...], wo_ref[...], preferred_element_type=F32)


def _mix(x, pos_e, px, hf, ab, h0b, mod5, icnt, pool_band, pool_bd, pool_scale, sgu_ln_g, sgu_ln_b,
         sgu_ws, sgu_bias, conv_d_w, conv_d_b, conv_ln_g, conv_ln_b, w_out, layer, ctx_row, ts,
         add_pos):
    b, s, d = x.shape
    n_t = s // ts
    nh = ts // HALO
    lasth = s // HALO - 1
    kernel = functools.partial(_mix_kernel, ts=ts, seq=s, add_pos=add_pos)

    def rev(bb, i):
        return (bb, n_t - 1 - i, 0)

    return pl.pallas_call(
        kernel,
        out_shape=(
            jax.ShapeDtypeStruct((b, s, d), F32),
            jax.ShapeDtypeStruct((b, 1, W_LRU), F32),
        ),
        grid=(b, n_t),
        in_specs=[
            pl.BlockSpec((None, ts, d), rev),
            _pos_spec(),
            pl.BlockSpec((None, ts, D_IN), rev),
            pl.BlockSpec((None, HALO, D_IN),
                         lambda bb, i: (bb, jnp.maximum((n_t - 1 - i) * nh - 1, 0), 0)),
            pl.BlockSpec((None, HALO, D_IN),
                         lambda bb, i: (bb, jnp.minimum((n_t - i) * nh, lasth), 0)),
            pl.BlockSpec((None, ts, W_LRU), rev),
            pl.BlockSpec((None, ts, 2 * W_LRU), rev),
            pl.BlockSpec((None, 1, W_LRU), lambda bb, i: (bb, 0, 0)),
            _mod_spec(layer, 2, ctx_row),
            pl.BlockSpec((ts, W_POOL), lambda bb, i: (n_t - 1 - i, 0)),
            pl.BlockSpec(pool_band.shape, lambda bb, i: (0, 0, 0)),
            _layer_spec((W_POOL, W_POOL), layer),
            _layer_spec((1, W_POOL), layer),
            _layer_spec((1, W_SGU), layer),
            _layer_spec((1, W_SGU), layer),
            _layer_spec((SGU_CHUNK, SGU_HEADS * SGU_CHUNK), layer),
            _layer_spec((SGU_CHUNK, W_SGU), layer),
            _layer_spec((CONF_CONV_W, W_CONV), layer),
            _layer_spec((1, W_CONV), layer),
            _layer_spec((1, W_CONV), layer),
            _layer_spec((1, W_CONV), layer),
            _layer_spec((d, d), layer),
        ],
        out_specs=(
            pl.BlockSpec((None, ts, d), rev),
            pl.BlockSpec((None, 1, W_LRU), lambda bb, i: (bb, 0, 0)),
        ),
        scratch_shapes=[
            pltpu.VMEM((ts, W_LRU), F32),
            pltpu.VMEM((1, W_LRU), F32),
            pltpu.VMEM((ts + 2 * HALO, W_POOL), BF16),
            pltpu.VMEM((ts + 2 * HALO, W_CONV), F32),
            pltpu.VMEM((SUBLANES - 1, ts + 2 * HALO - SUBLANES, W_CONV), F32),
            pltpu.VMEM((ts, d), BF16),
        ],
        compiler_params=_params("arbitrary", "arbitrary"),
        name="mix",
    )(x, pos_e, px, px, px, hf, ab, h0b, mod5, icnt, pool_band, pool_bd, pool_scale, sgu_ln_g,
      sgu_ln_b, sgu_ws, sgu_bias, conv_d_w, conv_d_b, conv_ln_g, conv_ln_b, w_out)


def _bwd_state_kernel(ab_ref, o_ref, hb_scr, *, rows):
    zero = jnp.zeros((1, W_LRU), F32)
    o_ref[...] = _scan_tile(ab_ref.at[:, :W_LRU], ab_ref.at[:, W_LRU:], hb_scr, zero, rows,
                            reverse=True)


def _bwd_state(ab):
    b, s, _ = ab.shape
    return pl.pallas_call(
        functools.partial(_bwd_state_kernel, rows=s),
        out_shape=jax.ShapeDtypeStruct((b, 1, W_LRU), F32),
        grid=(b,),
        in_specs=[pl.BlockSpec((None, s, 2 * W_LRU), lambda bb: (bb, 0, 0))],
        out_specs=pl.BlockSpec((None, 1, W_LRU), lambda bb: (bb, 0, 0)),
        scratch_shapes=[pltpu.VMEM((s, W_LRU), F32)],
        compiler_params=_params("arbitrary"),
        name="bwd_state",
    )(ab)


def _mlp_kernel(x_ref, g_ref, sh_ref, sc_ref, gt_ref, w1_ref, w2_ref, gf_ref, o_ref, acc_scr,
                *, final):
    x = x_ref[...]
    h = _rms_mod(x, g_ref[...], sh_ref[...], sc_ref[...]).astype(BF16)
    for j in range(D_FF // FF_CHUNK):
        cols = slice(j * FF_CHUNK, (j + 1) * FF_CHUNK)
        u = jnp.dot(h, w1_ref[:, cols], preferred_element_type=F32)
        u = jnp.square(jnp.maximum(u, 0.0)).astype(BF16)
        part = jnp.dot(u, w2_ref[cols, :], preferred_element_type=F32)
        if j == 0:
            acc_scr[...] = part
        else:
            acc_scr[...] += part
    y = x + gt_ref[...] * acc_scr[...]
    if final:
        ms = jnp.mean(y * y, axis=-1, keepdims=True)
        y = y * lax.rsqrt(ms + EPS) * gf_ref[...]
    o_ref[...] = y


def _mlp(x, mod5, g2, w1, w2, g_final, layer, ctx_row, tm, final):
    b, s, d = x.shape
    return pl.pallas_call(
        functools.partial(_mlp_kernel, final=final),
        out_shape=jax.ShapeDtypeStruct(x.shape, F32),
        grid=(b, s // tm),
        in_specs=[
            pl.BlockSpec((None, tm, d), lambda bb, i: (bb, i, 0)),
            _layer_spec((1, d), layer),
            _mod_spec(layer, 3, ctx_row),
            _mod_spec(layer, 4, ctx_row),
            _mod_spec(layer, 5, ctx_row),
            _layer_spec((d, D_FF), layer, single_buffer=True),
            _layer_spec((D_FF, d), layer, single_buffer=True),
            pl.BlockSpec((1, d), lambda bb, i: (0, 0)),
        ],
        out_specs=pl.BlockSpec((None, tm, d), lambda bb, i: (bb, i, 0)),
        scratch_shapes=[pltpu.VMEM((tm, d), F32)],
        compiler_params=_params("arbitrary", "arbitrary"),
        name="mlp",
    )(x, g2, mod5, mod5, mod5, w1, w2, g_final)


def _block_diag(w):
    *lead, h, n, _ = w.shape
    eye = jnp.eye(h, dtype=w.dtype)
    out = jnp.einsum('...hij,hg->...higj', w, eye)
    return out.reshape(*lead, h * n, h * n)


def _tile_rows(s, want):
    t = min(s, want)
    while s % t:
        t //= 2
    return t


def kernel(x, c, ctx, c_ctx, w_mod, b_mod, g_norm1, g_norm2, w_in, conv_a_w, conv_a_b, lru_wa, lru_ba, lru_wx, lru_bx, lru_lambda, pool_w, pool_scale, sgu_ln_g, sgu_ln_b, sgu_w, sgu_b, conv_d_w, conv_d_b, conv_ln_g, conv_ln_b, w_out, w_mlp1, w_mlp2, g_final):
    depth = w_in.shape[0]
    bsz, seq, d = x.shape
    ctx_len = ctx.shape[1]
    assert d == D_MODEL and bsz < MOD_ROWS
    assert seq % SGU_CHUNK == 0 and ctx_len % SGU_CHUNK == 0
    ctx_row = bsz

    w_in_b = w_in.astype(BF16)
    w_out_b = w_out.astype(BF16)
    w1_b = w_mlp1.astype(BF16)
    w2_b = w_mlp2.astype(BF16)
    wa_bd = _block_diag(lru_wa)
    wx_bd = _block_diag(lru_wx)
    wg = jnp.concatenate([wa_bd[:, 0], wx_bd[:, 0], wa_bd[:, 1], wx_bd[:, 1]], axis=-1).astype(BF16)
    bg = jnp.concatenate([lru_ba[:, 0], lru_bx[:, 0], lru_ba[:, 1], lru_bx[:, 1]], axis=-1)[:, None, :]
    lam = lru_lambda.reshape(depth, 1, 2 * W_LRU)
    pool_bd = _block_diag(pool_w).astype(BF16)
    sgu_ws = jnp.swapaxes(sgu_w, 1, 2).reshape(depth, SGU_CHUNK, SGU_HEADS * SGU_CHUNK).astype(BF16)
    sgu_bias = jnp.repeat(jnp.swapaxes(sgu_b, 1, 2), SGU_HEAD_DIM, axis=2)
    row = lambda v: v[:, None, :]

    cs = jnp.zeros((MOD_ROWS, d), F32).at[:bsz].set(c).at[ctx_row].set(c_ctx)
    mod = _modulation(cs, w_mod, b_mod)
    mod5 = mod.reshape(depth, MOD_ROWS, N_MOD, 1, d)

    ts = _tile_rows(seq, 1024)
    ts_c = _tile_rows(ctx_len, 512)
    tm = _tile_rows(seq, 1024)
    tm_c = _tile_rows(bsz * ctx_len, 1024)
    assert ts % GRID_W == 0 and ts % CONV_ROWS == 0 and ts_c % CONV_ROWS == 0
    pos_e = _pos_table()
    xc = ctx
    zeros_h = jnp.zeros((bsz, 1, W_LRU), F32)
    g_fin = g_final.reshape(1, d)
    mix_w = (_pool_bands(), pool_bd, row(pool_scale), row(sgu_ln_g), row(sgu_ln_b), sgu_ws,
             sgu_bias, conv_d_w, row(conv_d_b), row(conv_ln_g), row(conv_ln_b), w_out_b)
    icnt = _pool_icnt(seq, ts)
    icnt_c = _pool_icnt(ctx_len, ts_c)

    for l in range(depth):
        last = l == depth - 1
        first = l == 0
        pc = _inproj(xc.reshape(1, -1, d), pos_e, mod5, row(g_norm1), w_in_b, l, ctx_row, tm_c, False)
        pc = pc.reshape(bsz, ctx_len, D_IN)
        hcf, abc, hcf_last = _lru_fwd(pc, conv_a_w, row(conv_a_b), wg, bg, lam, zeros_h, l, ts_c)
        if last:
            hcb_first = _bwd_state(abc)
        else:
            xc, hcb_first = _mix(xc, pos_e, pc, hcf, abc, zeros_h, mod5, icnt_c, *mix_w, l, ctx_row,
                                 ts_c, False)
            xc = _mlp(xc.reshape(1, -1, d), mod5, row(g_norm2), w1_b, w2_b, g_fin, l, ctx_row, tm_c,
                      False).reshape(bsz, ctx_len, d)
        px = _inproj(x, pos_e, mod5, row(g_norm1), w_in_b, l, None, ts, first)
        hf, ab, _ = _lru_fwd(px, conv_a_w, row(conv_a_b), wg, bg, lam, hcf_last, l, ts)
        x, _ = _mix(x, pos_e, px, hf, ab, hcb_first, mod5, icnt, *mix_w, l, None, ts, first)
        x = _mlp(x, mod5, row(g_norm2), w1_b, w2_b, g_fin, l, None, tm, last)
    return x
```

```python
import functools
import math

import jax
import jax.numpy as jnp
from jax import lax
from jax.experimental import pallas as pl
from jax.experimental.pallas import tpu as pltpu

F32 = jnp.float32
BF16 = jnp.bfloat16

D_MODEL = 1024
GRID_W = 64
W_LRU = 256
W_POOL = 256
W_SGU = 256
W_CONV = 256
LRU_HEADS = 4
LRU_CONV_W = 4
LRU_PAD_L = 2
LRU_C = 8.0
POOL_WINDOWS = (2, 4, 8, 16)
POOL_GROUP = W_POOL // len(POOL_WINDOWS)
SGU_CHUNK = 128
SGU_HEADS = 4
SGU_HEAD_DIM = W_SGU // SGU_HEADS
CONF_CONV_W = 31
CONF_PAD = 15
D_FF = 4 * D_MODEL
D_IN = 2 * W_LRU + W_POOL + 2 * W_SGU + 2 * W_CONV
N_MOD = 6
EPS = 1e-6
POS_BASE = 10000.0

SUBLANES = 8
HALO = 16
VMEM_LIMIT = 56 * 1024 * 1024
MOD_ROWS = 8
FF_CHUNK = 512
CONV_ROWS = 64
POOL_ROWS = 128
INPROJ_CHUNKS = 2


def _params(*sem):
    return pltpu.CompilerParams(dimension_semantics=sem, vmem_limit_bytes=VMEM_LIMIT)


def _sigmoid(x):
    return 0.5 * jnp.tanh(0.5 * x) + 0.5


def _silu(x):
    return x * _sigmoid(x)


def _gelu_tanh(x):
    return 0.5 * x * (1.0 + jnp.tanh(0.7978845608028654 * (x + 0.044715 * (x * x * x))))


def _rms_mod(x, g, shift, scale):
    ms = jnp.mean(x * x, axis=-1, keepdims=True)
    y = x * lax.rsqrt(ms + EPS) * g
    return y * (1.0 + scale) + shift


def _layer_norm(x, g, b):
    mu = jnp.mean(x, axis=-1, keepdims=True)
    xc = x - mu
    var = jnp.mean(xc * xc, axis=-1, keepdims=True)
    return xc * lax.rsqrt(var + EPS) * g + b


def _mod_kernel(c_ref, w_ref, b_ref, o_ref):
    s = _silu(c_ref[...]).astype(BF16)
    o_ref[...] = jnp.dot(s, w_ref[...].astype(BF16), preferred_element_type=F32) + b_ref[...]


def _modulation(cs, w_mod, b_mod):
    depth, _, n = w_mod.shape
    tn = 1536
    return pl.pallas_call(
        _mod_kernel,
        out_shape=jax.ShapeDtypeStruct((depth, MOD_ROWS, n), F32),
        grid=(depth, n // tn),
        in_specs=[
            pl.BlockSpec((MOD_ROWS, D_MODEL), lambda l, j: (0, 0)),
            pl.BlockSpec((None, D_MODEL, tn), lambda l, j: (l, 0, j)),
            pl.BlockSpec((None, 1, tn), lambda l, j: (l, 0, j)),
        ],
        out_specs=pl.BlockSpec((None, MOD_ROWS, tn), lambda l, j: (l, 0, j)),
        compiler_params=_params("arbitrary", "arbitrary"),
        name="modulation",
    )(cs, w_mod, b_mod.reshape(depth, 1, n))


def _pos_table_kernel(o_ref):
    quarter = D_MODEL // 4
    pos = lax.broadcasted_iota(jnp.int32, (GRID_W, quarter), 0).astype(F32)
    j = lax.broadcasted_iota(jnp.int32, (GRID_W, quarter), 1).astype(F32)
    omega = 1.0 / jnp.exp(j * (math.log(POS_BASE) / quarter))
    ang = pos * omega
    o_ref[:, :quarter] = jnp.sin(ang)
    o_ref[:, quarter:] = jnp.cos(ang)


def _pos_table():
    return pl.pallas_call(
        _pos_table_kernel,
        out_shape=jax.ShapeDtypeStruct((GRID_W, D_MODEL // 2), F32),
        name="pos_table",
    )()


def _pos_rows(e_ref, tile, ts):
    reps = ts // GRID_W
    col = jnp.concatenate([e_ref[...]] * reps, axis=0)
    r0 = tile * reps
    rows = [jnp.broadcast_to(e_ref[pl.ds(r0 + k, 1), :], (GRID_W, D_MODEL // 2)) for k in range(reps)]
    return jnp.concatenate([jnp.concatenate(rows, axis=0), col], axis=1)


def _mod_spec(layer, k, ctx_row):
    if ctx_row is None:
        return pl.BlockSpec((None, None, None, 1, D_MODEL), lambda b, i: (layer, b, k, 0, 0))
    return pl.BlockSpec((None, None, None, 1, D_MODEL), lambda b, i: (layer, ctx_row, k, 0, 0))


def _layer_spec(shape, layer, single_buffer=False):
    zeros = (0,) * len(shape)
    kwargs = dict(pipeline_mode=pl.Buffered(1)) if single_buffer else {}
    return pl.BlockSpec((None,) + tuple(shape), lambda b, i: (layer,) + zeros, **kwargs)


def _pos_spec():
    return pl.BlockSpec((GRID_W, D_MODEL // 2), lambda b, i: (0, 0))


def _scan_tile(a_ref, b_ref, out_ref, carry0, n_rows, reverse, unroll=False):
    width = a_ref.shape[-1]
    n_groups = n_rows // SUBLANES
    row = lax.broadcasted_iota(jnp.int32, (SUBLANES, width), 0)

    def group(gi, carry):
        g = (n_groups - 1 - gi) if reverse else gi
        start = g * SUBLANES if unroll else pl.multiple_of(g * SUBLANES, SUBLANES)
        a = a_ref[pl.ds(start, SUBLANES), :]
        h = b_ref[pl.ds(start, SUBLANES), :]
        for s in (1, 2, 4):
            if reverse:
                keep = row < SUBLANES - s
                shift = SUBLANES - s
            else:
                keep = row >= s
                shift = s
            h_sh = jnp.where(keep, pltpu.roll(h, shift, 0), 0.0)
            a_sh = jnp.where(keep, pltpu.roll(a, shift, 0), 1.0)
            h = h + a * h_sh
            a = a * a_sh
        h = h + a * carry
        out_ref[pl.ds(start, SUBLANES), :] = h
        edge = h[0:1, :] if reverse else h[SUBLANES - 1:SUBLANES, :]
        return jnp.broadcast_to(edge, (SUBLANES, width))

    carry = jnp.broadcast_to(carry0, (SUBLANES, width))
    if unroll:
        for gi in range(n_groups):
            carry = group(gi, carry)
        return carry[0:1, :]
    return lax.fori_loop(0, n_groups, group, carry, unroll=2)[0:1, :]


def _proj_lru_kernel(x_ref, xn_ref, e_ref, g_ref, sh_ref, sc_ref, w_ref,
                     cw_ref, cb_ref, wg_ref, bg_ref, lam_ref, h0_ref,
                     px_ref, hf_ref, ab_ref, hlast_ref,
                     ext_scr, tail_scr, a_scr, b_scr, carry_scr, *, ts, add_pos):
    i = pl.program_id(1)
    n_t = pl.num_programs(1)
    reps = ts // GRID_W

    @pl.when(i == 0)
    def _():
        carry_scr[...] = h0_ref[...]
        tail_scr[...] = jnp.zeros(tail_scr.shape, F32)

    ext_scr[0:HALO, :] = tail_scr[...]

    x = x_ref[...]
    if add_pos:
        x = x + _pos_rows(e_ref, i, ts)
    rows = ts // INPROJ_CHUNKS
    for c in range(INPROJ_CHUNKS):
        sl = slice(c * rows, (c + 1) * rows)
        h = _rms_mod(x[sl, :], g_ref[...], sh_ref[...], sc_ref[...])
        res = jnp.dot(h.astype(BF16), w_ref[...], preferred_element_type=F32)
        px_ref[sl, :] = res.astype(px_ref.dtype)
        ext_scr[HALO + c * rows:HALO + (c + 1) * rows, :] = res[:, :W_LRU]

    xn = xn_ref[...]
    if add_pos:
        half = D_MODEL // 2
        pos_n = jnp.concatenate(
            [jnp.broadcast_to(e_ref[pl.ds(jnp.minimum((i + 1) * reps, GRID_W - 1), 1), :],
                              (SUBLANES, half)),
             e_ref[0:SUBLANES, :]], axis=1)
        xn = xn + pos_n
    hn = _rms_mod(xn, g_ref[...], sh_ref[...], sc_ref[...])
    pn = jnp.dot(hn.astype(BF16), w_ref[:, :W_LRU], preferred_element_type=F32)
    ext_scr[HALO + ts:HALO + ts + SUBLANES, :] = jnp.where(i < n_t - 1, pn, 0.0)
    tail_scr[...] = ext_scr[ts:ts + HALO, :]

    cx = jnp.broadcast_to(cb_ref[...], (ts, W_LRU))
    for k in range(LRU_CONV_W):
        off = HALO + k - LRU_PAD_L
        cx = cx + cw_ref[k:k + 1, :] * ext_scr[off:off + ts, :]

    gates = jnp.dot(cx.astype(BF16), wg_ref[...], preferred_element_type=F32) + bg_ref[...]
    neg_lam = -lam_ref[...]
    sp = jnp.maximum(neg_lam, 0.0) + jnp.log1p(jnp.exp(-jnp.abs(neg_lam)))
    lam_c = -LRU_C * sp
    for d in range(2):
        r = _sigmoid(gates[:, (2 * d) * W_LRU:(2 * d + 1) * W_LRU])
        ig = _sigmoid(gates[:, (2 * d + 1) * W_LRU:(2 * d + 2) * W_LRU])
        log_a = lam_c[:, d * W_LRU:(d + 1) * W_LRU] * r
        a = jnp.exp(log_a)
        t = jnp.tanh(log_a)
        z = -2.0 * t / (1.0 - t)
        bmul = jnp.where(z > 0.0, z * lax.rsqrt(z), 0.0)
        bb = bmul * (ig * cx)
        if d == 0:
            a_scr[...] = a
            b_scr[...] = bb
        else:
            ab_ref[:, :W_LRU] = a
            ab_ref[:, W_LRU:] = bb

    carry = _scan_tile(a_scr, b_scr, hf_ref, carry_scr[...], ts, reverse=False, unroll=True)
    carry_scr[...] = carry
    hlast_ref[...] = carry


def _proj_lru(x, pos_e, mod5, g1, w_in, conv_w, conv_b, wg, bg, lam, h0, layer, ctx_row, ts,
              add_pos):
    b, s, d = x.shape
    n8 = ts // SUBLANES
    last8 = s // SUBLANES - 1
    tile = lambda bb, i: (bb, i, 0)
    return pl.pallas_call(
        functools.partial(_proj_lru_kernel, ts=ts, add_pos=add_pos),
        out_shape=(
            jax.ShapeDtypeStruct((b, s, D_IN), BF16),
            jax.ShapeDtypeStruct((b, s, W_LRU), F32),
            jax.ShapeDtypeStruct((b, s, 2 * W_LRU), F32),
            jax.ShapeDtypeStruct((b, 1, W_LRU), F32),
        ),
        grid=(b, s // ts),
        in_specs=[
            pl.BlockSpec((None, ts, d), tile),
            pl.BlockSpec((None, SUBLANES, d), lambda bb, i: (bb, jnp.minimum((i + 1) * n8, last8), 0)),
            _pos_spec(),
            _layer_spec((1, d), layer),
            _mod_spec(layer, 0, ctx_row),
            _mod_spec(layer, 1, ctx_row),
            _layer_spec((d, D_IN), layer),
            _layer_spec((LRU_CONV_W, W_LRU), layer),
            _layer_spec((1, W_LRU), layer),
            _layer_spec((W_LRU, 4 * W_LRU), layer),
            _layer_spec((1, 4 * W_LRU), layer),
            _layer_spec((1, 2 * W_LRU), layer),
            pl.BlockSpec((None, 1, W_LRU), lambda bb, i: (bb, 0, 0)),
        ],
        out_specs=(
            pl.BlockSpec((None, ts, D_IN), tile),
            pl.BlockSpec((None, ts, W_LRU), tile),
            pl.BlockSpec((None, ts, 2 * W_LRU), tile),
            pl.BlockSpec((None, 1, W_LRU), lambda bb, i: (bb, 0, 0)),
        ),
        scratch_shapes=[
            pltpu.VMEM((ts + 2 * HALO, W_LRU), F32),
            pltpu.VMEM((HALO, W_LRU), F32),
            pltpu.VMEM((ts, W_LRU), F32),
            pltpu.VMEM((ts, W_LRU), F32),
            pltpu.VMEM((1, W_LRU), F32),
        ],
        compiler_params=_params("arbitrary", "arbitrary"),
        name="proj_lru",
    )(x, x, pos_e, g1, mod5, mod5, w_in, conv_w, conv_b, wg, bg, lam, h0)


def _pool_windows(shape):
    grp = lax.broadcasted_iota(jnp.int32, shape, 1) // POOL_GROUP
    win = jnp.full(shape, POOL_WINDOWS[-1], jnp.int32)
    for g in range(len(POOL_WINDOWS) - 2, -1, -1):
        win = jnp.where(grp == g, POOL_WINDOWS[g], win)
    return win


def _pool_icnt_kernel(o_ref, *, ts, seq):
    win = _pool_windows((ts, W_POOL))
    lo = win // 2
    hi = win - lo - 1
    tpos = pl.program_id(0) * ts + lax.broadcasted_iota(jnp.int32, (ts, W_POOL), 0)
    cnt = (jnp.minimum(tpos + hi + 1, seq) - jnp.maximum(tpos - lo, 0)).astype(F32)
    o_ref[...] = 1.0 / cnt


def _pool_icnt(seq, ts):
    return pl.pallas_call(
        functools.partial(_pool_icnt_kernel, ts=ts, seq=seq),
        out_shape=jax.ShapeDtypeStruct((seq, W_POOL), F32),
        grid=(seq // ts,),
        out_specs=pl.BlockSpec((ts, W_POOL), lambda i: (i, 0)),
        compiler_params=_params("arbitrary"),
        name="pool_icnt",
    )()


def _pool_bands():
    t = jnp.arange(POOL_ROWS)[:, None]
    tok = jnp.arange(POOL_ROWS + 2 * HALO)[None, :] - HALO
    bands = []
    for w in POOL_WINDOWS:
        lo = w // 2
        hi = w - lo - 1
        bands.append((tok >= t - lo) & (tok <= t + hi))
    return jnp.stack(bands).astype(BF16)


def _mix_kernel(x_ref, e_ref, px_ref, prev_ref, next_ref, hf_ref, ab_ref, h0_ref, gt_ref,
                icnt_ref, pb_ref, pw_ref, ps_ref, slg_ref, slb_ref, sw_ref, sb_ref,
                dw_ref, db_ref, dlg_ref, dlb_ref, wo_ref,
                o_ref, hfirst_ref,
                hb_scr, carry_scr, zext_scr, yext_scr, shift_scr, ycat_scr, *, ts, seq, add_pos):
    i = pl.program_id(1)
    n_t = pl.num_programs(1)
    tile = n_t - 1 - i
    c_gate, c_pool, c_u, c_v, c_cv, c_cg = (W_LRU * k for k in range(1, 7))
    has_prev = tile > 0
    has_next = tile < n_t - 1

    @pl.when(i == 0)
    def _():
        carry_scr[...] = h0_ref[...]

    carry = _scan_tile(ab_ref.at[:, :W_LRU], ab_ref.at[:, W_LRU:], hb_scr, carry_scr[...], ts,
                       reverse=True, unroll=True)
    carry_scr[...] = carry
    hfirst_ref[...] = carry
    y_a = (hf_ref[...] + hb_scr[...]) * _gelu_tanh(px_ref[:, c_gate:c_gate + W_LRU].astype(F32))
    ycat_scr[:, :W_LRU] = y_a.astype(BF16)

    zext_scr[0:HALO, :] = jnp.where(has_prev, prev_ref[:, c_pool:c_pool + W_POOL], 0.0)
    zext_scr[HALO:HALO + ts, :] = px_ref[:, c_pool:c_pool + W_POOL]
    zext_scr[HALO + ts:, :] = jnp.where(has_next, next_ref[:, c_pool:c_pool + W_POOL], 0.0)
    grp = lax.broadcasted_iota(jnp.int32, (POOL_ROWS, W_POOL), 1) // POOL_GROUP
    for blk in range(ts // POOL_ROWS):
        r0 = blk * POOL_ROWS
        zb = zext_scr[r0:r0 + POOL_ROWS + 2 * HALO, :]
        wsum = jnp.dot(pb_ref[len(POOL_WINDOWS) - 1], zb, preferred_element_type=F32)
        for g in range(len(POOL_WINDOWS) - 2, -1, -1):
            wsum = jnp.where(grp == g, jnp.dot(pb_ref[g], zb, preferred_element_type=F32), wsum)
        rows = slice(r0, r0 + POOL_ROWS)
        pooled = wsum * icnt_ref[rows, :] - px_ref[rows, c_pool:c_pool + W_POOL].astype(F32)
        y_b = jnp.dot(pooled.astype(BF16), pw_ref[...], preferred_element_type=F32) * ps_ref[...]
        ycat_scr[rows, W_LRU:W_LRU + W_POOL] = y_b.astype(BF16)

    vn = _layer_norm(px_ref[:, c_v:c_v + W_SGU].astype(F32), slg_ref[...], slb_ref[...]).astype(BF16)
    head = lax.broadcasted_iota(jnp.int32, (SGU_CHUNK, W_SGU), 1) // SGU_HEAD_DIM
    zero = jnp.zeros((SGU_CHUNK, W_SGU), BF16)
    for c in range(ts // SGU_CHUNK):
        rows = slice(c * SGU_CHUNK, (c + 1) * SGU_CHUNK)
        vh = jnp.concatenate([jnp.where(head == h, vn[rows, :], zero) for h in range(SGU_HEADS)],
                             axis=0)
        zc = jnp.dot(sw_ref[...], vh, preferred_element_type=F32) + sb_ref[...]
        y_c = px_ref[rows, c_u:c_u + W_SGU].astype(F32) * zc
        ycat_scr[rows, W_LRU + W_POOL:W_LRU + W_POOL + W_SGU] = y_c.astype(BF16)

    def gated(ref):
        return (ref[:, c_cv:c_cv + W_CONV].astype(F32)
                * _sigmoid(ref[:, c_cg:c_cg + W_CONV].astype(F32)))

    yext_scr[0:HALO, :] = jnp.where(has_prev, gated(prev_ref), 0.0)
    yext_scr[HALO:HALO + ts, :] = gated(px_ref)
    yext_scr[HALO + ts:, :] = jnp.where(has_next, gated(next_ref), 0.0)
    n_shift = ts + 2 * HALO - SUBLANES
    for r in range(1, SUBLANES):
        shift_scr[r - 1] = yext_scr[r:r + n_shift, :]
    for blk in range(ts // CONV_ROWS):
        r0 = blk * CONV_ROWS
        acc = jnp.broadcast_to(db_ref[...], (CONV_ROWS, W_CONV))
        for k in range(CONF_CONV_W):
            off = HALO + k - CONF_PAD
            src = r0 + (off // SUBLANES) * SUBLANES
            if off % SUBLANES:
                tap = shift_scr[off % SUBLANES - 1, src:src + CONV_ROWS, :]
            else:
                tap = yext_scr[src:src + CONV_ROWS, :]
            acc = acc + dw_ref[k:k + 1, :] * tap
        y_d = _silu(_layer_norm(acc, dlg_ref[...], dlb_ref[...]))
        ycat_scr[r0:r0 + CONV_ROWS, W_LRU + W_POOL + W_SGU:] = y_d.astype(BF16)

    x = x_ref[...]
    if add_pos:
        x = x + _pos_rows(e_ref, tile, ts)
    o_ref[...] = x + gt_ref[...] * jnp.dot(ycat_scr[...], wo_ref[...], preferred_element_type=F32)


def _mix(x, pos_e, px, hf, ab, h0b, mod5, icnt, pool_band, pool_bd, pool_scale, sgu_ln_g, sgu_ln_b,
         sgu_ws, sgu_bias, conv_d_w, conv_d_b, conv_ln_g, conv_ln_b, w_out, layer, ctx_row, ts,
         add_pos):
    b, s, d = x.shape
    n_t = s // ts
    nh = ts // HALO
    lasth = s // HALO - 1
    kernel = functools.partial(_mix_kernel, ts=ts, seq=s, add_pos=add_pos)

    def rev(bb, i):
        return (bb, n_t - 1 - i, 0)

    return pl.pallas_call(
        kernel,
        out_shape=(
            jax.ShapeDtypeStruct((b, s, d), F32),
            jax.ShapeDtypeStruct((b, 1, W_LRU), F32),
        ),
        grid=(b, n_t),
        in_specs=[
            pl.BlockSpec((None, ts, d), rev),
            _pos_spec(),
            pl.BlockSpec((None, ts, D_IN), rev),
            pl.BlockSpec((None, HALO, D_IN),
                         lambda bb, i: (bb, jnp.maximum((n_t - 1 - i) * nh - 1, 0), 0)),
            pl.BlockSpec((None, HALO, D_IN),
                         lambda bb, i: (bb, jnp.minimum((n_t - i) * nh, lasth), 0)),
            pl.BlockSpec((None, ts, W_LRU), rev),
            pl.BlockSpec((None, ts, 2 * W_LRU), rev),
            pl.BlockSpec((None, 1, W_LRU), lambda bb, i: (bb, 0, 0)),
            _mod_spec(layer, 2, ctx_row),
            pl.BlockSpec((ts, W_POOL), lambda bb, i: (n_t - 1 - i, 0)),
            pl.BlockSpec(pool_band.shape, lambda bb, i: (0, 0, 0)),
            _layer_spec((W_POOL, W_POOL), layer),
            _layer_spec((1, W_POOL), layer),
            _layer_spec((1, W_SGU), layer),
            _layer_spec((1, W_SGU), layer),
            _layer_spec((SGU_CHUNK, SGU_HEADS * SGU_CHUNK), layer),
            _layer_spec((SGU_CHUNK, W_SGU), layer),
            _layer_spec((CONF_CONV_W, W_CONV), layer),
            _layer_spec((1, W_CONV), layer),
            _layer_spec((1, W_CONV), layer),
            _layer_spec((1, W_CONV), layer),
            _layer_spec((d, d), layer),
        ],
        out_specs=(
            pl.BlockSpec((None, ts, d), rev),
            pl.BlockSpec((None, 1, W_LRU), lambda bb, i: (bb, 0, 0)),
        ),
        scratch_shapes=[
            pltpu.VMEM((ts, W_LRU), F32),
            pltpu.VMEM((1, W_LRU), F32),
            pltpu.VMEM((ts + 2 * HALO, W_POOL), BF16),
            pltpu.VMEM((ts + 2 * HALO, W_CONV), F32),
            pltpu.VMEM((SUBLANES - 1, ts + 2 * HALO - SUBLANES, W_CONV), F32),
            pltpu.VMEM((ts, d), BF16),
        ],
        compiler_params=_params("arbitrary", "arbitrary"),
        name="mix",
    )(x, pos_e, px, px, px, hf, ab, h0b, mod5, icnt, pool_band, pool_bd, pool_scale, sgu_ln_g,
      sgu_ln_b, sgu_ws, sgu_bias, conv_d_w, conv_d_b, conv_ln_g, conv_ln_b, w_out)


def _bwd_state_kernel(ab_ref, o_ref, hb_scr, *, rows):
    zero = jnp.zeros((1, W_LRU), F32)
    o_ref[...] = _scan_tile(ab_ref.at[:, :W_LRU], ab_ref.at[:, W_LRU:], hb_scr, zero, rows,
                            reverse=True)


def _bwd_state(ab):
    b, s, _ = ab.shape
    return pl.pallas_call(
        functools.partial(_bwd_state_kernel, rows=s),
        out_shape=jax.ShapeDtypeStruct((b, 1, W_LRU), F32),
        grid=(b,),
        in_specs=[pl.BlockSpec((None, s, 2 * W_LRU), lambda bb: (bb, 0, 0))],
        out_specs=pl.BlockSpec((None, 1, W_LRU), lambda bb: (bb, 0, 0)),
        scratch_shapes=[pltpu.VMEM((s, W_LRU), F32)],
        compiler_params=_params("arbitrary"),
        name="bwd_state",
    )(ab)


def _mlp_kernel(x_ref, g_ref, sh_ref, sc_ref, gt_ref, w1_ref, w2_ref, gf_ref, o_ref, acc_scr,
                *, final):
    x = x_ref[...]
    h = _rms_mod(x, g_ref[...], sh_ref[...], sc_ref[...]).astype(BF16)
    for j in range(D_FF // FF_CHUNK):
        cols = slice(j * FF_CHUNK, (j + 1) * FF_CHUNK)
        u = jnp.dot(h, w1_ref[:, cols], preferred_element_type=F32)
        u = jnp.square(jnp.maximum(u, 0.0)).astype(BF16)
        part = jnp.dot(u, w2_ref[cols, :], preferred_element_type=F32)
        if j == 0:
            acc_scr[...] = part
        else:
            acc_scr[...] += part
    y = x + gt_ref[...] * acc_scr[...]
    if final:
        ms = jnp.mean(y * y, axis=-1, keepdims=True)
        y = y * lax.rsqrt(ms + EPS) * gf_ref[...]
    o_ref[...] = y


def _mlp(x, mod5, g2, w1, w2, g_final, layer, ctx_row, tm, final):
    b, s, d = x.shape
    return pl.pallas_call(
        functools.partial(_mlp_kernel, final=final),
        out_shape=jax.ShapeDtypeStruct(x.shape, F32),
        grid=(b, s // tm),
        in_specs=[
            pl.BlockSpec((None, tm, d), lambda bb, i: (bb, i, 0)),
            _layer_spec((1, d), layer),
            _mod_spec(layer, 3, ctx_row),
            _mod_spec(layer, 4, ctx_row),
            _mod_spec(layer, 5, ctx_row),
            _layer_spec((d, D_FF), layer, single_buffer=True),
            _layer_spec((D_FF, d), layer, single_buffer=True),
            pl.BlockSpec((1, d), lambda bb, i: (0, 0)),
        ],
        out_specs=pl.BlockSpec((None, tm, d), lambda bb, i: (bb, i, 0)),
        scratch_shapes=[pltpu.VMEM((tm, d), F32)],
        compiler_params=_params("arbitrary", "arbitrary"),
        name="mlp",
    )(x, g2, mod5, mod5, mod5, w1, w2, g_final)


def _block_diag(w):
    *lead, h, n, _ = w.shape
    eye = jnp.eye(h, dtype=w.dtype)
    out = jnp.einsum('...hij,hg->...higj', w, eye)
    return out.reshape(*lead, h * n, h * n)


def _tile_rows(s, want):
    t = min(s, want)
    while s % t:
        t //= 2
    return t


def kernel(x, c, ctx, c_ctx, w_mod, b_mod, g_norm1, g_norm2, w_in, conv_a_w, conv_a_b, lru_wa, lru_ba, lru_wx, lru_bx, lru_lambda, pool_w, pool_scale, sgu_ln_g, sgu_ln_b, sgu_w, sgu_b, conv_d_w, conv_d_b, conv_ln_g, conv_ln_b, w_out, w_mlp1, w_mlp2, g_final):
    depth = w_in.shape[0]
    bsz, seq, d = x.shape
    ctx_len = ctx.shape[1]
    assert d == D_MODEL and bsz < MOD_ROWS
    assert seq % SGU_CHUNK == 0 and ctx_len % SGU_CHUNK == 0
    ctx_row = bsz

    w_in_b = w_in.astype(BF16)
    w_out_b = w_out.astype(BF16)
    w1_b = w_mlp1.astype(BF16)
    w2_b = w_mlp2.astype(BF16)
    wa_bd = _block_diag(lru_wa)
    wx_bd = _block_diag(lru_wx)
    wg = jnp.concatenate([wa_bd[:, 0], wx_bd[:, 0], wa_bd[:, 1], wx_bd[:, 1]], axis=-1).astype(BF16)
    bg = jnp.concatenate([lru_ba[:, 0], lru_bx[:, 0], lru_ba[:, 1], lru_bx[:, 1]], axis=-1)[:, None, :]
    lam = lru_lambda.reshape(depth, 1, 2 * W_LRU)
    pool_bd = _block_diag(pool_w).astype(BF16)
    sgu_ws = jnp.swapaxes(sgu_w, 1, 2).reshape(depth, SGU_CHUNK, SGU_HEADS * SGU_CHUNK).astype(BF16)
    sgu_bias = jnp.repeat(jnp.swapaxes(sgu_b, 1, 2), SGU_HEAD_DIM, axis=2)
    row = lambda v: v[:, None, :]

    cs = jnp.zeros((MOD_ROWS, d), F32).at[:bsz].set(c).at[ctx_row].set(c_ctx)
    mod = _modulation(cs, w_mod, b_mod)
    mod5 = mod.reshape(depth, MOD_ROWS, N_MOD, 1, d)

    ts = _tile_rows(seq, 1024)
    ts_c = _tile_rows(ctx_len, 512)
    tm = _tile_rows(seq, 1024)
    tm_c = _tile_rows(bsz * ctx_len, 1024)
    assert ts % GRID_W == 0 and ts % CONV_ROWS == 0 and ts_c % CONV_ROWS == 0
    pos_e = _pos_table()
    xc = ctx
    zeros_h = jnp.zeros((bsz, 1, W_LRU), F32)
    g_fin = g_final.reshape(1, d)
    mix_w = (_pool_bands(), pool_bd, row(pool_scale), row(sgu_ln_g), row(sgu_ln_b), sgu_ws,
             sgu_bias, conv_d_w, row(conv_d_b), row(conv_ln_g), row(conv_ln_b), w_out_b)
    icnt = _pool_icnt(seq, ts)
    icnt_c = _pool_icnt(ctx_len, ts_c)

    for l in range(depth):
        last = l == depth - 1
        first = l == 0
        pc, hcf, abc, hcf_last = _proj_lru(xc, pos_e, mod5, row(g_norm1), w_in_b, conv_a_w,
                                           row(conv_a_b), wg, bg, lam, zeros_h, l, ctx_row, ts_c,
                                           False)
        if last:
            hcb_first = _bwd_state(abc)
        else:
            xc, hcb_first = _mix(xc, pos_e, pc, hcf, abc, zeros_h, mod5, icnt_c, *mix_w, l, ctx_row,
                                 ts_c, False)
            xc = _mlp(xc.reshape(1, -1, d), mod5, row(g_norm2), w1_b, w2_b, g_fin, l, ctx_row, tm_c,
                      False).reshape(bsz, ctx_len, d)
        px, hf, ab, _ = _proj_lru(x, pos_e, mod5, row(g_norm1), w_in_b, conv_a_w, row(conv_a_b),
                                  wg, bg, lam, hcf_last, l, None, ts, first)
        x, _ = _mix(x, pos_e, px, hf, ab, hcb_first, mod5, icnt, *mix_w, l, None, ts, first)
        x = _mlp(x, mod5, row(g_norm2), w1_b, w2_b, g_fin, l, None, tm, last)
    return x
```

```python
import functools
import math

import jax
import jax.numpy as jnp
from jax import lax
from jax.experimental import pallas as pl
from jax.experimental.pallas import tpu as pltpu

F32 = jnp.float32
BF16 = jnp.bfloat16

D_MODEL = 1024
GRID_W = 64
W_LRU = 256
W_POOL = 256
W_SGU = 256
W_CONV = 256
LRU_HEADS = 4
LRU_CONV_W = 4
LRU_PAD_L = 2
LRU_C = 8.0
POOL_WINDOWS = (2, 4, 8, 16)
POOL_GROUP = W_POOL // len(POOL_WINDOWS)
SGU_CHUNK = 128
SGU_HEADS = 4
SGU_HEAD_DIM = W_SGU // SGU_HEADS
CONF_CONV_W = 31
CONF_PAD = 15
D_FF = 4 * D_MODEL
D_IN = 2 * W_LRU + W_POOL + 2 * W_SGU + 2 * W_CONV
D_MIX_IN = D_IN - W_LRU
N_MOD = 6
EPS = 1e-6
POS_BASE = 10000.0

SUBLANES = 8
HALO = 16
VMEM_LIMIT = 56 * 1024 * 1024
MOD_ROWS = 8
FF_CHUNK = 512
CONV_ROWS = 64
POOL_ROWS = 128
INPROJ_CHUNKS = 2


def _params(*sem):
    return pltpu.CompilerParams(dimension_semantics=sem, vmem_limit_bytes=VMEM_LIMIT)


def _sigmoid(x):
    return 0.5 * jnp.tanh(0.5 * x) + 0.5


def _silu(x):
    return x * _sigmoid(x)


def _gelu_tanh(x):
    return 0.5 * x * (1.0 + jnp.tanh(0.7978845608028654 * (x + 0.044715 * (x * x * x))))


def _rms_mod(x, g, shift, scale):
    ms = jnp.mean(x * x, axis=-1, keepdims=True)
    y = x * lax.rsqrt(ms + EPS) * g
    return y * (1.0 + scale) + shift


def _layer_norm(x, g, b):
    mu = jnp.mean(x, axis=-1, keepdims=True)
    xc = x - mu
    var = jnp.mean(xc * xc, axis=-1, keepdims=True)
    return xc * lax.rsqrt(var + EPS) * g + b


def _mod_kernel(c_ref, w_ref, b_ref, o_ref):
    s = _silu(c_ref[...]).astype(BF16)
    o_ref[...] = jnp.dot(s, w_ref[...].astype(BF16), preferred_element_type=F32) + b_ref[...]


def _modulation(cs, w_mod, b_mod):
    depth, _, n = w_mod.shape
    tn = 1536
    return pl.pallas_call(
        _mod_kernel,
        out_shape=jax.ShapeDtypeStruct((depth, MOD_ROWS, n), F32),
        grid=(depth, n // tn),
        in_specs=[
            pl.BlockSpec((MOD_ROWS, D_MODEL), lambda l, j: (0, 0)),
            pl.BlockSpec((None, D_MODEL, tn), lambda l, j: (l, 0, j)),
            pl.BlockSpec((None, 1, tn), lambda l, j: (l, 0, j)),
        ],
        out_specs=pl.BlockSpec((None, MOD_ROWS, tn), lambda l, j: (l, 0, j)),
        compiler_params=_params("arbitrary", "arbitrary"),
        name="modulation",
    )(cs, w_mod, b_mod.reshape(depth, 1, n))


def _pos_table_kernel(o_ref):
    quarter = D_MODEL // 4
    pos = lax.broadcasted_iota(jnp.int32, (GRID_W, quarter), 0).astype(F32)
    j = lax.broadcasted_iota(jnp.int32, (GRID_W, quarter), 1).astype(F32)
    omega = 1.0 / jnp.exp(j * (math.log(POS_BASE) / quarter))
    ang = pos * omega
    o_ref[:, :quarter] = jnp.sin(ang)
    o_ref[:, quarter:] = jnp.cos(ang)


def _pos_table():
    return pl.pallas_call(
        _pos_table_kernel,
        out_shape=jax.ShapeDtypeStruct((GRID_W, D_MODEL // 2), F32),
        name="pos_table",
    )()


def _pos_rows(e_ref, tile, ts):
    reps = ts // GRID_W
    col = jnp.concatenate([e_ref[...]] * reps, axis=0)
    r0 = tile * reps
    rows = [jnp.broadcast_to(e_ref[pl.ds(r0 + k, 1), :], (GRID_W, D_MODEL // 2)) for k in range(reps)]
    return jnp.concatenate([jnp.concatenate(rows, axis=0), col], axis=1)


def _mod_spec(layer, k, ctx_row):
    if ctx_row is None:
        return pl.BlockSpec((None, None, None, 1, D_MODEL), lambda b, i: (layer, b, k, 0, 0))
    return pl.BlockSpec((None, None, None, 1, D_MODEL), lambda b, i: (layer, ctx_row, k, 0, 0))


def _layer_spec(shape, layer, single_buffer=False):
    zeros = (0,) * len(shape)
    kwargs = dict(pipeline_mode=pl.Buffered(1)) if single_buffer else {}
    return pl.BlockSpec((None,) + tuple(shape), lambda b, i: (layer,) + zeros, **kwargs)


def _pos_spec():
    return pl.BlockSpec((GRID_W, D_MODEL // 2), lambda b, i: (0, 0))


def _scan_tile(a_ref, b_ref, out_ref, carry0, n_rows, reverse, unroll=False):
    width = a_ref.shape[-1]
    n_groups = n_rows // SUBLANES
    row = lax.broadcasted_iota(jnp.int32, (SUBLANES, width), 0)

    def group(gi, carry):
        g = (n_groups - 1 - gi) if reverse else gi
        start = g * SUBLANES if unroll else pl.multiple_of(g * SUBLANES, SUBLANES)
        a = a_ref[pl.ds(start, SUBLANES), :]
        h = b_ref[pl.ds(start, SUBLANES), :]
        for s in (1, 2, 4):
            if reverse:
                keep = row < SUBLANES - s
                shift = SUBLANES - s
            else:
                keep = row >= s
                shift = s
            h_sh = jnp.where(keep, pltpu.roll(h, shift, 0), 0.0)
            a_sh = jnp.where(keep, pltpu.roll(a, shift, 0), 1.0)
            h = h + a * h_sh
            a = a * a_sh
        h = h + a * carry
        out_ref[pl.ds(start, SUBLANES), :] = h
        edge = h[0:1, :] if reverse else h[SUBLANES - 1:SUBLANES, :]
        return jnp.broadcast_to(edge, (SUBLANES, width))

    carry = jnp.broadcast_to(carry0, (SUBLANES, width))
    if unroll:
        for gi in range(n_groups):
            carry = group(gi, carry)
        return carry[0:1, :]
    return lax.fori_loop(0, n_groups, group, carry, unroll=2)[0:1, :]


def _proj_lru_kernel(x_ref, xn_ref, e_ref, g_ref, sh_ref, sc_ref, w_ref,
                     cw_ref, cb_ref, wg_ref, bg_ref, lam_ref, h0_ref,
                     px_ref, lru_ref, hlast_ref,
                     ext_scr, tail_scr, a_scr, b_scr, carry_scr, *, ts, add_pos):
    i = pl.program_id(1)
    n_t = pl.num_programs(1)
    reps = ts // GRID_W

    @pl.when(i == 0)
    def _():
        carry_scr[...] = h0_ref[...]
        tail_scr[...] = jnp.zeros(tail_scr.shape, F32)

    ext_scr[0:HALO, :] = tail_scr[...]

    x = x_ref[...]
    if add_pos:
        x = x + _pos_rows(e_ref, i, ts)
    rows = ts // INPROJ_CHUNKS
    for c in range(INPROJ_CHUNKS):
        sl = slice(c * rows, (c + 1) * rows)
        h = _rms_mod(x[sl, :], g_ref[...], sh_ref[...], sc_ref[...])
        res = jnp.dot(h.astype(BF16), w_ref[...], preferred_element_type=F32)
        px_ref[sl, :] = res[:, W_LRU:].astype(px_ref.dtype)
        ext_scr[HALO + c * rows:HALO + (c + 1) * rows, :] = res[:, :W_LRU]

    xn = xn_ref[...]
    if add_pos:
        half = D_MODEL // 2
        pos_n = jnp.concatenate(
            [jnp.broadcast_to(e_ref[pl.ds(jnp.minimum((i + 1) * reps, GRID_W - 1), 1), :],
                              (SUBLANES, half)),
             e_ref[0:SUBLANES, :]], axis=1)
        xn = xn + pos_n
    hn = _rms_mod(xn, g_ref[...], sh_ref[...], sc_ref[...])
    pn = jnp.dot(hn.astype(BF16), w_ref[:, :W_LRU], preferred_element_type=F32)
    ext_scr[HALO + ts:HALO + ts + SUBLANES, :] = jnp.where(i < n_t - 1, pn, 0.0)
    tail_scr[...] = ext_scr[ts:ts + HALO, :]

    cx = jnp.broadcast_to(cb_ref[...], (ts, W_LRU))
    for k in range(LRU_CONV_W):
        off = HALO + k - LRU_PAD_L
        cx = cx + cw_ref[k:k + 1, :] * ext_scr[off:off + ts, :]

    gates = jnp.dot(cx.astype(BF16), wg_ref[...], preferred_element_type=F32) + bg_ref[...]
    neg_lam = -lam_ref[...]
    sp = jnp.maximum(neg_lam, 0.0) + jnp.log1p(jnp.exp(-jnp.abs(neg_lam)))
    lam_c = -LRU_C * sp
    for d in range(2):
        r = _sigmoid(gates[:, (2 * d) * W_LRU:(2 * d + 1) * W_LRU])
        ig = _sigmoid(gates[:, (2 * d + 1) * W_LRU:(2 * d + 2) * W_LRU])
        log_a = lam_c[:, d * W_LRU:(d + 1) * W_LRU] * r
        a = jnp.exp(log_a)
        t = jnp.tanh(log_a)
        z = -2.0 * t / (1.0 - t)
        bmul = jnp.where(z > 0.0, z * lax.rsqrt(z), 0.0)
        bb = bmul * (ig * cx)
        if d == 0:
            a_scr[...] = a
            b_scr[...] = bb
        else:
            lru_ref[:, W_LRU:2 * W_LRU] = a
            lru_ref[:, 2 * W_LRU:] = bb

    carry = _scan_tile(a_scr, b_scr, lru_ref.at[:, :W_LRU], carry_scr[...], ts, reverse=False,
                       unroll=True)
    carry_scr[...] = carry
    hlast_ref[...] = carry


def _proj_lru(x, pos_e, mod5, g1, w_in, conv_w, conv_b, wg, bg, lam, h0, layer, ctx_row, ts,
              add_pos):
    b, s, d = x.shape
    n8 = ts // SUBLANES
    last8 = s // SUBLANES - 1
    tile = lambda bb, i: (bb, i, 0)
    return pl.pallas_call(
        functools.partial(_proj_lru_kernel, ts=ts, add_pos=add_pos),
        out_shape=(
            jax.ShapeDtypeStruct((b, s, D_MIX_IN), BF16),
            jax.ShapeDtypeStruct((b, s, 3 * W_LRU), F32),
            jax.ShapeDtypeStruct((b, 1, W_LRU), F32),
        ),
        grid=(b, s // ts),
        in_specs=[
            pl.BlockSpec((None, ts, d), tile),
            pl.BlockSpec((None, SUBLANES, d), lambda bb, i: (bb, jnp.minimum((i + 1) * n8, last8), 0)),
            _pos_spec(),
            _layer_spec((1, d), layer),
            _mod_spec(layer, 0, ctx_row),
            _mod_spec(layer, 1, ctx_row),
            _layer_spec((d, D_IN), layer),
            _layer_spec((LRU_CONV_W, W_LRU), layer),
            _layer_spec((1, W_LRU), layer),
            _layer_spec((W_LRU, 4 * W_LRU), layer),
            _layer_spec((1, 4 * W_LRU), layer),
            _layer_spec((1, 2 * W_LRU), layer),
            pl.BlockSpec((None, 1, W_LRU), lambda bb, i: (bb, 0, 0)),
        ],
        out_specs=(
            pl.BlockSpec((None, ts, D_MIX_IN), tile),
            pl.BlockSpec((None, ts, 3 * W_LRU), tile),
            pl.BlockSpec((None, 1, W_LRU), lambda bb, i: (bb, 0, 0)),
        ),
        scratch_shapes=[
            pltpu.VMEM((ts + 2 * HALO, W_LRU), F32),
            pltpu.VMEM((HALO, W_LRU), F32),
            pltpu.VMEM((ts, W_LRU), F32),
            pltpu.VMEM((ts, W_LRU), F32),
            pltpu.VMEM((1, W_LRU), F32),
        ],
        compiler_params=_params("arbitrary", "arbitrary"),
        name="proj_lru",
    )(x, x, pos_e, g1, mod5, mod5, w_in, conv_w, conv_b, wg, bg, lam, h0)


def _pool_windows(shape):
    grp = lax.broadcasted_iota(jnp.int32, shape, 1) // POOL_GROUP
    win = jnp.full(shape, POOL_WINDOWS[-1], jnp.int32)
    for g in range(len(POOL_WINDOWS) - 2, -1, -1):
        win = jnp.where(grp == g, POOL_WINDOWS[g], win)
    return win


def _pool_icnt_kernel(o_ref, *, ts, seq):
    win = _pool_windows((ts, W_POOL))
    lo = win // 2
    hi = win - lo - 1
    tpos = pl.program_id(0) * ts + lax.broadcasted_iota(jnp.int32, (ts, W_POOL), 0)
    cnt = (jnp.minimum(tpos + hi + 1, seq) - jnp.maximum(tpos - lo, 0)).astype(F32)
    o_ref[...] = 1.0 / cnt


def _pool_icnt(seq, ts):
    return pl.pallas_call(
        functools.partial(_pool_icnt_kernel, ts=ts, seq=seq),
        out_shape=jax.ShapeDtypeStruct((seq, W_POOL), F32),
        grid=(seq // ts,),
        out_specs=pl.BlockSpec((ts, W_POOL), lambda i: (i, 0)),
        compiler_params=_params("arbitrary"),
        name="pool_icnt",
    )()


def _pool_bands():
    t = jnp.arange(POOL_ROWS)[:, None]
    tok = jnp.arange(POOL_ROWS + 2 * HALO)[None, :] - HALO
    bands = []
    for w in POOL_WINDOWS:
        lo = w // 2
        hi = w - lo - 1
        bands.append((tok >= t - lo) & (tok <= t + hi))
    return jnp.stack(bands).astype(BF16)


def _mix_kernel(x_ref, e_ref, px_ref, prev_ref, next_ref, lru_ref, h0_ref, gt_ref,
                icnt_ref, pb_ref, pw_ref, ps_ref, slg_ref, slb_ref, sw_ref, sb_ref,
                dw_ref, db_ref, dlg_ref, dlb_ref, wo_ref,
                o_ref, hfirst_ref,
                hb_scr, carry_scr, zext_scr, yext_scr, shift_scr, ycat_scr, *, ts, add_pos):
    i = pl.program_id(1)
    n_t = pl.num_programs(1)
    tile = n_t - 1 - i
    c_gate, c_pool, c_u, c_v, c_cv, c_cg = (W_LRU * k for k in range(6))
    hf_ref = lru_ref.at[:, :W_LRU]
    ab_ref = lru_ref.at[:, W_LRU:]
    has_prev = tile > 0
    has_next = tile < n_t - 1

    @pl.when(i == 0)
    def _():
        carry_scr[...] = h0_ref[...]

    carry = _scan_tile(ab_ref.at[:, :W_LRU], ab_ref.at[:, W_LRU:], hb_scr, carry_scr[...], ts,
                       reverse=True, unroll=True)
    carry_scr[...] = carry
    hfirst_ref[...] = carry
    y_a = (hf_ref[...] + hb_scr[...]) * _gelu_tanh(px_ref[:, c_gate:c_gate + W_LRU].astype(F32))
    ycat_scr[:, :W_LRU] = y_a.astype(BF16)

    zext_scr[0:HALO, :] = jnp.where(has_prev, prev_ref[:, c_pool:c_pool + W_POOL], 0.0)
    zext_scr[HALO:HALO + ts, :] = px_ref[:, c_pool:c_pool + W_POOL]
    zext_scr[HALO + ts:, :] = jnp.where(has_next, next_ref[:, c_pool:c_pool + W_POOL], 0.0)
    grp = lax.broadcasted_iota(jnp.int32, (POOL_ROWS, W_POOL), 1) // POOL_GROUP
    for blk in range(ts // POOL_ROWS):
        r0 = blk * POOL_ROWS
        zb = zext_scr[r0:r0 + POOL_ROWS + 2 * HALO, :]
        wsum = jnp.dot(pb_ref[len(POOL_WINDOWS) - 1], zb, preferred_element_type=F32)
        for g in range(len(POOL_WINDOWS) - 2, -1, -1):
            wsum = jnp.where(grp == g, jnp.dot(pb_ref[g], zb, preferred_element_type=F32), wsum)
        rows = slice(r0, r0 + POOL_ROWS)
        pooled = wsum * icnt_ref[rows, :] - px_ref[rows, c_pool:c_pool + W_POOL].astype(F32)
        y_b = jnp.dot(pooled.astype(BF16), pw_ref[...], preferred_element_type=F32) * ps_ref[...]
        ycat_scr[rows, W_LRU:W_LRU + W_POOL] = y_b.astype(BF16)

    vn = _layer_norm(px_ref[:, c_v:c_v + W_SGU].astype(F32), slg_ref[...], slb_ref[...]).astype(BF16)
    head = lax.broadcasted_iota(jnp.int32, (SGU_CHUNK, W_SGU), 1) // SGU_HEAD_DIM
    zero = jnp.zeros((SGU_CHUNK, W_SGU), BF16)
    for c in range(ts // SGU_CHUNK):
        rows = slice(c * SGU_CHUNK, (c + 1) * SGU_CHUNK)
        vh = jnp.concatenate([jnp.where(head == h, vn[rows, :], zero) for h in range(SGU_HEADS)],
                             axis=0)
        zc = jnp.dot(sw_ref[...], vh, preferred_element_type=F32) + sb_ref[...]
        y_c = px_ref[rows, c_u:c_u + W_SGU].astype(F32) * zc
        ycat_scr[rows, W_LRU + W_POOL:W_LRU + W_POOL + W_SGU] = y_c.astype(BF16)

    def gated(ref):
        return (ref[:, c_cv:c_cv + W_CONV].astype(F32)
                * _sigmoid(ref[:, c_cg:c_cg + W_CONV].astype(F32)))

    yext_scr[0:HALO, :] = jnp.where(has_prev, gated(prev_ref), 0.0)
    yext_scr[HALO:HALO + ts, :] = gated(px_ref)
    yext_scr[HALO + ts:, :] = jnp.where(has_next, gated(next_ref), 0.0)
    n_shift = ts + 2 * HALO - SUBLANES
    for r in range(1, SUBLANES):
        shift_scr[r - 1] = yext_scr[r:r + n_shift, :]
    for blk in range(ts // CONV_ROWS):
        r0 = blk * CONV_ROWS
        acc = jnp.broadcast_to(db_ref[...], (CONV_ROWS, W_CONV))
        for k in range(CONF_CONV_W):
            off = HALO + k - CONF_PAD
            src = r0 + (off // SUBLANES) * SUBLANES
            if off % SUBLANES:
                tap = shift_scr[off % SUBLANES - 1, src:src + CONV_ROWS, :]
            else:
                tap = yext_scr[src:src + CONV_ROWS, :]
            acc = acc + dw_ref[k:k + 1, :] * tap
        y_d = _silu(_layer_norm(acc, dlg_ref[...], dlb_ref[...]))
        ycat_scr[r0:r0 + CONV_ROWS, W_LRU + W_POOL + W_SGU:] = y_d.astype(BF16)

    x = x_ref[...]
    if add_pos:
        x = x + _pos_rows(e_ref, tile, ts)
    o_ref[...] = x + gt_ref[...] * jnp.dot(ycat_scr[...], wo_ref[...], preferred_element_type=F32)


def _mix(x, pos_e, px, lru, h0b, mod5, icnt, pool_band, pool_bd, pool_scale, sgu_ln_g, sgu_ln_b,
         sgu_ws, sgu_bias, conv_d_w, conv_d_b, conv_ln_g, conv_ln_b, w_out, layer, ctx_row, ts,
         add_pos):
    b, s, d = x.shape
    n_t = s // ts
    nh = ts // HALO
    lasth = s // HALO - 1
    kernel = functools.partial(_mix_kernel, ts=ts, add_pos=add_pos)

    def rev(bb, i):
        return (bb, n_t - 1 - i, 0)

    def icnt_map(bb, i):
        tile = n_t - 1 - i
        return (jnp.where(tile == 0, 0, jnp.where(tile == n_t - 1, 2, 1)), 0)

    return pl.pallas_call(
        kernel,
        out_shape=(
            jax.ShapeDtypeStruct((b, s, d), F32),
            jax.ShapeDtypeStruct((b, 1, W_LRU), F32),
        ),
        grid=(b, n_t),
        in_specs=[
            pl.BlockSpec((None, ts, d), rev),
            _pos_spec(),
            pl.BlockSpec((None, ts, D_MIX_IN), rev),
            pl.BlockSpec((None, HALO, D_MIX_IN),
                         lambda bb, i: (bb, jnp.maximum((n_t - 1 - i) * nh - 1, 0), 0)),
            pl.BlockSpec((None, HALO, D_MIX_IN),
                         lambda bb, i: (bb, jnp.minimum((n_t - i) * nh, lasth), 0)),
            pl.BlockSpec((None, ts, 3 * W_LRU), rev),
            pl.BlockSpec((None, 1, W_LRU), lambda bb, i: (bb, 0, 0)),
            _mod_spec(layer, 2, ctx_row),
            pl.BlockSpec((ts, W_POOL), icnt_map),
            pl.BlockSpec(pool_band.shape, lambda bb, i: (0, 0, 0)),
            _layer_spec((W_POOL, W_POOL), layer),
            _layer_spec((1, W_POOL), layer),
            _layer_spec((1, W_SGU), layer),
            _layer_spec((1, W_SGU), layer),
            _layer_spec((SGU_CHUNK, SGU_HEADS * SGU_CHUNK), layer),
            _layer_spec((SGU_CHUNK, W_SGU), layer),
            _layer_spec((CONF_CONV_W, W_CONV), layer),
            _layer_spec((1, W_CONV), layer),
            _layer_spec((1, W_CONV), layer),
            _layer_spec((1, W_CONV), layer),
            _layer_spec((d, d), layer),
        ],
        out_specs=(
            pl.BlockSpec((None, ts, d), rev),
            pl.BlockSpec((None, 1, W_LRU), lambda bb, i: (bb, 0, 0)),
        ),
        scratch_shapes=[
            pltpu.VMEM((ts, W_LRU), F32),
            pltpu.VMEM((1, W_LRU), F32),
            pltpu.VMEM((ts + 2 * HALO, W_POOL), BF16),
            pltpu.VMEM((ts + 2 * HALO, W_CONV), F32),
            pltpu.VMEM((SUBLANES - 1, ts + 2 * HALO - SUBLANES, W_CONV), F32),
            pltpu.VMEM((ts, d), BF16),
        ],
        compiler_params=_params("arbitrary", "arbitrary"),
        name="mix",
    )(x, pos_e, px, px, px, lru, h0b, mod5, icnt, pool_band, pool_bd, pool_scale, sgu_ln_g,
      sgu_ln_b, sgu_ws, sgu_bias, conv_d_w, conv_d_b, conv_ln_g, conv_ln_b, w_out)


def _bwd_state_kernel(lru_ref, o_ref, hb_scr, *, rows):
    zero = jnp.zeros((1, W_LRU), F32)
    o_ref[...] = _scan_tile(lru_ref.at[:, W_LRU:2 * W_LRU], lru_ref.at[:, 2 * W_LRU:], hb_scr, zero,
                            rows, reverse=True)


def _bwd_state(lru):
    b, s, _ = lru.shape
    return pl.pallas_call(
        functools.partial(_bwd_state_kernel, rows=s),
        out_shape=jax.ShapeDtypeStruct((b, 1, W_LRU), F32),
        grid=(b,),
        in_specs=[pl.BlockSpec((None, s, 3 * W_LRU), lambda bb: (bb, 0, 0))],
        out_specs=pl.BlockSpec((None, 1, W_LRU), lambda bb: (bb, 0, 0)),
        scratch_shapes=[pltpu.VMEM((s, W_LRU), F32)],
        compiler_params=_params("arbitrary"),
        name="bwd_state",
    )(lru)


def _mlp_kernel(x_ref, g_ref, sh_ref, sc_ref, gt_ref, w1_ref, w2_ref, gf_ref, o_ref, acc_scr,
                *, final):
    x = x_ref[...]
    h = _rms_mod(x, g_ref[...], sh_ref[...], sc_ref[...]).astype(BF16)
    for j in range(D_FF // FF_CHUNK):
        cols = slice(j * FF_CHUNK, (j + 1) * FF_CHUNK)
        u = jnp.dot(h, w1_ref[:, cols], preferred_element_type=F32)
        u = jnp.square(jnp.maximum(u, 0.0)).astype(BF16)
        part = jnp.dot(u, w2_ref[cols, :], preferred_element_type=F32)
        if j == 0:
            acc_scr[...] = part
        else:
            acc_scr[...] += part
    y = x + gt_ref[...] * acc_scr[...]
    if final:
        ms = jnp.mean(y * y, axis=-1, keepdims=True)
        y = y * lax.rsqrt(ms + EPS) * gf_ref[...]
    o_ref[...] = y


def _mlp(x, mod5, g2, w1, w2, g_final, layer, ctx_row, tm, final):
    b, s, d = x.shape
    return pl.pallas_call(
        functools.partial(_mlp_kernel, final=final),
        out_shape=jax.ShapeDtypeStruct(x.shape, F32),
        grid=(b, s // tm),
        in_specs=[
            pl.BlockSpec((None, tm, d), lambda bb, i: (bb, i, 0)),
            _layer_spec((1, d), layer),
            _mod_spec(layer, 3, ctx_row),
            _mod_spec(layer, 4, ctx_row),
            _mod_spec(layer, 5, ctx_row),
            _layer_spec((d, D_FF), layer, single_buffer=True),
            _layer_spec((D_FF, d), layer, single_buffer=True),
            pl.BlockSpec((1, d), lambda bb, i: (0, 0)),
        ],
        out_specs=pl.BlockSpec((None, tm, d), lambda bb, i: (bb, i, 0)),
        scratch_shapes=[pltpu.VMEM((tm, d), F32)],
        compiler_params=_params("arbitrary", "arbitrary"),
        name="mlp",
    )(x, g2, mod5, mod5, mod5, w1, w2, g_final)


def _block_diag(w):
    *lead, h, n, _ = w.shape
    eye = jnp.eye(h, dtype=w.dtype)
    out = jnp.einsum('...hij,hg->...higj', w, eye)
    return out.reshape(*lead, h * n, h * n)


def _edge_tiles(table, ts):
    n_t = table.shape[0] // ts
    mid = min(1, n_t - 1)
    return jnp.concatenate([table[:ts], table[mid * ts:(mid + 1) * ts], table[-ts:]], axis=0)


def _tile_rows(s, want):
    t = min(s, want)
    while s % t:
        t //= 2
    return t


def kernel(x, c, ctx, c_ctx, w_mod, b_mod, g_norm1, g_norm2, w_in, conv_a_w, conv_a_b, lru_wa, lru_ba, lru_wx, lru_bx, lru_lambda, pool_w, pool_scale, sgu_ln_g, sgu_ln_b, sgu_w, sgu_b, conv_d_w, conv_d_b, conv_ln_g, conv_ln_b, w_out, w_mlp1, w_mlp2, g_final):
    depth = w_in.shape[0]
    bsz, seq, d = x.shape
    ctx_len = ctx.shape[1]
    assert d == D_MODEL and bsz < MOD_ROWS
    assert seq % SGU_CHUNK == 0 and ctx_len % SGU_CHUNK == 0
    ctx_row = bsz

    w_in_b = w_in.astype(BF16)
    w_out_b = w_out.astype(BF16)
    w1_b = w_mlp1.astype(BF16)
    w2_b = w_mlp2.astype(BF16)
    wa_bd = _block_diag(lru_wa)
    wx_bd = _block_diag(lru_wx)
    wg = jnp.concatenate([wa_bd[:, 0], wx_bd[:, 0], wa_bd[:, 1], wx_bd[:, 1]], axis=-1).astype(BF16)
    bg = jnp.concatenate([lru_ba[:, 0], lru_bx[:, 0], lru_ba[:, 1], lru_bx[:, 1]], axis=-1)[:, None, :]
    lam = lru_lambda.reshape(depth, 1, 2 * W_LRU)
    pool_bd = _block_diag(pool_w).astype(BF16)
    sgu_ws = jnp.swapaxes(sgu_w, 1, 2).reshape(depth, SGU_CHUNK, SGU_HEADS * SGU_CHUNK).astype(BF16)
    sgu_bias = jnp.repeat(jnp.swapaxes(sgu_b, 1, 2), SGU_HEAD_DIM, axis=2)
    row = lambda v: v[:, None, :]

    cs = jnp.zeros((MOD_ROWS, d), F32).at[:bsz].set(c).at[ctx_row].set(c_ctx)
    mod = _modulation(cs, w_mod, b_mod)
    mod5 = mod.reshape(depth, MOD_ROWS, N_MOD, 1, d)

    ts = _tile_rows(seq, 1024)
    ts_c = _tile_rows(ctx_len, 512)
    tm = _tile_rows(seq, 1024)
    tm_c = _tile_rows(bsz * ctx_len, 1024)
    assert ts % GRID_W == 0 and ts % CONV_ROWS == 0 and ts_c % CONV_ROWS == 0
    pos_e = _pos_table()
    xc = ctx
    zeros_h = jnp.zeros((bsz, 1, W_LRU), F32)
    g_fin = g_final.reshape(1, d)
    mix_w = (_pool_bands(), pool_bd, row(pool_scale), row(sgu_ln_g), row(sgu_ln_b), sgu_ws,
             sgu_bias, conv_d_w, row(conv_d_b), row(conv_ln_g), row(conv_ln_b), w_out_b)
    icnt = _edge_tiles(_pool_icnt(seq, ts), ts)
    icnt_c = _edge_tiles(_pool_icnt(ctx_len, ts_c), ts_c)

    for l in range(depth):
        last = l == depth - 1
        first = l == 0
        pc, lru_c, hcf_last = _proj_lru(xc, pos_e, mod5, row(g_norm1), w_in_b, conv_a_w,
                                        row(conv_a_b), wg, bg, lam, zeros_h, l, ctx_row, ts_c, False)
        if last:
            hcb_first = _bwd_state(lru_c)
        else:
            xc, hcb_first = _mix(xc, pos_e, pc, lru_c, zeros_h, mod5, icnt_c, *mix_w, l, ctx_row,
                                 ts_c, False)
            xc = _mlp(xc.reshape(1, -1, d), mod5, row(g_norm2), w1_b, w2_b, g_fin, l, ctx_row, tm_c,
                      False).reshape(bsz, ctx_len, d)
        px, lru, _ = _proj_lru(x, pos_e, mod5, row(g_norm1), w_in_b, conv_a_w, row(conv_a_b),
                               wg, bg, lam, hcf_last, l, None, ts, first)
        x, _ = _mix(x, pos_e, px, lru, hcb_first, mod5, icnt, *mix_w, l, None, ts, first)
        x = _mlp(x, mod5, row(g_norm2), w1_b, w2_b, g_fin, l, None, tm, last)
    return x
```

```python
import functools
import math

import jax
import jax.numpy as jnp
from jax import lax
from jax.experimental import pallas as pl
from jax.experimental.pallas import tpu as pltpu

F32 = jnp.float32
BF16 = jnp.bfloat16

D_MODEL = 1024
GRID_W = 64
W_LRU = 256
W_POOL = 256
W_SGU = 256
W_CONV = 256
LRU_HEADS = 4
LRU_CONV_W = 4
LRU_PAD_L = 2
LRU_C = 8.0
POOL_WINDOWS = (2, 4, 8, 16)
POOL_GROUP = W_POOL // len(POOL_WINDOWS)
SGU_CHUNK = 128
SGU_HEADS = 4
SGU_HEAD_DIM = W_SGU // SGU_HEADS
CONF_CONV_W = 31
CONF_PAD = 15
D_FF = 4 * D_MODEL
D_IN = 2 * W_LRU + W_POOL + 2 * W_SGU + 2 * W_CONV
D_MIX_IN = D_IN - W_LRU
N_MOD = 6
EPS = 1e-6
POS_BASE = 10000.0

SUBLANES = 8
HALO = 16
VMEM_LIMIT = 56 * 1024 * 1024
MOD_ROWS = 8
FF_CHUNK = 512
CONV_ROWS = 256
POOL_ROWS = 128
INPROJ_CHUNKS = 2


def _params(*sem):
    return pltpu.CompilerParams(dimension_semantics=sem, vmem_limit_bytes=VMEM_LIMIT)


def _sigmoid(x):
    return 0.5 * jnp.tanh(0.5 * x) + 0.5


def _silu(x):
    return x * _sigmoid(x)


def _gelu_tanh(x):
    return 0.5 * x * (1.0 + jnp.tanh(0.7978845608028654 * (x + 0.044715 * (x * x * x))))


def _rms_mod(x, g, shift, scale):
    ms = jnp.mean(x * x, axis=-1, keepdims=True)
    y = x * lax.rsqrt(ms + EPS) * g
    return y * (1.0 + scale) + shift


def _layer_norm(x, g, b):
    mu = jnp.mean(x, axis=-1, keepdims=True)
    xc = x - mu
    var = jnp.mean(xc * xc, axis=-1, keepdims=True)
    return xc * lax.rsqrt(var + EPS) * g + b


def _mod_kernel(c_ref, w_ref, b_ref, o_ref):
    s = _silu(c_ref[...]).astype(BF16)
    o_ref[...] = jnp.dot(s, w_ref[...].astype(BF16), preferred_element_type=F32) + b_ref[...]


def _modulation(cs, w_mod, b_mod):
    depth, _, n = w_mod.shape
    tn = 1536
    return pl.pallas_call(
        _mod_kernel,
        out_shape=jax.ShapeDtypeStruct((depth, MOD_ROWS, n), F32),
        grid=(depth, n // tn),
        in_specs=[
            pl.BlockSpec((MOD_ROWS, D_MODEL), lambda l, j: (0, 0)),
            pl.BlockSpec((None, D_MODEL, tn), lambda l, j: (l, 0, j)),
            pl.BlockSpec((None, 1, tn), lambda l, j: (l, 0, j)),
        ],
        out_specs=pl.BlockSpec((None, MOD_ROWS, tn), lambda l, j: (l, 0, j)),
        compiler_params=_params("arbitrary", "arbitrary"),
        name="modulation",
    )(cs, w_mod, b_mod.reshape(depth, 1, n))


def _pos_table_kernel(o_ref):
    quarter = D_MODEL // 4
    pos = lax.broadcasted_iota(jnp.int32, (GRID_W, quarter), 0).astype(F32)
    j = lax.broadcasted_iota(jnp.int32, (GRID_W, quarter), 1).astype(F32)
    omega = 1.0 / jnp.exp(j * (math.log(POS_BASE) / quarter))
    ang = pos * omega
    o_ref[:, :quarter] = jnp.sin(ang)
    o_ref[:, quarter:] = jnp.cos(ang)


def _pos_table():
    return pl.pallas_call(
        _pos_table_kernel,
        out_shape=jax.ShapeDtypeStruct((GRID_W, D_MODEL // 2), F32),
        name="pos_table",
    )()


def _pos_rows(e_ref, tile, ts):
    reps = ts // GRID_W
    col = jnp.concatenate([e_ref[...]] * reps, axis=0)
    r0 = tile * reps
    rows = [jnp.broadcast_to(e_ref[pl.ds(r0 + k, 1), :], (GRID_W, D_MODEL // 2)) for k in range(reps)]
    return jnp.concatenate([jnp.concatenate(rows, axis=0), col], axis=1)


def _mod_spec(layer, k, ctx_row):
    if ctx_row is None:
        return pl.BlockSpec((None, None, None, 1, D_MODEL), lambda b, i: (layer, b, k, 0, 0))
    return pl.BlockSpec((None, None, None, 1, D_MODEL), lambda b, i: (layer, ctx_row, k, 0, 0))


def _layer_spec(shape, layer, single_buffer=False):
    zeros = (0,) * len(shape)
    kwargs = dict(pipeline_mode=pl.Buffered(1)) if single_buffer else {}
    return pl.BlockSpec((None,) + tuple(shape), lambda b, i: (layer,) + zeros, **kwargs)


def _pos_spec():
    return pl.BlockSpec((GRID_W, D_MODEL // 2), lambda b, i: (0, 0))


def _scan_tile(a_ref, b_ref, out_ref, carry0, n_rows, reverse, unroll=False):
    width = a_ref.shape[-1]
    n_groups = n_rows // SUBLANES
    row = lax.broadcasted_iota(jnp.int32, (SUBLANES, width), 0)

    def group(gi, carry):
        g = (n_groups - 1 - gi) if reverse else gi
        start = g * SUBLANES if unroll else pl.multiple_of(g * SUBLANES, SUBLANES)
        a = a_ref[pl.ds(start, SUBLANES), :]
        h = b_ref[pl.ds(start, SUBLANES), :]
        for s in (1, 2, 4):
            if reverse:
                keep = row < SUBLANES - s
                shift = SUBLANES - s
            else:
                keep = row >= s
                shift = s
            h_sh = jnp.where(keep, pltpu.roll(h, shift, 0), 0.0)
            a_sh = jnp.where(keep, pltpu.roll(a, shift, 0), 1.0)
            h = h + a * h_sh
            a = a * a_sh
        h = h + a * carry
        out_ref[pl.ds(start, SUBLANES), :] = h
        edge = h[0:1, :] if reverse else h[SUBLANES - 1:SUBLANES, :]
        return jnp.broadcast_to(edge, (SUBLANES, width))

    carry = jnp.broadcast_to(carry0, (SUBLANES, width))
    if unroll:
        for gi in range(n_groups):
            carry = group(gi, carry)
        return carry[0:1, :]
    return lax.fori_loop(0, n_groups, group, carry, unroll=2)[0:1, :]


def _proj_lru_kernel(x_ref, xn_ref, e_ref, g_ref, sh_ref, sc_ref, w_ref,
                     cw_ref, cb_ref, wg_ref, bg_ref, lam_ref, h0_ref,
                     px_ref, lru_ref, hlast_ref,
                     ext_scr, tail_scr, a_scr, b_scr, carry_scr, *, ts, add_pos):
    i = pl.program_id(1)
    n_t = pl.num_programs(1)
    reps = ts // GRID_W

    @pl.when(i == 0)
    def _():
        carry_scr[...] = h0_ref[...]
        tail_scr[...] = jnp.zeros(tail_scr.shape, F32)

    ext_scr[0:HALO, :] = tail_scr[...]

    x = x_ref[...]
    if add_pos:
        x = x + _pos_rows(e_ref, i, ts)
    rows = ts // INPROJ_CHUNKS
    for c in range(INPROJ_CHUNKS):
        sl = slice(c * rows, (c + 1) * rows)
        h = _rms_mod(x[sl, :], g_ref[...], sh_ref[...], sc_ref[...])
        res = jnp.dot(h.astype(BF16), w_ref[...], preferred_element_type=F32)
        px_ref[sl, :] = res[:, W_LRU:].astype(px_ref.dtype)
        ext_scr[HALO + c * rows:HALO + (c + 1) * rows, :] = res[:, :W_LRU]

    xn = xn_ref[...]
    if add_pos:
        half = D_MODEL // 2
        pos_n = jnp.concatenate(
            [jnp.broadcast_to(e_ref[pl.ds(jnp.minimum((i + 1) * reps, GRID_W - 1), 1), :],
                              (SUBLANES, half)),
             e_ref[0:SUBLANES, :]], axis=1)
        xn = xn + pos_n
    hn = _rms_mod(xn, g_ref[...], sh_ref[...], sc_ref[...])
    pn = jnp.dot(hn.astype(BF16), w_ref[:, :W_LRU], preferred_element_type=F32)
    ext_scr[HALO + ts:HALO + ts + SUBLANES, :] = jnp.where(i < n_t - 1, pn, 0.0)
    tail_scr[...] = ext_scr[ts:ts + HALO, :]

    cx = jnp.broadcast_to(cb_ref[...], (ts, W_LRU))
    for k in range(LRU_CONV_W):
        off = HALO + k - LRU_PAD_L
        cx = cx + cw_ref[k:k + 1, :] * ext_scr[off:off + ts, :]

    gates = jnp.dot(cx.astype(BF16), wg_ref[...], preferred_element_type=F32) + bg_ref[...]
    neg_lam = -lam_ref[...]
    sp = jnp.maximum(neg_lam, 0.0) + jnp.log1p(jnp.exp(-jnp.abs(neg_lam)))
    lam_c = -LRU_C * sp
    for d in range(2):
        r = _sigmoid(gates[:, (2 * d) * W_LRU:(2 * d + 1) * W_LRU])
        ig = _sigmoid(gates[:, (2 * d + 1) * W_LRU:(2 * d + 2) * W_LRU])
        log_a = lam_c[:, d * W_LRU:(d + 1) * W_LRU] * r
        a = jnp.exp(log_a)
        t = jnp.tanh(log_a)
        z = -2.0 * t / (1.0 - t)
        bmul = jnp.where(z > 0.0, z * lax.rsqrt(z), 0.0)
        bb = bmul * (ig * cx)
        if d == 0:
            a_scr[...] = a
            b_scr[...] = bb
        else:
            lru_ref[:, W_LRU:2 * W_LRU] = a
            lru_ref[:, 2 * W_LRU:] = bb

    carry = _scan_tile(a_scr, b_scr, lru_ref.at[:, :W_LRU], carry_scr[...], ts, reverse=False,
                       unroll=True)
    carry_scr[...] = carry
    hlast_ref[...] = carry


def _proj_lru(x, pos_e, mod5, g1, w_in, conv_w, conv_b, wg, bg, lam, h0, layer, ctx_row, ts,
              add_pos):
    b, s, d = x.shape
    n8 = ts // SUBLANES
    last8 = s // SUBLANES - 1
    tile = lambda bb, i: (bb, i, 0)
    return pl.pallas_call(
        functools.partial(_proj_lru_kernel, ts=ts, add_pos=add_pos),
        out_shape=(
            jax.ShapeDtypeStruct((b, s, D_MIX_IN), BF16),
            jax.ShapeDtypeStruct((b, s, 3 * W_LRU), F32),
            jax.ShapeDtypeStruct((b, 1, W_LRU), F32),
        ),
        grid=(b, s // ts),
        in_specs=[
            pl.BlockSpec((None, ts, d), tile),
            pl.BlockSpec((None, SUBLANES, d), lambda bb, i: (bb, jnp.minimum((i + 1) * n8, last8), 0)),
            _pos_spec(),
            _layer_spec((1, d), layer),
            _mod_spec(layer, 0, ctx_row),
            _mod_spec(layer, 1, ctx_row),
            _layer_spec((d, D_IN), layer),
            _layer_spec((LRU_CONV_W, W_LRU), layer),
            _layer_spec((1, W_LRU), layer),
            _layer_spec((W_LRU, 4 * W_LRU), layer),
            _layer_spec((1, 4 * W_LRU), layer),
            _layer_spec((1, 2 * W_LRU), layer),
            pl.BlockSpec((None, 1, W_LRU), lambda bb, i: (bb, 0, 0)),
        ],
        out_specs=(
            pl.BlockSpec((None, ts, D_MIX_IN), tile),
            pl.BlockSpec((None, ts, 3 * W_LRU), tile),
            pl.BlockSpec((None, 1, W_LRU), lambda bb, i: (bb, 0, 0)),
        ),
        scratch_shapes=[
            pltpu.VMEM((ts + 2 * HALO, W_LRU), F32),
            pltpu.VMEM((HALO, W_LRU), F32),
            pltpu.VMEM((ts, W_LRU), F32),
            pltpu.VMEM((ts, W_LRU), F32),
            pltpu.VMEM((1, W_LRU), F32),
        ],
        compiler_params=_params("arbitrary", "arbitrary"),
        name="proj_lru",
    )(x, x, pos_e, g1, mod5, mod5, w_in, conv_w, conv_b, wg, bg, lam, h0)


def _pool_windows(shape):
    grp = lax.broadcasted_iota(jnp.int32, shape, 1) // POOL_GROUP
    win = jnp.full(shape, POOL_WINDOWS[-1], jnp.int32)
    for g in range(len(POOL_WINDOWS) - 2, -1, -1):
        win = jnp.where(grp == g, POOL_WINDOWS[g], win)
    return win


def _pool_icnt_kernel(o_ref, *, ts, seq):
    win = _pool_windows((ts, W_POOL))
    lo = win // 2
    hi = win - lo - 1
    tpos = pl.program_id(0) * ts + lax.broadcasted_iota(jnp.int32, (ts, W_POOL), 0)
    cnt = (jnp.minimum(tpos + hi + 1, seq) - jnp.maximum(tpos - lo, 0)).astype(F32)
    o_ref[...] = 1.0 / cnt


def _pool_icnt(seq, ts):
    return pl.pallas_call(
        functools.partial(_pool_icnt_kernel, ts=ts, seq=seq),
        out_shape=jax.ShapeDtypeStruct((seq, W_POOL), F32),
        grid=(seq // ts,),
        out_specs=pl.BlockSpec((ts, W_POOL), lambda i: (i, 0)),
        compiler_params=_params("arbitrary"),
        name="pool_icnt",
    )()


def _pool_bands():
    t = jnp.arange(POOL_ROWS)[:, None]
    tok = jnp.arange(POOL_ROWS + 2 * HALO)[None, :] - HALO
    bands = []
    for w in POOL_WINDOWS:
        lo = w // 2
        hi = w - lo - 1
        bands.append((tok >= t - lo) & (tok <= t + hi))
    return jnp.stack(bands).astype(BF16)


def _mix_kernel(x_ref, e_ref, px_ref, prev_ref, next_ref, lru_ref, h0_ref, gt_ref,
                icnt_ref, pb_ref, pw_ref, ps_ref, slg_ref, slb_ref, sw_ref, sb_ref,
                dw_ref, db_ref, dlg_ref, dlb_ref, wo_ref,
                o_ref, hfirst_ref,
                hb_scr, carry_scr, zext_scr, yext_scr, shift_scr, ycat_scr, *, ts, add_pos):
    i = pl.program_id(1)
    n_t = pl.num_programs(1)
    tile = n_t - 1 - i
    c_gate, c_pool, c_u, c_v, c_cv, c_cg = (W_LRU * k for k in range(6))
    hf_ref = lru_ref.at[:, :W_LRU]
    ab_ref = lru_ref.at[:, W_LRU:]
    has_prev = tile > 0
    has_next = tile < n_t - 1

    @pl.when(i == 0)
    def _():
        carry_scr[...] = h0_ref[...]

    carry = _scan_tile(ab_ref.at[:, :W_LRU], ab_ref.at[:, W_LRU:], hb_scr, carry_scr[...], ts,
                       reverse=True, unroll=True)
    carry_scr[...] = carry
    hfirst_ref[...] = carry
    y_a = (hf_ref[...] + hb_scr[...]) * _gelu_tanh(px_ref[:, c_gate:c_gate + W_LRU].astype(F32))
    ycat_scr[:, :W_LRU] = y_a.astype(BF16)

    zext_scr[0:HALO, :] = jnp.where(has_prev, prev_ref[:, c_pool:c_pool + W_POOL], 0.0)
    zext_scr[HALO:HALO + ts, :] = px_ref[:, c_pool:c_pool + W_POOL]
    zext_scr[HALO + ts:, :] = jnp.where(has_next, next_ref[:, c_pool:c_pool + W_POOL], 0.0)
    grp = lax.broadcasted_iota(jnp.int32, (POOL_ROWS, W_POOL), 1) // POOL_GROUP
    for blk in range(ts // POOL_ROWS):
        r0 = blk * POOL_ROWS
        zb = zext_scr[r0:r0 + POOL_ROWS + 2 * HALO, :]
        wsum = jnp.dot(pb_ref[len(POOL_WINDOWS) - 1], zb, preferred_element_type=F32)
        for g in range(len(POOL_WINDOWS) - 2, -1, -1):
            wsum = jnp.where(grp == g, jnp.dot(pb_ref[g], zb, preferred_element_type=F32), wsum)
        rows = slice(r0, r0 + POOL_ROWS)
        pooled = wsum * icnt_ref[rows, :] - px_ref[rows, c_pool:c_pool + W_POOL].astype(F32)
        y_b = jnp.dot(pooled.astype(BF16), pw_ref[...], preferred_element_type=F32) * ps_ref[...]
        ycat_scr[rows, W_LRU:W_LRU + W_POOL] = y_b.astype(BF16)

    vn = _layer_norm(px_ref[:, c_v:c_v + W_SGU].astype(F32), slg_ref[...], slb_ref[...]).astype(BF16)
    head = lax.broadcasted_iota(jnp.int32, (SGU_CHUNK, W_SGU), 1) // SGU_HEAD_DIM
    zero = jnp.zeros((SGU_CHUNK, W_SGU), BF16)
    for c in range(ts // SGU_CHUNK):
        rows = slice(c * SGU_CHUNK, (c + 1) * SGU_CHUNK)
        vh = jnp.concatenate([jnp.where(head == h, vn[rows, :], zero) for h in range(SGU_HEADS)],
                             axis=0)
        zc = jnp.dot(sw_ref[...], vh, preferred_element_type=F32) + sb_ref[...]
        y_c = px_ref[rows, c_u:c_u + W_SGU].astype(F32) * zc
        ycat_scr[rows, W_LRU + W_POOL:W_LRU + W_POOL + W_SGU] = y_c.astype(BF16)

    def gated(ref):
        return (ref[:, c_cv:c_cv + W_CONV].astype(F32)
                * _sigmoid(ref[:, c_cg:c_cg + W_CONV].astype(F32)))

    yext_scr[0:HALO, :] = jnp.where(has_prev, gated(prev_ref), 0.0)
    yext_scr[HALO:HALO + ts, :] = gated(px_ref)
    yext_scr[HALO + ts:, :] = jnp.where(has_next, gated(next_ref), 0.0)
    n_shift = ts + 2 * HALO - SUBLANES
    for r in range(1, SUBLANES):
        shift_scr[r - 1] = yext_scr[r:r + n_shift, :]
    for blk in range(ts // CONV_ROWS):
        r0 = blk * CONV_ROWS
        acc = jnp.broadcast_to(db_ref[...], (CONV_ROWS, W_CONV))
        for k in range(CONF_CONV_W):
            off = HALO + k - CONF_PAD
            src = r0 + (off // SUBLANES) * SUBLANES
            if off % SUBLANES:
                tap = shift_scr[off % SUBLANES - 1, src:src + CONV_ROWS, :]
            else:
                tap = yext_scr[src:src + CONV_ROWS, :]
            acc = acc + dw_ref[k:k + 1, :] * tap
        y_d = _silu(_layer_norm(acc, dlg_ref[...], dlb_ref[...]))
        ycat_scr[r0:r0 + CONV_ROWS, W_LRU + W_POOL + W_SGU:] = y_d.astype(BF16)

    x = x_ref[...]
    if add_pos:
        x = x + _pos_rows(e_ref, tile, ts)
    o_ref[...] = x + gt_ref[...] * jnp.dot(ycat_scr[...], wo_ref[...], preferred_element_type=F32)


def _mix(x, pos_e, px, lru, h0b, mod5, icnt, pool_band, pool_bd, pool_scale, sgu_ln_g, sgu_ln_b,
         sgu_ws, sgu_bias, conv_d_w, conv_d_b, conv_ln_g, conv_ln_b, w_out, layer, ctx_row, ts,
         add_pos):
    b, s, d = x.shape
    n_t = s // ts
    nh = ts // HALO
    lasth = s // HALO - 1
    kernel = functools.partial(_mix_kernel, ts=ts, add_pos=add_pos)

    def rev(bb, i):
        return (bb, n_t - 1 - i, 0)

    def icnt_map(bb, i):
        tile = n_t - 1 - i
        return (jnp.where(tile == 0, 0, jnp.where(tile == n_t - 1, 2, 1)), 0)

    return pl.pallas_call(
        kernel,
        out_shape=(
            jax.ShapeDtypeStruct((b, s, d), F32),
            jax.ShapeDtypeStruct((b, 1, W_LRU), F32),
        ),
        grid=(b, n_t),
        in_specs=[
            pl.BlockSpec((None, ts, d), rev),
            _pos_spec(),
            pl.BlockSpec((None, ts, D_MIX_IN), rev),
            pl.BlockSpec((None, HALO, D_MIX_IN),
                         lambda bb, i: (bb, jnp.maximum((n_t - 1 - i) * nh - 1, 0), 0)),
            pl.BlockSpec((None, HALO, D_MIX_IN),
                         lambda bb, i: (bb, jnp.minimum((n_t - i) * nh, lasth), 0)),
            pl.BlockSpec((None, ts, 3 * W_LRU), rev),
            pl.BlockSpec((None, 1, W_LRU), lambda bb, i: (bb, 0, 0)),
            _mod_spec(layer, 2, ctx_row),
            pl.BlockSpec((ts, W_POOL), icnt_map),
            pl.BlockSpec(pool_band.shape, lambda bb, i: (0, 0, 0)),
            _layer_spec((W_POOL, W_POOL), layer),
            _layer_spec((1, W_POOL), layer),
            _layer_spec((1, W_SGU), layer),
            _layer_spec((1, W_SGU), layer),
            _layer_spec((SGU_CHUNK, SGU_HEADS * SGU_CHUNK), layer),
            _layer_spec((SGU_CHUNK, W_SGU), layer),
            _layer_spec((CONF_CONV_W, W_CONV), layer),
            _layer_spec((1, W_CONV), layer),
            _layer_spec((1, W_CONV), layer),
            _layer_spec((1, W_CONV), layer),
            _layer_spec((d, d), layer),
        ],
        out_specs=(
            pl.BlockSpec((None, ts, d), rev),
            pl.BlockSpec((None, 1, W_LRU), lambda bb, i: (bb, 0, 0)),
        ),
        scratch_shapes=[
            pltpu.VMEM((ts, W_LRU), F32),
            pltpu.VMEM((1, W_LRU), F32),
            pltpu.VMEM((ts + 2 * HALO, W_POOL), BF16),
            pltpu.VMEM((ts + 2 * HALO, W_CONV), F32),
            pltpu.VMEM((SUBLANES - 1, ts + 2 * HALO - SUBLANES, W_CONV), F32),
            pltpu.VMEM((ts, d), BF16),
        ],
        compiler_params=_params("arbitrary", "arbitrary"),
        name="mix",
    )(x, pos_e, px, px, px, lru, h0b, mod5, icnt, pool_band, pool_bd, pool_scale, sgu_ln_g,
      sgu_ln_b, sgu_ws, sgu_bias, conv_d_w, conv_d_b, conv_ln_g, conv_ln_b, w_out)


def _bwd_state_kernel(lru_ref, o_ref, hb_scr, *, rows):
    zero = jnp.zeros((1, W_LRU), F32)
    o_ref[...] = _scan_tile(lru_ref.at[:, W_LRU:2 * W_LRU], lru_ref.at[:, 2 * W_LRU:], hb_scr, zero,
                            rows, reverse=True)


def _bwd_state(lru):
    b, s, _ = lru.shape
    return pl.pallas_call(
        functools.partial(_bwd_state_kernel, rows=s),
        out_shape=jax.ShapeDtypeStruct((b, 1, W_LRU), F32),
        grid=(b,),
        in_specs=[pl.BlockSpec((None, s, 3 * W_LRU), lambda bb: (bb, 0, 0))],
        out_specs=pl.BlockSpec((None, 1, W_LRU), lambda bb: (bb, 0, 0)),
        scratch_shapes=[pltpu.VMEM((s, W_LRU), F32)],
        compiler_params=_params("arbitrary"),
        name="bwd_state",
    )(lru)


def _mlp_kernel(x_ref, g_ref, sh_ref, sc_ref, gt_ref, w1_ref, w2_ref, gf_ref, o_ref, acc_scr,
                *, final):
    x = x_ref[...]
    h = _rms_mod(x, g_ref[...], sh_ref[...], sc_ref[...]).astype(BF16)
    for j in range(D_FF // FF_CHUNK):
        cols = slice(j * FF_CHUNK, (j + 1) * FF_CHUNK)
        u = jnp.dot(h, w1_ref[:, cols], preferred_element_type=F32)
        u = jnp.square(jnp.maximum(u, 0.0)).astype(BF16)
        part = jnp.dot(u, w2_ref[cols, :], preferred_element_type=F32)
        if j == 0:
            acc_scr[...] = part
        else:
            acc_scr[...] += part
    y = x + gt_ref[...] * acc_scr[...]
    if final:
        ms = jnp.mean(y * y, axis=-1, keepdims=True)
        y = y * lax.rsqrt(ms + EPS) * gf_ref[...]
    o_ref[...] = y


def _mlp(x, mod5, g2, w1, w2, g_final, layer, ctx_row, tm, final):
    b, s, d = x.shape
    return pl.pallas_call(
        functools.partial(_mlp_kernel, final=final),
        out_shape=jax.ShapeDtypeStruct(x.shape, F32),
        grid=(b, s // tm),
        in_specs=[
            pl.BlockSpec((None, tm, d), lambda bb, i: (bb, i, 0)),
            _layer_spec((1, d), layer),
            _mod_spec(layer, 3, ctx_row),
            _mod_spec(layer, 4, ctx_row),
            _mod_spec(layer, 5, ctx_row),
            _layer_spec((d, D_FF), layer, single_buffer=True),
            _layer_spec((D_FF, d), layer, single_buffer=True),
            pl.BlockSpec((1, d), lambda bb, i: (0, 0)),
        ],
        out_specs=pl.BlockSpec((None, tm, d), lambda bb, i: (bb, i, 0)),
        scratch_shapes=[pltpu.VMEM((tm, d), F32)],
        compiler_params=_params("arbitrary", "arbitrary"),
        name="mlp",
    )(x, g2, mod5, mod5, mod5, w1, w2, g_final)


def _block_diag(w):
    *lead, h, n, _ = w.shape
    eye = jnp.eye(h, dtype=w.dtype)
    out = jnp.einsum('...hij,hg->...higj', w, eye)
    return out.reshape(*lead, h * n, h * n)


def _edge_tiles(table, ts):
    n_t = table.shape[0] // ts
    mid = min(1, n_t - 1)
    return jnp.concatenate([table[:ts], table[mid * ts:(mid + 1) * ts], table[-ts:]], axis=0)


def _tile_rows(s, want):
    t = min(s, want)
    while s % t:
        t //= 2
    return t


def kernel(x, c, ctx, c_ctx, w_mod, b_mod, g_norm1, g_norm2, w_in, conv_a_w, conv_a_b, lru_wa, lru_ba, lru_wx, lru_bx, lru_lambda, pool_w, pool_scale, sgu_ln_g, sgu_ln_b, sgu_w, sgu_b, conv_d_w, conv_d_b, conv_ln_g, conv_ln_b, w_out, w_mlp1, w_mlp2, g_final):
    depth = w_in.shape[0]
    bsz, seq, d = x.shape
    ctx_len = ctx.shape[1]
    assert d == D_MODEL and bsz < MOD_ROWS
    assert seq % SGU_CHUNK == 0 and ctx_len % SGU_CHUNK == 0
    ctx_row = bsz

    w_in_b = w_in.astype(BF16)
    w_out_b = w_out.astype(BF16)
    w1_b = w_mlp1.astype(BF16)
    w2_b = w_mlp2.astype(BF16)
    wa_bd = _block_diag(lru_wa)
    wx_bd = _block_diag(lru_wx)
    wg = jnp.concatenate([wa_bd[:, 0], wx_bd[:, 0], wa_bd[:, 1], wx_bd[:, 1]], axis=-1).astype(BF16)
    bg = jnp.concatenate([lru_ba[:, 0], lru_bx[:, 0], lru_ba[:, 1], lru_bx[:, 1]], axis=-1)[:, None, :]
    lam = lru_lambda.reshape(depth, 1, 2 * W_LRU)
    pool_bd = _block_diag(pool_w).astype(BF16)
    sgu_ws = jnp.swapaxes(sgu_w, 1, 2).reshape(depth, SGU_CHUNK, SGU_HEADS * SGU_CHUNK).astype(BF16)
    sgu_bias = jnp.repeat(jnp.swapaxes(sgu_b, 1, 2), SGU_HEAD_DIM, axis=2)
    row = lambda v: v[:, None, :]

    cs = jnp.zeros((MOD_ROWS, d), F32).at[:bsz].set(c).at[ctx_row].set(c_ctx)
    mod = _modulation(cs, w_mod, b_mod)
    mod5 = mod.reshape(depth, MOD_ROWS, N_MOD, 1, d)

    ts = _tile_rows(seq, 1024)
    ts_c = _tile_rows(ctx_len, 512)
    tm = _tile_rows(seq, 1024)
    tm_c = _tile_rows(bsz * ctx_len, 1024)
    assert ts % GRID_W == 0 and ts % CONV_ROWS == 0 and ts_c % CONV_ROWS == 0
    pos_e = _pos_table()
    xc = ctx
    zeros_h = jnp.zeros((bsz, 1, W_LRU), F32)
    g_fin = g_final.reshape(1, d)
    mix_w = (_pool_bands(), pool_bd, row(pool_scale), row(sgu_ln_g), row(sgu_ln_b), sgu_ws,
             sgu_bias, conv_d_w, row(conv_d_b), row(conv_ln_g), row(conv_ln_b), w_out_b)
    icnt = _edge_tiles(_pool_icnt(seq, ts), ts)
    icnt_c = _edge_tiles(_pool_icnt(ctx_len, ts_c), ts_c)

    for l in range(depth):
        last = l == depth - 1
        first = l == 0
        pc, lru_c, hcf_last = _proj_lru(xc, pos_e, mod5, row(g_norm1), w_in_b, conv_a_w,
                                        row(conv_a_b), wg, bg, lam, zeros_h, l, ctx_row, ts_c, False)
        if last:
            hcb_first = _bwd_state(lru_c)
        else:
            xc, hcb_first = _mix(xc, pos_e, pc, lru_c, zeros_h, mod5, icnt_c, *mix_w, l, ctx_row,
                                 ts_c, False)
            xc = _mlp(xc.reshape(1, -1, d), mod5, row(g_norm2), w1_b, w2_b, g_fin, l, ctx_row, tm_c,
                      False).reshape(bsz, ctx_len, d)
        px, lru, _ = _proj_lru(x, pos_e, mod5, row(g_norm1), w_in_b, conv_a_w, row(conv_a_b),
                               wg, bg, lam, hcf_last, l, None, ts, first)
        x, _ = _mix(x, pos_e, px, lru, hcb_first, mod5, icnt, *mix_w, l, None, ts, first)
        x = _mlp(x, mod5, row(g_norm2), w1_b, w2_b, g_fin, l, None, tm, last)
    return x
```

```python
import functools
import math

import jax
import jax.numpy as jnp
from jax import lax
from jax.experimental import pallas as pl
from jax.experimental.pallas import tpu as pltpu

F32 = jnp.float32
BF16 = jnp.bfloat16

D_MODEL = 1024
GRID_W = 64
W_LRU = 256
W_POOL = 256
W_SGU = 256
W_CONV = 256
LRU_HEADS = 4
LRU_CONV_W = 4
LRU_PAD_L = 2
LRU_C = 8.0
POOL_WINDOWS = (2, 4, 8, 16)
POOL_GROUP = W_POOL // len(POOL_WINDOWS)
SGU_CHUNK = 128
SGU_HEADS = 4
SGU_HEAD_DIM = W_SGU // SGU_HEADS
CONF_CONV_W = 31
CONF_PAD = 15
D_FF = 4 * D_MODEL
D_IN = 2 * W_LRU + W_POOL + 2 * W_SGU + 2 * W_CONV
D_MIX_IN = D_IN - W_LRU
N_MOD = 6
EPS = 1e-6
POS_BASE = 10000.0

SUBLANES = 8
HALO = 16
VMEM_LIMIT = 56 * 1024 * 1024
MOD_ROWS = 8
FF_CHUNK = 512
CONV_ROWS = 256
POOL_ROWS = 128
INPROJ_CHUNKS = 2
CAST_CHUNK = 256


def _params(*sem):
    return pltpu.CompilerParams(dimension_semantics=sem, vmem_limit_bytes=VMEM_LIMIT)


def _sigmoid(x):
    return 0.5 * jnp.tanh(0.5 * x) + 0.5


def _silu(x):
    return x * _sigmoid(x)


def _gelu_tanh(x):
    return 0.5 * x * (1.0 + jnp.tanh(0.7978845608028654 * (x + 0.044715 * (x * x * x))))


def _rms_mod(x, g, shift, scale):
    ms = jnp.mean(x * x, axis=-1, keepdims=True)
    y = x * lax.rsqrt(ms + EPS) * g
    return y * (1.0 + scale) + shift


def _layer_norm(x, g, b):
    mu = jnp.mean(x, axis=-1, keepdims=True)
    xc = x - mu
    var = jnp.mean(xc * xc, axis=-1, keepdims=True)
    return xc * lax.rsqrt(var + EPS) * g + b


def _mod_kernel(c_ref, w_ref, b_ref, o_ref):
    s = _silu(c_ref[...]).astype(BF16)
    o_ref[...] = jnp.dot(s, w_ref[...].astype(BF16), preferred_element_type=F32) + b_ref[...]


def _modulation(cs, w_mod, b_mod):
    depth, _, n = w_mod.shape
    tn = 1536
    return pl.pallas_call(
        _mod_kernel,
        out_shape=jax.ShapeDtypeStruct((depth, MOD_ROWS, n), F32),
        grid=(depth, n // tn),
        in_specs=[
            pl.BlockSpec((MOD_ROWS, D_MODEL), lambda l, j: (0, 0)),
            pl.BlockSpec((None, D_MODEL, tn), lambda l, j: (l, 0, j)),
            pl.BlockSpec((None, 1, tn), lambda l, j: (l, 0, j)),
        ],
        out_specs=pl.BlockSpec((None, MOD_ROWS, tn), lambda l, j: (l, 0, j)),
        compiler_params=_params("arbitrary", "arbitrary"),
        name="modulation",
    )(cs, w_mod, b_mod.reshape(depth, 1, n))


def _pos_table_kernel(o_ref):
    quarter = D_MODEL // 4
    pos = lax.broadcasted_iota(jnp.int32, (GRID_W, quarter), 0).astype(F32)
    j = lax.broadcasted_iota(jnp.int32, (GRID_W, quarter), 1).astype(F32)
    omega = 1.0 / jnp.exp(j * (math.log(POS_BASE) / quarter))
    ang = pos * omega
    o_ref[:, :quarter] = jnp.sin(ang)
    o_ref[:, quarter:] = jnp.cos(ang)


def _pos_table():
    return pl.pallas_call(
        _pos_table_kernel,
        out_shape=jax.ShapeDtypeStruct((GRID_W, D_MODEL // 2), F32),
        name="pos_table",
    )()


def _pos_rows(e_ref, tile, ts):
    reps = ts // GRID_W
    col = jnp.concatenate([e_ref[...]] * reps, axis=0)
    r0 = tile * reps
    rows = [jnp.broadcast_to(e_ref[pl.ds(r0 + k, 1), :], (GRID_W, D_MODEL // 2)) for k in range(reps)]
    return jnp.concatenate([jnp.concatenate(rows, axis=0), col], axis=1)


def _mod_spec(layer, k, ctx_row):
    if ctx_row is None:
        return pl.BlockSpec((None, None, None, 1, D_MODEL), lambda b, i: (layer, b, k, 0, 0))
    return pl.BlockSpec((None, None, None, 1, D_MODEL), lambda b, i: (layer, ctx_row, k, 0, 0))


def _layer_spec(shape, layer, single_buffer=False):
    zeros = (0,) * len(shape)
    kwargs = dict(pipeline_mode=pl.Buffered(1)) if single_buffer else {}
    return pl.BlockSpec((None,) + tuple(shape), lambda b, i: (layer,) + zeros, **kwargs)


def _pos_spec():
    return pl.BlockSpec((GRID_W, D_MODEL // 2), lambda b, i: (0, 0))


def _scan_tile(a_ref, b_ref, out_ref, carry0, n_rows, reverse, unroll=False):
    width = a_ref.shape[-1]
    n_groups = n_rows // SUBLANES
    row = lax.broadcasted_iota(jnp.int32, (SUBLANES, width), 0)

    def group(gi, carry):
        g = (n_groups - 1 - gi) if reverse else gi
        start = g * SUBLANES if unroll else pl.multiple_of(g * SUBLANES, SUBLANES)
        a = a_ref[pl.ds(start, SUBLANES), :]
        h = b_ref[pl.ds(start, SUBLANES), :]
        for s in (1, 2, 4):
            if reverse:
                keep = row < SUBLANES - s
                shift = SUBLANES - s
            else:
                keep = row >= s
                shift = s
            h_sh = jnp.where(keep, pltpu.roll(h, shift, 0), 0.0)
            a_sh = jnp.where(keep, pltpu.roll(a, shift, 0), 1.0)
            h = h + a * h_sh
            a = a * a_sh
        h = h + a * carry
        out_ref[pl.ds(start, SUBLANES), :] = h
        edge = h[0:1, :] if reverse else h[SUBLANES - 1:SUBLANES, :]
        return jnp.broadcast_to(edge, (SUBLANES, width))

    carry = jnp.broadcast_to(carry0, (SUBLANES, width))
    if unroll:
        for gi in range(n_groups):
            carry = group(gi, carry)
        return carry[0:1, :]
    return lax.fori_loop(0, n_groups, group, carry, unroll=2)[0:1, :]


def _proj_lru_kernel(x_ref, xn_ref, e_ref, g_ref, sh_ref, sc_ref, w_ref,
                     cw_ref, cb_ref, wg_ref, bg_ref, lam_ref, h0_ref,
                     px_ref, lru_ref, hlast_ref,
                     ext_scr, tail_scr, a_scr, b_scr, carry_scr, *, ts, add_pos):
    i = pl.program_id(1)
    n_t = pl.num_programs(1)
    reps = ts // GRID_W

    @pl.when(i == 0)
    def _():
        carry_scr[...] = h0_ref[...]
        tail_scr[...] = jnp.zeros(tail_scr.shape, F32)

    ext_scr[0:HALO, :] = tail_scr[...]

    x = x_ref[...]
    if add_pos:
        x = x + _pos_rows(e_ref, i, ts)
    rows = ts // INPROJ_CHUNKS
    for c in range(INPROJ_CHUNKS):
        sl = slice(c * rows, (c + 1) * rows)
        h = _rms_mod(x[sl, :], g_ref[...], sh_ref[...], sc_ref[...])
        res = jnp.dot(h.astype(BF16), w_ref[...], preferred_element_type=F32)
        px_ref[sl, :] = res[:, W_LRU:].astype(px_ref.dtype)
        ext_scr[HALO + c * rows:HALO + (c + 1) * rows, :] = res[:, :W_LRU]

    xn = xn_ref[...]
    if add_pos:
        half = D_MODEL // 2
        pos_n = jnp.concatenate(
            [jnp.broadcast_to(e_ref[pl.ds(jnp.minimum((i + 1) * reps, GRID_W - 1), 1), :],
                              (SUBLANES, half)),
             e_ref[0:SUBLANES, :]], axis=1)
        xn = xn + pos_n
    hn = _rms_mod(xn, g_ref[...], sh_ref[...], sc_ref[...])
    pn = jnp.dot(hn.astype(BF16), w_ref[:, :W_LRU], preferred_element_type=F32)
    ext_scr[HALO + ts:HALO + ts + SUBLANES, :] = jnp.where(i < n_t - 1, pn, 0.0)
    tail_scr[...] = ext_scr[ts:ts + HALO, :]

    cx = jnp.broadcast_to(cb_ref[...], (ts, W_LRU))
    for k in range(LRU_CONV_W):
        off = HALO + k - LRU_PAD_L
        cx = cx + cw_ref[k:k + 1, :] * ext_scr[off:off + ts, :]

    gates = jnp.dot(cx.astype(BF16), wg_ref[...], preferred_element_type=F32) + bg_ref[...]
    neg_lam = -lam_ref[...]
    sp = jnp.maximum(neg_lam, 0.0) + jnp.log1p(jnp.exp(-jnp.abs(neg_lam)))
    lam_c = -LRU_C * sp
    for d in range(2):
        r = _sigmoid(gates[:, (2 * d) * W_LRU:(2 * d + 1) * W_LRU])
        ig = _sigmoid(gates[:, (2 * d + 1) * W_LRU:(2 * d + 2) * W_LRU])
        log_a = lam_c[:, d * W_LRU:(d + 1) * W_LRU] * r
        a = jnp.exp(log_a)
        t = jnp.tanh(log_a)
        z = -2.0 * t / (1.0 - t)
        bmul = jnp.where(z > 0.0, z * lax.rsqrt(z), 0.0)
        bb = bmul * (ig * cx)
        if d == 0:
            a_scr[...] = a
            b_scr[...] = bb
        else:
            lru_ref[:, W_LRU:2 * W_LRU] = a
            lru_ref[:, 2 * W_LRU:] = bb

    carry = _scan_tile(a_scr, b_scr, lru_ref.at[:, :W_LRU], carry_scr[...], ts, reverse=False,
                       unroll=True)
    carry_scr[...] = carry
    hlast_ref[...] = carry


def _proj_lru(x, pos_e, mod5, g1, w_in, conv_w, conv_b, wg, bg, lam, h0, layer, ctx_row, ts,
              add_pos):
    b, s, d = x.shape
    n8 = ts // SUBLANES
    last8 = s // SUBLANES - 1
    tile = lambda bb, i: (bb, i, 0)
    return pl.pallas_call(
        functools.partial(_proj_lru_kernel, ts=ts, add_pos=add_pos),
        out_shape=(
            jax.ShapeDtypeStruct((b, s, D_MIX_IN), BF16),
            jax.ShapeDtypeStruct((b, s, 3 * W_LRU), F32),
            jax.ShapeDtypeStruct((b, 1, W_LRU), F32),
        ),
        grid=(b, s // ts),
        in_specs=[
            pl.BlockSpec((None, ts, d), tile),
            pl.BlockSpec((None, SUBLANES, d), lambda bb, i: (bb, jnp.minimum((i + 1) * n8, last8), 0)),
            _pos_spec(),
            _layer_spec((1, d), layer),
            _mod_spec(layer, 0, ctx_row),
            _mod_spec(layer, 1, ctx_row),
            _layer_spec((d, D_IN), 0),
            _layer_spec((LRU_CONV_W, W_LRU), layer),
            _layer_spec((1, W_LRU), layer),
            _layer_spec((W_LRU, 4 * W_LRU), layer),
            _layer_spec((1, 4 * W_LRU), layer),
            _layer_spec((1, 2 * W_LRU), layer),
            pl.BlockSpec((None, 1, W_LRU), lambda bb, i: (bb, 0, 0)),
        ],
        out_specs=(
            pl.BlockSpec((None, ts, D_MIX_IN), tile),
            pl.BlockSpec((None, ts, 3 * W_LRU), tile),
            pl.BlockSpec((None, 1, W_LRU), lambda bb, i: (bb, 0, 0)),
        ),
        scratch_shapes=[
            pltpu.VMEM((ts + 2 * HALO, W_LRU), F32),
            pltpu.VMEM((HALO, W_LRU), F32),
            pltpu.VMEM((ts, W_LRU), F32),
            pltpu.VMEM((ts, W_LRU), F32),
            pltpu.VMEM((1, W_LRU), F32),
        ],
        compiler_params=_params("arbitrary", "arbitrary"),
        name="proj_lru",
    )(x, x, pos_e, g1, mod5, mod5, w_in, conv_w, conv_b, wg, bg, lam, h0)


def _pool_windows(shape):
    grp = lax.broadcasted_iota(jnp.int32, shape, 1) // POOL_GROUP
    win = jnp.full(shape, POOL_WINDOWS[-1], jnp.int32)
    for g in range(len(POOL_WINDOWS) - 2, -1, -1):
        win = jnp.where(grp == g, POOL_WINDOWS[g], win)
    return win


def _pool_icnt_kernel(o_ref, *, ts, seq):
    win = _pool_windows((ts, W_POOL))
    lo = win // 2
    hi = win - lo - 1
    tpos = pl.program_id(0) * ts + lax.broadcasted_iota(jnp.int32, (ts, W_POOL), 0)
    cnt = (jnp.minimum(tpos + hi + 1, seq) - jnp.maximum(tpos - lo, 0)).astype(F32)
    o_ref[...] = 1.0 / cnt


def _pool_icnt(seq, ts):
    return pl.pallas_call(
        functools.partial(_pool_icnt_kernel, ts=ts, seq=seq),
        out_shape=jax.ShapeDtypeStruct((seq, W_POOL), F32),
        grid=(seq // ts,),
        out_specs=pl.BlockSpec((ts, W_POOL), lambda i: (i, 0)),
        compiler_params=_params("arbitrary"),
        name="pool_icnt",
    )()


def _pool_bands():
    t = jnp.arange(POOL_ROWS)[:, None]
    tok = jnp.arange(POOL_ROWS + 2 * HALO)[None, :] - HALO
    bands = []
    for w in POOL_WINDOWS:
        lo = w // 2
        hi = w - lo - 1
        bands.append((tok >= t - lo) & (tok <= t + hi))
    return jnp.stack(bands).astype(BF16)


def _mix_kernel(x_ref, e_ref, px_ref, prev_ref, next_ref, lru_ref, h0_ref, gt_ref,
                icnt_ref, pb_ref, pw_ref, ps_ref, slg_ref, slb_ref, sw_ref, sb_ref,
                dw_ref, db_ref, dlg_ref, dlb_ref, wo_ref,
                o_ref, hfirst_ref,
                hb_scr, carry_scr, zext_scr, yext_scr, shift_scr, ycat_scr, *, ts, add_pos):
    i = pl.program_id(1)
    n_t = pl.num_programs(1)
    tile = n_t - 1 - i
    c_gate, c_pool, c_u, c_v, c_cv, c_cg = (W_LRU * k for k in range(6))
    hf_ref = lru_ref.at[:, :W_LRU]
    ab_ref = lru_ref.at[:, W_LRU:]
    has_prev = tile > 0
    has_next = tile < n_t - 1

    @pl.when(i == 0)
    def _():
        carry_scr[...] = h0_ref[...]

    carry = _scan_tile(ab_ref.at[:, :W_LRU], ab_ref.at[:, W_LRU:], hb_scr, carry_scr[...], ts,
                       reverse=True, unroll=True)
    carry_scr[...] = carry
    hfirst_ref[...] = carry
    y_a = (hf_ref[...] + hb_scr[...]) * _gelu_tanh(px_ref[:, c_gate:c_gate + W_LRU].astype(F32))
    ycat_scr[:, :W_LRU] = y_a.astype(BF16)

    zext_scr[0:HALO, :] = jnp.where(has_prev, prev_ref[:, c_pool:c_pool + W_POOL], 0.0)
    zext_scr[HALO:HALO + ts, :] = px_ref[:, c_pool:c_pool + W_POOL]
    zext_scr[HALO + ts:, :] = jnp.where(has_next, next_ref[:, c_pool:c_pool + W_POOL], 0.0)
    grp = lax.broadcasted_iota(jnp.int32, (POOL_ROWS, W_POOL), 1) // POOL_GROUP
    for blk in range(ts // POOL_ROWS):
        r0 = blk * POOL_ROWS
        zb = zext_scr[r0:r0 + POOL_ROWS + 2 * HALO, :]
        wsum = jnp.dot(pb_ref[len(POOL_WINDOWS) - 1], zb, preferred_element_type=F32)
        for g in range(len(POOL_WINDOWS) - 2, -1, -1):
            wsum = jnp.where(grp == g, jnp.dot(pb_ref[g], zb, preferred_element_type=F32), wsum)
        rows = slice(r0, r0 + POOL_ROWS)
        pooled = wsum * icnt_ref[rows, :] - px_ref[rows, c_pool:c_pool + W_POOL].astype(F32)
        y_b = jnp.dot(pooled.astype(BF16), pw_ref[...], preferred_element_type=F32) * ps_ref[...]
        ycat_scr[rows, W_LRU:W_LRU + W_POOL] = y_b.astype(BF16)

    vn = _layer_norm(px_ref[:, c_v:c_v + W_SGU].astype(F32), slg_ref[...], slb_ref[...]).astype(BF16)
    head = lax.broadcasted_iota(jnp.int32, (SGU_CHUNK, W_SGU), 1) // SGU_HEAD_DIM
    zero = jnp.zeros((SGU_CHUNK, W_SGU), BF16)
    for c in range(ts // SGU_CHUNK):
        rows = slice(c * SGU_CHUNK, (c + 1) * SGU_CHUNK)
        vh = jnp.concatenate([jnp.where(head == h, vn[rows, :], zero) for h in range(SGU_HEADS)],
                             axis=0)
        zc = jnp.dot(sw_ref[...], vh, preferred_element_type=F32) + sb_ref[...]
        y_c = px_ref[rows, c_u:c_u + W_SGU].astype(F32) * zc
        ycat_scr[rows, W_LRU + W_POOL:W_LRU + W_POOL + W_SGU] = y_c.astype(BF16)

    def gated(ref):
        return (ref[:, c_cv:c_cv + W_CONV].astype(F32)
                * _sigmoid(ref[:, c_cg:c_cg + W_CONV].astype(F32)))

    yext_scr[0:HALO, :] = jnp.where(has_prev, gated(prev_ref), 0.0)
    yext_scr[HALO:HALO + ts, :] = gated(px_ref)
    yext_scr[HALO + ts:, :] = jnp.where(has_next, gated(next_ref), 0.0)
    n_shift = ts + 2 * HALO - SUBLANES
    for r in range(1, SUBLANES):
        shift_scr[r - 1] = yext_scr[r:r + n_shift, :]
    for blk in range(ts // CONV_ROWS):
        r0 = blk * CONV_ROWS
        acc = jnp.broadcast_to(db_ref[...], (CONV_ROWS, W_CONV))
        for k in range(CONF_CONV_W):
            off = HALO + k - CONF_PAD
            src = r0 + (off // SUBLANES) * SUBLANES
            if off % SUBLANES:
                tap = shift_scr[off % SUBLANES - 1, src:src + CONV_ROWS, :]
            else:
                tap = yext_scr[src:src + CONV_ROWS, :]
            acc = acc + dw_ref[k:k + 1, :] * tap
        y_d = _silu(_layer_norm(acc, dlg_ref[...], dlb_ref[...]))
        ycat_scr[r0:r0 + CONV_ROWS, W_LRU + W_POOL + W_SGU:] = y_d.astype(BF16)

    x = x_ref[...]
    if add_pos:
        x = x + _pos_rows(e_ref, tile, ts)
    o_ref[...] = x + gt_ref[...] * jnp.dot(ycat_scr[...], wo_ref[...], preferred_element_type=F32)


def _mix(x, pos_e, px, lru, h0b, mod5, icnt, pool_band, pool_bd, pool_scale, sgu_ln_g, sgu_ln_b,
         sgu_ws, sgu_bias, conv_d_w, conv_d_b, conv_ln_g, conv_ln_b, w_out, layer, ctx_row, ts,
         add_pos):
    b, s, d = x.shape
    n_t = s // ts
    nh = ts // HALO
    lasth = s // HALO - 1
    kernel = functools.partial(_mix_kernel, ts=ts, add_pos=add_pos)

    def rev(bb, i):
        return (bb, n_t - 1 - i, 0)

    def icnt_map(bb, i):
        tile = n_t - 1 - i
        return (jnp.where(tile == 0, 0, jnp.where(tile == n_t - 1, 2, 1)), 0)

    return pl.pallas_call(
        kernel,
        out_shape=(
            jax.ShapeDtypeStruct((b, s, d), F32),
            jax.ShapeDtypeStruct((b, 1, W_LRU), F32),
        ),
        grid=(b, n_t),
        in_specs=[
            pl.BlockSpec((None, ts, d), rev),
            _pos_spec(),
            pl.BlockSpec((None, ts, D_MIX_IN), rev),
            pl.BlockSpec((None, HALO, D_MIX_IN),
                         lambda bb, i: (bb, jnp.maximum((n_t - 1 - i) * nh - 1, 0), 0)),
            pl.BlockSpec((None, HALO, D_MIX_IN),
                         lambda bb, i: (bb, jnp.minimum((n_t - i) * nh, lasth), 0)),
            pl.BlockSpec((None, ts, 3 * W_LRU), rev),
            pl.BlockSpec((None, 1, W_LRU), lambda bb, i: (bb, 0, 0)),
            _mod_spec(layer, 2, ctx_row),
            pl.BlockSpec((ts, W_POOL), icnt_map),
            pl.BlockSpec(pool_band.shape, lambda bb, i: (0, 0, 0)),
            _layer_spec((W_POOL, W_POOL), layer),
            _layer_spec((1, W_POOL), layer),
            _layer_spec((1, W_SGU), layer),
            _layer_spec((1, W_SGU), layer),
            _layer_spec((SGU_CHUNK, SGU_HEADS * SGU_CHUNK), layer),
            _layer_spec((SGU_CHUNK, W_SGU), layer),
            _layer_spec((CONF_CONV_W, W_CONV), layer),
            _layer_spec((1, W_CONV), layer),
            _layer_spec((1, W_CONV), layer),
            _layer_spec((1, W_CONV), layer),
            _layer_spec((d, d), 0),
        ],
        out_specs=(
            pl.BlockSpec((None, ts, d), rev),
            pl.BlockSpec((None, 1, W_LRU), lambda bb, i: (bb, 0, 0)),
        ),
        scratch_shapes=[
            pltpu.VMEM((ts, W_LRU), F32),
            pltpu.VMEM((1, W_LRU), F32),
            pltpu.VMEM((ts + 2 * HALO, W_POOL), BF16),
            pltpu.VMEM((ts + 2 * HALO, W_CONV), F32),
            pltpu.VMEM((SUBLANES - 1, ts + 2 * HALO - SUBLANES, W_CONV), F32),
            pltpu.VMEM((ts, d), BF16),
        ],
        compiler_params=_params("arbitrary", "arbitrary"),
        name="mix",
    )(x, pos_e, px, px, px, lru, h0b, mod5, icnt, pool_band, pool_bd, pool_scale, sgu_ln_g,
      sgu_ln_b, sgu_ws, sgu_bias, conv_d_w, conv_d_b, conv_ln_g, conv_ln_b, w_out)


def _bwd_state_kernel(lru_ref, o_ref, hb_scr, *, rows):
    zero = jnp.zeros((1, W_LRU), F32)
    o_ref[...] = _scan_tile(lru_ref.at[:, W_LRU:2 * W_LRU], lru_ref.at[:, 2 * W_LRU:], hb_scr, zero,
                            rows, reverse=True)


def _bwd_state(lru):
    b, s, _ = lru.shape
    return pl.pallas_call(
        functools.partial(_bwd_state_kernel, rows=s),
        out_shape=jax.ShapeDtypeStruct((b, 1, W_LRU), F32),
        grid=(b,),
        in_specs=[pl.BlockSpec((None, s, 3 * W_LRU), lambda bb: (bb, 0, 0))],
        out_specs=pl.BlockSpec((None, 1, W_LRU), lambda bb: (bb, 0, 0)),
        scratch_shapes=[pltpu.VMEM((s, W_LRU), F32)],
        compiler_params=_params("arbitrary"),
        name="bwd_state",
    )(lru)


def _mlp_kernel(x_ref, g_ref, sh_ref, sc_ref, gt_ref, w1_ref, w2_ref, gf_ref, *rest, final, n_cast):
    cast_in, o_ref, cast_out, acc_scr = rest[:n_cast], rest[n_cast], rest[n_cast + 1:-1], rest[-1]
    for src, dst in zip(cast_in, cast_out):
        dst[...] = src[...].astype(BF16)
    x = x_ref[...]
    h = _rms_mod(x, g_ref[...], sh_ref[...], sc_ref[...]).astype(BF16)
    for j in range(D_FF // FF_CHUNK):
        cols = slice(j * FF_CHUNK, (j + 1) * FF_CHUNK)
        u = jnp.dot(h, w1_ref[:, cols], preferred_element_type=F32)
        u = jnp.square(jnp.maximum(u, 0.0)).astype(BF16)
        part = jnp.dot(u, w2_ref[cols, :], preferred_element_type=F32)
        if j == 0:
            acc_scr[...] = part
        else:
            acc_scr[...] += part
    y = x + gt_ref[...] * acc_scr[...]
    if final:
        ms = jnp.mean(y * y, axis=-1, keepdims=True)
        y = y * lax.rsqrt(ms + EPS) * gf_ref[...]
    o_ref[...] = y


def _mlp(x, mod5, g2, w1, w2, g_final, layer, ctx_row, tm, final, cast_next=()):
    b, s, d = x.shape
    n_i = s // tm
    n_steps = b * n_i
    cast_specs, cast_out_specs, cast_out_shapes = [], [], []
    for w, axis in cast_next:
        shape = w.shape[1:]
        assert shape[axis] % CAST_CHUNK == 0
        units = shape[axis] // CAST_CHUNK
        n_chunks = max(k for k in range(1, units + 1) if units % k == 0 and k <= n_steps)
        block = tuple(n // n_chunks if a == axis else n for a, n in enumerate(shape))

        def index(lead, n_chunks=n_chunks, axis=axis):
            def index_map(bb, i):
                c = jnp.minimum(bb * n_i + i, n_chunks - 1)
                return (lead,) + tuple(c if a == axis else 0 for a in range(2))
            return index_map

        cast_specs.append(pl.BlockSpec((None,) + block, index(layer + 1)))
        cast_out_specs.append(pl.BlockSpec((None,) + block, index(0)))
        cast_out_shapes.append(jax.ShapeDtypeStruct((1,) + shape, BF16))
    out = pl.pallas_call(
        functools.partial(_mlp_kernel, final=final, n_cast=len(cast_next)),
        out_shape=[jax.ShapeDtypeStruct(x.shape, F32)] + cast_out_shapes,
        grid=(b, n_i),
        in_specs=[
            pl.BlockSpec((None, tm, d), lambda bb, i: (bb, i, 0)),
            _layer_spec((1, d), layer),
            _mod_spec(layer, 3, ctx_row),
            _mod_spec(layer, 4, ctx_row),
            _mod_spec(layer, 5, ctx_row),
            _layer_spec((d, D_FF), 0, single_buffer=True),
            _layer_spec((D_FF, d), 0, single_buffer=True),
            pl.BlockSpec((1, d), lambda bb, i: (0, 0)),
        ] + cast_specs,
        out_specs=[pl.BlockSpec((None, tm, d), lambda bb, i: (bb, i, 0))] + cast_out_specs,
        scratch_shapes=[pltpu.VMEM((tm, d), F32)],
        compiler_params=_params("arbitrary", "arbitrary"),
        name="mlp",
    )(x, g2, mod5, mod5, mod5, w1, w2, g_final, *[w for w, _ in cast_next])
    return out[0], tuple(out[1:])


def _block_diag(w):
    *lead, h, n, _ = w.shape
    eye = jnp.eye(h, dtype=w.dtype)
    out = jnp.einsum('...hij,hg->...higj', w, eye)
    return out.reshape(*lead, h * n, h * n)


def _edge_tiles(table, ts):
    n_t = table.shape[0] // ts
    mid = min(1, n_t - 1)
    return jnp.concatenate([table[:ts], table[mid * ts:(mid + 1) * ts], table[-ts:]], axis=0)


def _tile_rows(s, want):
    t = min(s, want)
    while s % t:
        t //= 2
    return t


def kernel(x, c, ctx, c_ctx, w_mod, b_mod, g_norm1, g_norm2, w_in, conv_a_w, conv_a_b, lru_wa, lru_ba, lru_wx, lru_bx, lru_lambda, pool_w, pool_scale, sgu_ln_g, sgu_ln_b, sgu_w, sgu_b, conv_d_w, conv_d_b, conv_ln_g, conv_ln_b, w_out, w_mlp1, w_mlp2, g_final):
    depth = w_in.shape[0]
    bsz, seq, d = x.shape
    ctx_len = ctx.shape[1]
    assert d == D_MODEL and bsz < MOD_ROWS
    assert seq % SGU_CHUNK == 0 and ctx_len % SGU_CHUNK == 0
    ctx_row = bsz

    big = ((w_mlp1, 1), (w_mlp2, 0), (w_in, 1), (w_out, 1))
    w1_l, w2_l, w_in_l, w_out_l = (w[0:1].astype(BF16) for w, _ in big)
    wa_bd = _block_diag(lru_wa)
    wx_bd = _block_diag(lru_wx)
    wg = jnp.concatenate([wa_bd[:, 0], wx_bd[:, 0], wa_bd[:, 1], wx_bd[:, 1]], axis=-1).astype(BF16)
    bg = jnp.concatenate([lru_ba[:, 0], lru_bx[:, 0], lru_ba[:, 1], lru_bx[:, 1]], axis=-1)[:, None, :]
    lam = lru_lambda.reshape(depth, 1, 2 * W_LRU)
    pool_bd = _block_diag(pool_w).astype(BF16)
    sgu_ws = jnp.swapaxes(sgu_w, 1, 2).reshape(depth, SGU_CHUNK, SGU_HEADS * SGU_CHUNK).astype(BF16)
    sgu_bias = jnp.repeat(jnp.swapaxes(sgu_b, 1, 2), SGU_HEAD_DIM, axis=2)
    row = lambda v: v[:, None, :]

    cs = jnp.zeros((MOD_ROWS, d), F32).at[:bsz].set(c).at[ctx_row].set(c_ctx)
    mod = _modulation(cs, w_mod, b_mod)
    mod5 = mod.reshape(depth, MOD_ROWS, N_MOD, 1, d)

    ts = _tile_rows(seq, 1024)
    ts_c = _tile_rows(ctx_len, 512)
    tm = _tile_rows(seq, 1024)
    tm_c = _tile_rows(bsz * ctx_len, 1024)
    assert ts % GRID_W == 0 and ts % CONV_ROWS == 0 and ts_c % CONV_ROWS == 0
    pos_e = _pos_table()
    xc = ctx
    zeros_h = jnp.zeros((bsz, 1, W_LRU), F32)
    g_fin = g_final.reshape(1, d)
    mix_w = (_pool_bands(), pool_bd, row(pool_scale), row(sgu_ln_g), row(sgu_ln_b), sgu_ws,
             sgu_bias, conv_d_w, row(conv_d_b), row(conv_ln_g), row(conv_ln_b))
    icnt = _edge_tiles(_pool_icnt(seq, ts), ts)
    icnt_c = _edge_tiles(_pool_icnt(ctx_len, ts_c), ts_c)

    for l in range(depth):
        last = l == depth - 1
        first = l == 0
        pc, lru_c, hcf_last = _proj_lru(xc, pos_e, mod5, row(g_norm1), w_in_l, conv_a_w,
                                        row(conv_a_b), wg, bg, lam, zeros_h, l, ctx_row, ts_c, False)
        if last:
            hcb_first = _bwd_state(lru_c)
        else:
            xc, hcb_first = _mix(xc, pos_e, pc, lru_c, zeros_h, mod5, icnt_c, *mix_w, w_out_l, l,
                                 ctx_row, ts_c, False)
            xc, _ = _mlp(xc.reshape(1, -1, d), mod5, row(g_norm2), w1_l, w2_l, g_fin, l, ctx_row,
                         tm_c, False)
            xc = xc.reshape(bsz, ctx_len, d)
        px, lru, _ = _proj_lru(x, pos_e, mod5, row(g_norm1), w_in_l, conv_a_w, row(conv_a_b),
                               wg, bg, lam, hcf_last, l, None, ts, first)
        x, _ = _mix(x, pos_e, px, lru, hcb_first, mod5, icnt, *mix_w, w_out_l, l, None, ts, first)
        x, nxt = _mlp(x, mod5, row(g_norm2), w1_l, w2_l, g_fin, l, None, tm, last,
                      cast_next=() if last else big)
        if not last:
            w1_l, w2_l, w_in_l, w_out_l = nxt
    return x
```

```python
import functools
import math

import jax
import jax.numpy as jnp
from jax import lax
from jax.experimental import pallas as pl
from jax.experimental.pallas import tpu as pltpu

F32 = jnp.float32
BF16 = jnp.bfloat16

D_MODEL = 1024
GRID_W = 64
W_LRU = 256
W_POOL = 256
W_SGU = 256
W_CONV = 256
LRU_HEADS = 4
LRU_CONV_W = 4
LRU_PAD_L = 2
LRU_C = 8.0
POOL_WINDOWS = (2, 4, 8, 16)
POOL_GROUP = W_POOL // len(POOL_WINDOWS)
SGU_CHUNK = 128
SGU_HEADS = 4
SGU_HEAD_DIM = W_SGU // SGU_HEADS
CONF_CONV_W = 31
CONF_PAD = 15
D_FF = 4 * D_MODEL
D_IN = 2 * W_LRU + W_POOL + 2 * W_SGU + 2 * W_CONV
D_MIX_IN = D_IN - W_LRU
N_MOD = 6
EPS = 1e-6
POS_BASE = 10000.0

SUBLANES = 8
HALO = 16
VMEM_LIMIT = 56 * 1024 * 1024
MOD_ROWS = 8
FF_CHUNK = 512
CONV_ROWS = 256
POOL_ROWS = 128
INPROJ_CHUNKS = 2
CAST_CHUNK = 256


def _params(*sem):
    return pltpu.CompilerParams(dimension_semantics=sem, vmem_limit_bytes=VMEM_LIMIT)


def _sigmoid(x):
    return 0.5 * jnp.tanh(0.5 * x) + 0.5


def _silu(x):
    return x * _sigmoid(x)


def _gelu_tanh(x):
    return 0.5 * x * (1.0 + jnp.tanh(0.7978845608028654 * (x + 0.044715 * (x * x * x))))


def _rms_mod(x, g, shift, scale):
    ms = jnp.mean(x * x, axis=-1, keepdims=True)
    y = x * lax.rsqrt(ms + EPS) * g
    return y * (1.0 + scale) + shift


def _layer_norm(x, g, b):
    mu = jnp.mean(x, axis=-1, keepdims=True)
    xc = x - mu
    var = jnp.mean(xc * xc, axis=-1, keepdims=True)
    return xc * lax.rsqrt(var + EPS) * g + b


def _mod_kernel(c_ref, w_ref, b_ref, o_ref):
    s = _silu(c_ref[...]).astype(BF16)
    o_ref[...] = jnp.dot(s, w_ref[...].astype(BF16), preferred_element_type=F32) + b_ref[...]


def _modulation(cs, w_mod, b_mod):
    depth, _, n = w_mod.shape
    tn = 1536
    return pl.pallas_call(
        _mod_kernel,
        out_shape=jax.ShapeDtypeStruct((depth, MOD_ROWS, n), F32),
        grid=(depth, n // tn),
        in_specs=[
            pl.BlockSpec((MOD_ROWS, D_MODEL), lambda l, j: (0, 0)),
            pl.BlockSpec((None, D_MODEL, tn), lambda l, j: (l, 0, j)),
            pl.BlockSpec((None, 1, tn), lambda l, j: (l, 0, j)),
        ],
        out_specs=pl.BlockSpec((None, MOD_ROWS, tn), lambda l, j: (l, 0, j)),
        compiler_params=_params("arbitrary", "arbitrary"),
        name="modulation",
    )(cs, w_mod, b_mod.reshape(depth, 1, n))


def _pos_table_kernel(o_ref):
    quarter = D_MODEL // 4
    pos = lax.broadcasted_iota(jnp.int32, (GRID_W, quarter), 0).astype(F32)
    j = lax.broadcasted_iota(jnp.int32, (GRID_W, quarter), 1).astype(F32)
    omega = 1.0 / jnp.exp(j * (math.log(POS_BASE) / quarter))
    ang = pos * omega
    o_ref[:, :quarter] = jnp.sin(ang)
    o_ref[:, quarter:] = jnp.cos(ang)


def _pos_table():
    return pl.pallas_call(
        _pos_table_kernel,
        out_shape=jax.ShapeDtypeStruct((GRID_W, D_MODEL // 2), F32),
        name="pos_table",
    )()


def _pos_rows(e_ref, tile, ts):
    reps = ts // GRID_W
    col = jnp.concatenate([e_ref[...]] * reps, axis=0)
    r0 = tile * reps
    rows = [jnp.broadcast_to(e_ref[pl.ds(r0 + k, 1), :], (GRID_W, D_MODEL // 2)) for k in range(reps)]
    return jnp.concatenate([jnp.concatenate(rows, axis=0), col], axis=1)


def _mod_spec(layer, k, ctx_row):
    if ctx_row is None:
        return pl.BlockSpec((None, None, None, 1, D_MODEL), lambda b, i: (layer, b, k, 0, 0))
    return pl.BlockSpec((None, None, None, 1, D_MODEL), lambda b, i: (layer, ctx_row, k, 0, 0))


def _layer_spec(shape, layer, single_buffer=False):
    zeros = (0,) * len(shape)
    kwargs = dict(pipeline_mode=pl.Buffered(1)) if single_buffer else {}
    return pl.BlockSpec((None,) + tuple(shape), lambda b, i: (layer,) + zeros, **kwargs)


def _pos_spec():
    return pl.BlockSpec((GRID_W, D_MODEL // 2), lambda b, i: (0, 0))


def _scan_tile(a_ref, b_ref, out_ref, carry0, n_rows, reverse, unroll=False):
    width = a_ref.shape[-1]
    n_groups = n_rows // SUBLANES
    row = lax.broadcasted_iota(jnp.int32, (SUBLANES, width), 0)

    def group(gi, carry):
        g = (n_groups - 1 - gi) if reverse else gi
        start = g * SUBLANES if unroll else pl.multiple_of(g * SUBLANES, SUBLANES)
        a = a_ref[pl.ds(start, SUBLANES), :]
        h = b_ref[pl.ds(start, SUBLANES), :]
        for s in (1, 2, 4):
            if reverse:
                keep = row < SUBLANES - s
                shift = SUBLANES - s
            else:
                keep = row >= s
                shift = s
            h_sh = jnp.where(keep, pltpu.roll(h, shift, 0), 0.0)
            a_sh = jnp.where(keep, pltpu.roll(a, shift, 0), 1.0)
            h = h + a * h_sh
            a = a * a_sh
        h = h + a * carry
        out_ref[pl.ds(start, SUBLANES), :] = h
        edge = h[0:1, :] if reverse else h[SUBLANES - 1:SUBLANES, :]
        return jnp.broadcast_to(edge, (SUBLANES, width))

    carry = jnp.broadcast_to(carry0, (SUBLANES, width))
    if unroll:
        for gi in range(n_groups):
            carry = group(gi, carry)
        return carry[0:1, :]
    return lax.fori_loop(0, n_groups, group, carry, unroll=2)[0:1, :]


def _proj_lru_kernel(x_ref, xn_ref, e_ref, g_ref, sh_ref, sc_ref, w_ref,
                     cw_ref, cb_ref, wg_ref, bg_ref, lam_ref, h0_ref, *rest, ts, add_pos, chunks):
    n_cast = len(chunks)
    cast_in, rest = rest[:n_cast], rest[n_cast:]
    px_ref, lru_ref, hlast_ref = rest[:3]
    cast_out, (ext_scr, tail_scr, a_scr, b_scr, carry_scr) = rest[3:3 + n_cast], rest[3 + n_cast:]
    _cast_chunks(cast_in, cast_out, chunks)
    i = pl.program_id(1)
    n_t = pl.num_programs(1)
    reps = ts // GRID_W

    @pl.when(i == 0)
    def _():
        carry_scr[...] = h0_ref[...]
        tail_scr[...] = jnp.zeros(tail_scr.shape, F32)

    ext_scr[0:HALO, :] = tail_scr[...]

    x = x_ref[...]
    if add_pos:
        x = x + _pos_rows(e_ref, i, ts)
    rows = ts // INPROJ_CHUNKS
    for c in range(INPROJ_CHUNKS):
        sl = slice(c * rows, (c + 1) * rows)
        h = _rms_mod(x[sl, :], g_ref[...], sh_ref[...], sc_ref[...])
        res = jnp.dot(h.astype(BF16), w_ref[...], preferred_element_type=F32)
        px_ref[sl, :] = res[:, W_LRU:].astype(px_ref.dtype)
        ext_scr[HALO + c * rows:HALO + (c + 1) * rows, :] = res[:, :W_LRU]

    xn = xn_ref[...]
    if add_pos:
        half = D_MODEL // 2
        pos_n = jnp.concatenate(
            [jnp.broadcast_to(e_ref[pl.ds(jnp.minimum((i + 1) * reps, GRID_W - 1), 1), :],
                              (SUBLANES, half)),
             e_ref[0:SUBLANES, :]], axis=1)
        xn = xn + pos_n
    hn = _rms_mod(xn, g_ref[...], sh_ref[...], sc_ref[...])
    pn = jnp.dot(hn.astype(BF16), w_ref[:, :W_LRU], preferred_element_type=F32)
    ext_scr[HALO + ts:HALO + ts + SUBLANES, :] = jnp.where(i < n_t - 1, pn, 0.0)
    tail_scr[...] = ext_scr[ts:ts + HALO, :]

    cx = jnp.broadcast_to(cb_ref[...], (ts, W_LRU))
    for k in range(LRU_CONV_W):
        off = HALO + k - LRU_PAD_L
        cx = cx + cw_ref[k:k + 1, :] * ext_scr[off:off + ts, :]

    gates = jnp.dot(cx.astype(BF16), wg_ref[...], preferred_element_type=F32) + bg_ref[...]
    neg_lam = -lam_ref[...]
    sp = jnp.maximum(neg_lam, 0.0) + jnp.log1p(jnp.exp(-jnp.abs(neg_lam)))
    lam_c = -LRU_C * sp
    for d in range(2):
        r = _sigmoid(gates[:, (2 * d) * W_LRU:(2 * d + 1) * W_LRU])
        ig = _sigmoid(gates[:, (2 * d + 1) * W_LRU:(2 * d + 2) * W_LRU])
        log_a = lam_c[:, d * W_LRU:(d + 1) * W_LRU] * r
        a = jnp.exp(log_a)
        t = jnp.tanh(log_a)
        z = -2.0 * t / (1.0 - t)
        bmul = jnp.where(z > 0.0, z * lax.rsqrt(z), 0.0)
        bb = bmul * (ig * cx)
        if d == 0:
            a_scr[...] = a
            b_scr[...] = bb
        else:
            lru_ref[:, W_LRU:2 * W_LRU] = a
            lru_ref[:, 2 * W_LRU:] = bb

    carry = _scan_tile(a_scr, b_scr, lru_ref.at[:, :W_LRU], carry_scr[...], ts, reverse=False,
                       unroll=True)
    carry_scr[...] = carry
    hlast_ref[...] = carry


def _proj_lru(x, pos_e, mod5, g1, w_in, conv_w, conv_b, wg, bg, lam, h0, layer, ctx_row, ts,
              add_pos, cast=()):
    b, s, d = x.shape
    n8 = ts // SUBLANES
    last8 = s // SUBLANES - 1
    tile = lambda bb, i: (bb, i, 0)
    cast_specs, cast_out_specs, cast_out_shapes, chunks = _cast_plan(cast, layer, b, s // ts)
    out = pl.pallas_call(
        functools.partial(_proj_lru_kernel, ts=ts, add_pos=add_pos, chunks=chunks),
        out_shape=[
            jax.ShapeDtypeStruct((b, s, D_MIX_IN), BF16),
            jax.ShapeDtypeStruct((b, s, 3 * W_LRU), F32),
            jax.ShapeDtypeStruct((b, 1, W_LRU), F32),
        ] + cast_out_shapes,
        grid=(b, s // ts),
        in_specs=[
            pl.BlockSpec((None, ts, d), tile),
            pl.BlockSpec((None, SUBLANES, d), lambda bb, i: (bb, jnp.minimum((i + 1) * n8, last8), 0)),
            _pos_spec(),
            _layer_spec((1, d), layer),
            _mod_spec(layer, 0, ctx_row),
            _mod_spec(layer, 1, ctx_row),
            _layer_spec((d, D_IN), 0),
            _layer_spec((LRU_CONV_W, W_LRU), layer),
            _layer_spec((1, W_LRU), layer),
            _layer_spec((W_LRU, 4 * W_LRU), layer),
            _layer_spec((1, 4 * W_LRU), layer),
            _layer_spec((1, 2 * W_LRU), layer),
            pl.BlockSpec((None, 1, W_LRU), lambda bb, i: (bb, 0, 0)),
        ] + cast_specs,
        out_specs=[
            pl.BlockSpec((None, ts, D_MIX_IN), tile),
            pl.BlockSpec((None, ts, 3 * W_LRU), tile),
            pl.BlockSpec((None, 1, W_LRU), lambda bb, i: (bb, 0, 0)),
        ] + cast_out_specs,
        scratch_shapes=[
            pltpu.VMEM((ts + 2 * HALO, W_LRU), F32),
            pltpu.VMEM((HALO, W_LRU), F32),
            pltpu.VMEM((ts, W_LRU), F32),
            pltpu.VMEM((ts, W_LRU), F32),
            pltpu.VMEM((1, W_LRU), F32),
        ],
        compiler_params=_params("arbitrary", "arbitrary"),
        name="proj_lru",
    )(x, x, pos_e, g1, mod5, mod5, w_in, conv_w, conv_b, wg, bg, lam, h0, *[w for w, _ in cast])
    return out[0], out[1], out[2], tuple(out[3:])


def _pool_windows(shape):
    grp = lax.broadcasted_iota(jnp.int32, shape, 1) // POOL_GROUP
    win = jnp.full(shape, POOL_WINDOWS[-1], jnp.int32)
    for g in range(len(POOL_WINDOWS) - 2, -1, -1):
        win = jnp.where(grp == g, POOL_WINDOWS[g], win)
    return win


def _pool_icnt_kernel(o_ref, *, ts, seq):
    win = _pool_windows((ts, W_POOL))
    lo = win // 2
    hi = win - lo - 1
    tpos = pl.program_id(0) * ts + lax.broadcasted_iota(jnp.int32, (ts, W_POOL), 0)
    cnt = (jnp.minimum(tpos + hi + 1, seq) - jnp.maximum(tpos - lo, 0)).astype(F32)
    o_ref[...] = 1.0 / cnt


def _pool_icnt(seq, ts):
    return pl.pallas_call(
        functools.partial(_pool_icnt_kernel, ts=ts, seq=seq),
        out_shape=jax.ShapeDtypeStruct((seq, W_POOL), F32),
        grid=(seq // ts,),
        out_specs=pl.BlockSpec((ts, W_POOL), lambda i: (i, 0)),
        compiler_params=_params("arbitrary"),
        name="pool_icnt",
    )()


def _pool_bands():
    t = jnp.arange(POOL_ROWS)[:, None]
    tok = jnp.arange(POOL_ROWS + 2 * HALO)[None, :] - HALO
    bands = []
    for w in POOL_WINDOWS:
        lo = w // 2
        hi = w - lo - 1
        bands.append((tok >= t - lo) & (tok <= t + hi))
    return jnp.stack(bands).astype(BF16)


def _mix_kernel(x_ref, e_ref, px_ref, prev_ref, next_ref, lru_ref, h0_ref, gt_ref,
                icnt_ref, pb_ref, pw_ref, ps_ref, slg_ref, slb_ref, sw_ref, sb_ref,
                dw_ref, db_ref, dlg_ref, dlb_ref, wo_ref,
                o_ref, hfirst_ref,
                hb_scr, carry_scr, zext_scr, yext_scr, shift_scr, ycat_scr, *, ts, add_pos):
    i = pl.program_id(1)
    n_t = pl.num_programs(1)
    tile = n_t - 1 - i
    c_gate, c_pool, c_u, c_v, c_cv, c_cg = (W_LRU * k for k in range(6))
    hf_ref = lru_ref.at[:, :W_LRU]
    ab_ref = lru_ref.at[:, W_LRU:]
    has_prev = tile > 0
    has_next = tile < n_t - 1

    @pl.when(i == 0)
    def _():
        carry_scr[...] = h0_ref[...]

    carry = _scan_tile(ab_ref.at[:, :W_LRU], ab_ref.at[:, W_LRU:], hb_scr, carry_scr[...], ts,
                       reverse=True, unroll=True)
    carry_scr[...] = carry
    hfirst_ref[...] = carry
    y_a = (hf_ref[...] + hb_scr[...]) * _gelu_tanh(px_ref[:, c_gate:c_gate + W_LRU].astype(F32))
    ycat_scr[:, :W_LRU] = y_a.astype(BF16)

    zext_scr[0:HALO, :] = jnp.where(has_prev, prev_ref[:, c_pool:c_pool + W_POOL], 0.0)
    zext_scr[HALO:HALO + ts, :] = px_ref[:, c_pool:c_pool + W_POOL]
    zext_scr[HALO + ts:, :] = jnp.where(has_next, next_ref[:, c_pool:c_pool + W_POOL], 0.0)
    grp = lax.broadcasted_iota(jnp.int32, (POOL_ROWS, W_POOL), 1) // POOL_GROUP
    for blk in range(ts // POOL_ROWS):
        r0 = blk * POOL_ROWS
        zb = zext_scr[r0:r0 + POOL_ROWS + 2 * HALO, :]
        wsum = jnp.dot(pb_ref[len(POOL_WINDOWS) - 1], zb, preferred_element_type=F32)
        for g in range(len(POOL_WINDOWS) - 2, -1, -1):
            wsum = jnp.where(grp == g, jnp.dot(pb_ref[g], zb, preferred_element_type=F32), wsum)
        rows = slice(r0, r0 + POOL_ROWS)
        pooled = wsum * icnt_ref[rows, :] - px_ref[rows, c_pool:c_pool + W_POOL].astype(F32)
        y_b = jnp.dot(pooled.astype(BF16), pw_ref[...], preferred_element_type=F32) * ps_ref[...]
        ycat_scr[rows, W_LRU:W_LRU + W_POOL] = y_b.astype(BF16)

    vn = _layer_norm(px_ref[:, c_v:c_v + W_SGU].astype(F32), slg_ref[...], slb_ref[...]).astype(BF16)
    head = lax.broadcasted_iota(jnp.int32, (SGU_CHUNK, W_SGU), 1) // SGU_HEAD_DIM
    zero = jnp.zeros((SGU_CHUNK, W_SGU), BF16)
    for c in range(ts // SGU_CHUNK):
        rows = slice(c * SGU_CHUNK, (c + 1) * SGU_CHUNK)
        vh = jnp.concatenate([jnp.where(head == h, vn[rows, :], zero) for h in range(SGU_HEADS)],
                             axis=0)
        zc = jnp.dot(sw_ref[...], vh, preferred_element_type=F32) + sb_ref[...]
        y_c = px_ref[rows, c_u:c_u + W_SGU].astype(F32) * zc
        ycat_scr[rows, W_LRU + W_POOL:W_LRU + W_POOL + W_SGU] = y_c.astype(BF16)

    def gated(ref):
        return (ref[:, c_cv:c_cv + W_CONV].astype(F32)
                * _sigmoid(ref[:, c_cg:c_cg + W_CONV].astype(F32)))

    yext_scr[0:HALO, :] = jnp.where(has_prev, gated(prev_ref), 0.0)
    yext_scr[HALO:HALO + ts, :] = gated(px_ref)
    yext_scr[HALO + ts:, :] = jnp.where(has_next, gated(next_ref), 0.0)
    n_shift = ts + 2 * HALO - SUBLANES
    for r in range(1, SUBLANES):
        shift_scr[r - 1] = yext_scr[r:r + n_shift, :]
    for blk in range(ts // CONV_ROWS):
        r0 = blk * CONV_ROWS
        acc = jnp.broadcast_to(db_ref[...], (CONV_ROWS, W_CONV))
        for k in range(CONF_CONV_W):
            off = HALO + k - CONF_PAD
            src = r0 + (off // SUBLANES) * SUBLANES
            if off % SUBLANES:
                tap = shift_scr[off % SUBLANES - 1, src:src + CONV_ROWS, :]
            else:
                tap = yext_scr[src:src + CONV_ROWS, :]
            acc = acc + dw_ref[k:k + 1, :] * tap
        y_d = _silu(_layer_norm(acc, dlg_ref[...], dlb_ref[...]))
        ycat_scr[r0:r0 + CONV_ROWS, W_LRU + W_POOL + W_SGU:] = y_d.astype(BF16)

    x = x_ref[...]
    if add_pos:
        x = x + _pos_rows(e_ref, tile, ts)
    o_ref[...] = x + gt_ref[...] * jnp.dot(ycat_scr[...], wo_ref[...], preferred_element_type=F32)


def _mix(x, pos_e, px, lru, h0b, mod5, icnt, pool_band, pool_bd, pool_scale, sgu_ln_g, sgu_ln_b,
         sgu_ws, sgu_bias, conv_d_w, conv_d_b, conv_ln_g, conv_ln_b, w_out, layer, ctx_row, ts,
         add_pos):
    b, s, d = x.shape
    n_t = s // ts
    nh = ts // HALO
    lasth = s // HALO - 1
    kernel = functools.partial(_mix_kernel, ts=ts, add_pos=add_pos)

    def rev(bb, i):
        return (bb, n_t - 1 - i, 0)

    def icnt_map(bb, i):
        tile = n_t - 1 - i
        return (jnp.where(tile == 0, 0, jnp.where(tile == n_t - 1, 2, 1)), 0)

    return pl.pallas_call(
        kernel,
        out_shape=(
            jax.ShapeDtypeStruct((b, s, d), F32),
            jax.ShapeDtypeStruct((b, 1, W_LRU), F32),
        ),
        grid=(b, n_t),
        in_specs=[
            pl.BlockSpec((None, ts, d), rev),
            _pos_spec(),
            pl.BlockSpec((None, ts, D_MIX_IN), rev),
            pl.BlockSpec((None, HALO, D_MIX_IN),
                         lambda bb, i: (bb, jnp.maximum((n_t - 1 - i) * nh - 1, 0), 0)),
            pl.BlockSpec((None, HALO, D_MIX_IN),
                         lambda bb, i: (bb, jnp.minimum((n_t - i) * nh, lasth), 0)),
            pl.BlockSpec((None, ts, 3 * W_LRU), rev),
            pl.BlockSpec((None, 1, W_LRU), lambda bb, i: (bb, 0, 0)),
            _mod_spec(layer, 2, ctx_row),
            pl.BlockSpec((ts, W_POOL), icnt_map),
            pl.BlockSpec(pool_band.shape, lambda bb, i: (0, 0, 0)),
            _layer_spec((W_POOL, W_POOL), layer),
            _layer_spec((1, W_POOL), layer),
            _layer_spec((1, W_SGU), layer),
            _layer_spec((1, W_SGU), layer),
            _layer_spec((SGU_CHUNK, SGU_HEADS * SGU_CHUNK), layer),
            _layer_spec((SGU_CHUNK, W_SGU), layer),
            _layer_spec((CONF_CONV_W, W_CONV), layer),
            _layer_spec((1, W_CONV), layer),
            _layer_spec((1, W_CONV), layer),
            _layer_spec((1, W_CONV), layer),
            _layer_spec((d, d), 0),
        ],
        out_specs=(
            pl.BlockSpec((None, ts, d), rev),
            pl.BlockSpec((None, 1, W_LRU), lambda bb, i: (bb, 0, 0)),
        ),
        scratch_shapes=[
            pltpu.VMEM((ts, W_LRU), F32),
            pltpu.VMEM((1, W_LRU), F32),
            pltpu.VMEM((ts + 2 * HALO, W_POOL), BF16),
            pltpu.VMEM((ts + 2 * HALO, W_CONV), F32),
            pltpu.VMEM((SUBLANES - 1, ts + 2 * HALO - SUBLANES, W_CONV), F32),
            pltpu.VMEM((ts, d), BF16),
        ],
        compiler_params=_params("arbitrary", "arbitrary"),
        name="mix",
    )(x, pos_e, px, px, px, lru, h0b, mod5, icnt, pool_band, pool_bd, pool_scale, sgu_ln_g,
      sgu_ln_b, sgu_ws, sgu_bias, conv_d_w, conv_d_b, conv_ln_g, conv_ln_b, w_out)


def _bwd_state_kernel(lru_ref, o_ref, hb_scr, *, rows):
    zero = jnp.zeros((1, W_LRU), F32)
    o_ref[...] = _scan_tile(lru_ref.at[:, W_LRU:2 * W_LRU], lru_ref.at[:, 2 * W_LRU:], hb_scr, zero,
                            rows, reverse=True)


def _bwd_state(lru):
    b, s, _ = lru.shape
    return pl.pallas_call(
        functools.partial(_bwd_state_kernel, rows=s),
        out_shape=jax.ShapeDtypeStruct((b, 1, W_LRU), F32),
        grid=(b,),
        in_specs=[pl.BlockSpec((None, s, 3 * W_LRU), lambda bb: (bb, 0, 0))],
        out_specs=pl.BlockSpec((None, 1, W_LRU), lambda bb: (bb, 0, 0)),
        scratch_shapes=[pltpu.VMEM((s, W_LRU), F32)],
        compiler_params=_params("arbitrary"),
        name="bwd_state",
    )(lru)


def _mlp_kernel(x_ref, g_ref, sh_ref, sc_ref, gt_ref, w1_ref, w2_ref, gf_ref, *rest, final, chunks):
    n_cast = len(chunks)
    cast_in, o_ref, cast_out, acc_scr = rest[:n_cast], rest[n_cast], rest[n_cast + 1:-1], rest[-1]
    _cast_chunks(cast_in, cast_out, chunks)
    x = x_ref[...]
    h = _rms_mod(x, g_ref[...], sh_ref[...], sc_ref[...]).astype(BF16)
    for j in range(D_FF // FF_CHUNK):
        cols = slice(j * FF_CHUNK, (j + 1) * FF_CHUNK)
        u = jnp.dot(h, w1_ref[:, cols], preferred_element_type=F32)
        u = jnp.square(jnp.maximum(u, 0.0)).astype(BF16)
        part = jnp.dot(u, w2_ref[cols, :], preferred_element_type=F32)
        if j == 0:
            acc_scr[...] = part
        else:
            acc_scr[...] += part
    y = x + gt_ref[...] * acc_scr[...]
    if final:
        ms = jnp.mean(y * y, axis=-1, keepdims=True)
        y = y * lax.rsqrt(ms + EPS) * gf_ref[...]
    o_ref[...] = y


def _cast_plan(cast, src_layer, n_b, n_i):
    n_steps = n_b * n_i
    in_specs, out_specs, out_shapes, chunks = [], [], [], []
    for w, axis in cast:
        shape = w.shape[1:]
        assert shape[axis] % CAST_CHUNK == 0
        units = shape[axis] // CAST_CHUNK
        n_chunks = max(k for k in range(1, units + 1) if units % k == 0 and k <= n_steps)
        block = tuple(n // n_chunks if a == axis else n for a, n in enumerate(shape))

        def index(lead, n_chunks=n_chunks, axis=axis):
            def index_map(bb, i):
                c = jnp.minimum(bb * n_i + i, n_chunks - 1)
                return (lead,) + tuple(c if a == axis else 0 for a in range(2))
            return index_map

        in_specs.append(pl.BlockSpec((None,) + block, index(src_layer)))
        out_specs.append(pl.BlockSpec((None,) + block, index(0)))
        out_shapes.append(jax.ShapeDtypeStruct((1,) + shape, BF16))
        chunks.append(n_chunks)
    return in_specs, out_specs, out_shapes, tuple(chunks)


def _cast_chunks(cast_in, cast_out, chunks):
    step = pl.program_id(0) * pl.num_programs(1) + pl.program_id(1)
    for src, dst, n_chunks in zip(cast_in, cast_out, chunks):
        @pl.when(step < n_chunks)
        def _(src=src, dst=dst):
            dst[...] = src[...].astype(BF16)


def _mlp(x, mod5, g2, w1, w2, g_final, layer, ctx_row, tm, final, cast_next=()):
    b, s, d = x.shape
    n_i = s // tm
    cast_specs, cast_out_specs, cast_out_shapes, chunks = _cast_plan(cast_next, layer + 1, b, n_i)
    out = pl.pallas_call(
        functools.partial(_mlp_kernel, final=final, chunks=chunks),
        out_shape=[jax.ShapeDtypeStruct(x.shape, F32)] + cast_out_shapes,
        grid=(b, n_i),
        in_specs=[
            pl.BlockSpec((None, tm, d), lambda bb, i: (bb, i, 0)),
            _layer_spec((1, d), layer),
            _mod_spec(layer, 3, ctx_row),
            _mod_spec(layer, 4, ctx_row),
            _mod_spec(layer, 5, ctx_row),
            _layer_spec((d, D_FF), 0, single_buffer=True),
            _layer_spec((D_FF, d), 0, single_buffer=True),
            pl.BlockSpec((1, d), lambda bb, i: (0, 0)),
        ] + cast_specs,
        out_specs=[pl.BlockSpec((None, tm, d), lambda bb, i: (bb, i, 0))] + cast_out_specs,
        scratch_shapes=[pltpu.VMEM((tm, d), F32)],
        compiler_params=_params("arbitrary", "arbitrary"),
        name="mlp",
    )(x, g2, mod5, mod5, mod5, w1, w2, g_final, *[w for w, _ in cast_next])
    return out[0], tuple(out[1:])


def _block_diag(w):
    *lead, h, n, _ = w.shape
    eye = jnp.eye(h, dtype=w.dtype)
    out = jnp.einsum('...hij,hg->...higj', w, eye)
    return out.reshape(*lead, h * n, h * n)


def _edge_tiles(table, ts):
    n_t = table.shape[0] // ts
    mid = min(1, n_t - 1)
    return jnp.concatenate([table[:ts], table[mid * ts:(mid + 1) * ts], table[-ts:]], axis=0)


def _tile_rows(s, want):
    t = min(s, want)
    while s % t:
        t //= 2
    return t


def kernel(x, c, ctx, c_ctx, w_mod, b_mod, g_norm1, g_norm2, w_in, conv_a_w, conv_a_b, lru_wa, lru_ba, lru_wx, lru_bx, lru_lambda, pool_w, pool_scale, sgu_ln_g, sgu_ln_b, sgu_w, sgu_b, conv_d_w, conv_d_b, conv_ln_g, conv_ln_b, w_out, w_mlp1, w_mlp2, g_final):
    depth = w_in.shape[0]
    bsz, seq, d = x.shape
    ctx_len = ctx.shape[1]
    assert d == D_MODEL and bsz < MOD_ROWS
    assert seq % SGU_CHUNK == 0 and ctx_len % SGU_CHUNK == 0
    ctx_row = bsz

    big = ((w_mlp1, 1), (w_mlp2, 0), (w_in, 1), (w_out, 1))
    w_in_l, w_out_l = (w[0:1].astype(BF16) for w in (w_in, w_out))
    w1_l = w2_l = None
    wa_bd = _block_diag(lru_wa)
    wx_bd = _block_diag(lru_wx)
    wg = jnp.concatenate([wa_bd[:, 0], wx_bd[:, 0], wa_bd[:, 1], wx_bd[:, 1]], axis=-1).astype(BF16)
    bg = jnp.concatenate([lru_ba[:, 0], lru_bx[:, 0], lru_ba[:, 1], lru_bx[:, 1]], axis=-1)[:, None, :]
    lam = lru_lambda.reshape(depth, 1, 2 * W_LRU)
    pool_bd = _block_diag(pool_w).astype(BF16)
    sgu_ws = jnp.swapaxes(sgu_w, 1, 2).reshape(depth, SGU_CHUNK, SGU_HEADS * SGU_CHUNK).astype(BF16)
    sgu_bias = jnp.repeat(jnp.swapaxes(sgu_b, 1, 2), SGU_HEAD_DIM, axis=2)
    row = lambda v: v[:, None, :]

    cs = jnp.zeros((MOD_ROWS, d), F32).at[:bsz].set(c).at[ctx_row].set(c_ctx)
    mod = _modulation(cs, w_mod, b_mod)
    mod5 = mod.reshape(depth, MOD_ROWS, N_MOD, 1, d)

    ts = _tile_rows(seq, 1024)
    ts_c = _tile_rows(ctx_len, 512)
    tm = _tile_rows(seq, 1024)
    tm_c = _tile_rows(bsz * ctx_len, 1024)
    assert ts % GRID_W == 0 and ts % CONV_ROWS == 0 and ts_c % CONV_ROWS == 0
    pos_e = _pos_table()
    xc = ctx
    zeros_h = jnp.zeros((bsz, 1, W_LRU), F32)
    g_fin = g_final.reshape(1, d)
    mix_w = (_pool_bands(), pool_bd, row(pool_scale), row(sgu_ln_g), row(sgu_ln_b), sgu_ws,
             sgu_bias, conv_d_w, row(conv_d_b), row(conv_ln_g), row(conv_ln_b))
    icnt = _edge_tiles(_pool_icnt(seq, ts), ts)
    icnt_c = _edge_tiles(_pool_icnt(ctx_len, ts_c), ts_c)

    for l in range(depth):
        last = l == depth - 1
        first = l == 0
        pc, lru_c, hcf_last, _ = _proj_lru(xc, pos_e, mod5, row(g_norm1), w_in_l, conv_a_w,
                                           row(conv_a_b), wg, bg, lam, zeros_h, l, ctx_row, ts_c,
                                           False)
        px, lru, _, cast0 = _proj_lru(x, pos_e, mod5, row(g_norm1), w_in_l, conv_a_w, row(conv_a_b),
                                      wg, bg, lam, hcf_last, l, None, ts, first,
                                      cast=big[:2] if first else ())
        if first:
            w1_l, w2_l = cast0
        if last:
            hcb_first = _bwd_state(lru_c)
        else:
            xc, hcb_first = _mix(xc, pos_e, pc, lru_c, zeros_h, mod5, icnt_c, *mix_w, w_out_l, l,
                                 ctx_row, ts_c, False)
            xc, _ = _mlp(xc.reshape(1, -1, d), mod5, row(g_norm2), w1_l, w2_l, g_fin, l, ctx_row,
                         tm_c, False)
            xc = xc.reshape(bsz, ctx_len, d)
        x, _ = _mix(x, pos_e, px, lru, hcb_first, mod5, icnt, *mix_w, w_out_l, l, None, ts, first)
        x, nxt = _mlp(x, mod5, row(g_norm2), w1_l, w2_l, g_fin, l, None, tm, last,
                      cast_next=() if last else big)
        if not last:
            w1_l, w2_l, w_in_l, w_out_l = nxt
    return x
```

```python
import functools
import math

import jax
import jax.numpy as jnp
from jax import lax
from jax.experimental import pallas as pl
from jax.experimental.pallas import tpu as pltpu

F32 = jnp.float32
BF16 = jnp.bfloat16

D_MODEL = 1024
GRID_W = 64
W_LRU = 256
W_POOL = 256
W_SGU = 256
W_CONV = 256
LRU_HEADS = 4
LRU_CONV_W = 4
LRU_PAD_L = 2
LRU_C = 8.0
POOL_WINDOWS = (2, 4, 8, 16)
POOL_GROUP = W_POOL // len(POOL_WINDOWS)
SGU_CHUNK = 128
SGU_HEADS = 4
SGU_HEAD_DIM = W_SGU // SGU_HEADS
CONF_CONV_W = 31
CONF_PAD = 15
D_FF = 4 * D_MODEL
D_IN = 2 * W_LRU + W_POOL + 2 * W_SGU + 2 * W_CONV
D_MIX_IN = D_IN - W_LRU
N_MOD = 6
EPS = 1e-6
POS_BASE = 10000.0

SUBLANES = 8
HALO = 16
VMEM_LIMIT = 56 * 1024 * 1024
MOD_ROWS = 8
FF_CHUNK = 512
CONV_ROWS = 256
CONV_LANES = 128
POOL_ROWS = 128
INPROJ_CHUNKS = 2
CAST_CHUNK = 256


def _params(*sem):
    return pltpu.CompilerParams(dimension_semantics=sem, vmem_limit_bytes=VMEM_LIMIT)


def _sigmoid(x):
    return 0.5 * jnp.tanh(0.5 * x) + 0.5


def _silu(x):
    return x * _sigmoid(x)


def _gelu_tanh(x):
    return 0.5 * x * (1.0 + jnp.tanh(0.7978845608028654 * (x + 0.044715 * (x * x * x))))


def _rms_mod(x, g, shift, scale):
    ms = jnp.mean(x * x, axis=-1, keepdims=True)
    y = x * lax.rsqrt(ms + EPS) * g
    return y * (1.0 + scale) + shift


def _layer_norm(x, g, b):
    mu = jnp.mean(x, axis=-1, keepdims=True)
    xc = x - mu
    var = jnp.mean(xc * xc, axis=-1, keepdims=True)
    return xc * lax.rsqrt(var + EPS) * g + b


def _mod_kernel(c_ref, w_ref, b_ref, o_ref):
    s = _silu(c_ref[...]).astype(BF16)
    o_ref[...] = jnp.dot(s, w_ref[...].astype(BF16), preferred_element_type=F32) + b_ref[...]


def _modulation(cs, w_mod, b_mod):
    depth, _, n = w_mod.shape
    tn = 1536
    return pl.pallas_call(
        _mod_kernel,
        out_shape=jax.ShapeDtypeStruct((depth, MOD_ROWS, n), F32),
        grid=(depth, n // tn),
        in_specs=[
            pl.BlockSpec((MOD_ROWS, D_MODEL), lambda l, j: (0, 0)),
            pl.BlockSpec((None, D_MODEL, tn), lambda l, j: (l, 0, j)),
            pl.BlockSpec((None, 1, tn), lambda l, j: (l, 0, j)),
        ],
        out_specs=pl.BlockSpec((None, MOD_ROWS, tn), lambda l, j: (l, 0, j)),
        compiler_params=_params("arbitrary", "arbitrary"),
        name="modulation",
    )(cs, w_mod, b_mod.reshape(depth, 1, n))


def _pos_table_kernel(o_ref):
    quarter = D_MODEL // 4
    pos = lax.broadcasted_iota(jnp.int32, (GRID_W, quarter), 0).astype(F32)
    j = lax.broadcasted_iota(jnp.int32, (GRID_W, quarter), 1).astype(F32)
    omega = 1.0 / jnp.exp(j * (math.log(POS_BASE) / quarter))
    ang = pos * omega
    o_ref[:, :quarter] = jnp.sin(ang)
    o_ref[:, quarter:] = jnp.cos(ang)


def _pos_table():
    return pl.pallas_call(
        _pos_table_kernel,
        out_shape=jax.ShapeDtypeStruct((GRID_W, D_MODEL // 2), F32),
        name="pos_table",
    )()


def _pos_rows(e_ref, tile, ts):
    reps = ts // GRID_W
    col = jnp.concatenate([e_ref[...]] * reps, axis=0)
    r0 = tile * reps
    rows = [jnp.broadcast_to(e_ref[pl.ds(r0 + k, 1), :], (GRID_W, D_MODEL // 2)) for k in range(reps)]
    return jnp.concatenate([jnp.concatenate(rows, axis=0), col], axis=1)


def _mod_spec(layer, k, ctx_row):
    if ctx_row is None:
        return pl.BlockSpec((None, None, None, 1, D_MODEL), lambda b, i: (layer, b, k, 0, 0))
    return pl.BlockSpec((None, None, None, 1, D_MODEL), lambda b, i: (layer, ctx_row, k, 0, 0))


def _layer_spec(shape, layer, single_buffer=False):
    zeros = (0,) * len(shape)
    kwargs = dict(pipeline_mode=pl.Buffered(1)) if single_buffer else {}
    return pl.BlockSpec((None,) + tuple(shape), lambda b, i: (layer,) + zeros, **kwargs)


def _pos_spec():
    return pl.BlockSpec((GRID_W, D_MODEL // 2), lambda b, i: (0, 0))


def _scan_tile(a_ref, b_ref, out_ref, carry0, n_rows, reverse, unroll=False):
    width = a_ref.shape[-1]
    n_groups = n_rows // SUBLANES
    row = lax.broadcasted_iota(jnp.int32, (SUBLANES, width), 0)

    def group(gi, carry):
        g = (n_groups - 1 - gi) if reverse else gi
        start = g * SUBLANES if unroll else pl.multiple_of(g * SUBLANES, SUBLANES)
        a = a_ref[pl.ds(start, SUBLANES), :]
        h = b_ref[pl.ds(start, SUBLANES), :]
        for s in (1, 2, 4):
            if reverse:
                keep = row < SUBLANES - s
                shift = SUBLANES - s
            else:
                keep = row >= s
                shift = s
            h_sh = jnp.where(keep, pltpu.roll(h, shift, 0), 0.0)
            a_sh = jnp.where(keep, pltpu.roll(a, shift, 0), 1.0)
            h = h + a * h_sh
            a = a * a_sh
        h = h + a * carry
        out_ref[pl.ds(start, SUBLANES), :] = h
        edge = h[0:1, :] if reverse else h[SUBLANES - 1:SUBLANES, :]
        return jnp.broadcast_to(edge, (SUBLANES, width))

    carry = jnp.broadcast_to(carry0, (SUBLANES, width))
    if unroll:
        for gi in range(n_groups):
            carry = group(gi, carry)
        return carry[0:1, :]
    return lax.fori_loop(0, n_groups, group, carry, unroll=2)[0:1, :]


def _proj_lru_kernel(x_ref, xn_ref, e_ref, g_ref, sh_ref, sc_ref, w_ref,
                     cw_ref, cb_ref, wg_ref, bg_ref, lam_ref, h0_ref, *rest, ts, add_pos, chunks):
    n_cast = len(chunks)
    cast_in, rest = rest[:n_cast], rest[n_cast:]
    px_ref, lru_ref, hlast_ref = rest[:3]
    cast_out, (ext_scr, tail_scr, a_scr, b_scr, carry_scr) = rest[3:3 + n_cast], rest[3 + n_cast:]
    _cast_chunks(cast_in, cast_out, chunks)
    i = pl.program_id(1)
    n_t = pl.num_programs(1)
    reps = ts // GRID_W

    @pl.when(i == 0)
    def _():
        carry_scr[...] = h0_ref[...]
        tail_scr[...] = jnp.zeros(tail_scr.shape, F32)

    ext_scr[0:HALO, :] = tail_scr[...]

    x = x_ref[...]
    if add_pos:
        x = x + _pos_rows(e_ref, i, ts)
    rows = ts // INPROJ_CHUNKS
    for c in range(INPROJ_CHUNKS):
        sl = slice(c * rows, (c + 1) * rows)
        h = _rms_mod(x[sl, :], g_ref[...], sh_ref[...], sc_ref[...])
        res = jnp.dot(h.astype(BF16), w_ref[...], preferred_element_type=F32)
        px_ref[sl, :] = res[:, W_LRU:].astype(px_ref.dtype)
        ext_scr[HALO + c * rows:HALO + (c + 1) * rows, :] = res[:, :W_LRU]

    xn = xn_ref[...]
    if add_pos:
        half = D_MODEL // 2
        pos_n = jnp.concatenate(
            [jnp.broadcast_to(e_ref[pl.ds(jnp.minimum((i + 1) * reps, GRID_W - 1), 1), :],
                              (SUBLANES, half)),
             e_ref[0:SUBLANES, :]], axis=1)
        xn = xn + pos_n
    hn = _rms_mod(xn, g_ref[...], sh_ref[...], sc_ref[...])
    pn = jnp.dot(hn.astype(BF16), w_ref[:, :W_LRU], preferred_element_type=F32)
    ext_scr[HALO + ts:HALO + ts + SUBLANES, :] = jnp.where(i < n_t - 1, pn, 0.0)
    tail_scr[...] = ext_scr[ts:ts + HALO, :]

    cx = jnp.broadcast_to(cb_ref[...], (ts, W_LRU))
    for k in range(LRU_CONV_W):
        off = HALO + k - LRU_PAD_L
        cx = cx + cw_ref[k:k + 1, :] * ext_scr[off:off + ts, :]

    gates = jnp.dot(cx.astype(BF16), wg_ref[...], preferred_element_type=F32) + bg_ref[...]
    neg_lam = -lam_ref[...]
    sp = jnp.maximum(neg_lam, 0.0) + jnp.log1p(jnp.exp(-jnp.abs(neg_lam)))
    lam_c = -LRU_C * sp
    for d in range(2):
        r = _sigmoid(gates[:, (2 * d) * W_LRU:(2 * d + 1) * W_LRU])
        ig = _sigmoid(gates[:, (2 * d + 1) * W_LRU:(2 * d + 2) * W_LRU])
        log_a = lam_c[:, d * W_LRU:(d + 1) * W_LRU] * r
        a = jnp.exp(log_a)
        t = jnp.tanh(log_a)
        z = -2.0 * t / (1.0 - t)
        bmul = jnp.where(z > 0.0, z * lax.rsqrt(z), 0.0)
        bb = bmul * (ig * cx)
        if d == 0:
            a_scr[...] = a
            b_scr[...] = bb
        else:
            lru_ref[:, W_LRU:2 * W_LRU] = a
            lru_ref[:, 2 * W_LRU:] = bb

    carry = _scan_tile(a_scr, b_scr, lru_ref.at[:, :W_LRU], carry_scr[...], ts, reverse=False,
                       unroll=True)
    carry_scr[...] = carry
    hlast_ref[...] = carry


def _proj_lru(x, pos_e, mod5, g1, w_in, conv_w, conv_b, wg, bg, lam, h0, layer, ctx_row, ts,
              add_pos, cast=()):
    b, s, d = x.shape
    n8 = ts // SUBLANES
    last8 = s // SUBLANES - 1
    tile = lambda bb, i: (bb, i, 0)
    cast_specs, cast_out_specs, cast_out_shapes, chunks = _cast_plan(cast, layer, b, s // ts)
    out = pl.pallas_call(
        functools.partial(_proj_lru_kernel, ts=ts, add_pos=add_pos, chunks=chunks),
        out_shape=[
            jax.ShapeDtypeStruct((b, s, D_MIX_IN), BF16),
            jax.ShapeDtypeStruct((b, s, 3 * W_LRU), F32),
            jax.ShapeDtypeStruct((b, 1, W_LRU), F32),
        ] + cast_out_shapes,
        grid=(b, s // ts),
        in_specs=[
            pl.BlockSpec((None, ts, d), tile),
            pl.BlockSpec((None, SUBLANES, d), lambda bb, i: (bb, jnp.minimum((i + 1) * n8, last8), 0)),
            _pos_spec(),
            _layer_spec((1, d), layer),
            _mod_spec(layer, 0, ctx_row),
            _mod_spec(layer, 1, ctx_row),
            _layer_spec((d, D_IN), 0),
            _layer_spec((LRU_CONV_W, W_LRU), layer),
            _layer_spec((1, W_LRU), layer),
            _layer_spec((W_LRU, 4 * W_LRU), layer),
            _layer_spec((1, 4 * W_LRU), layer),
            _layer_spec((1, 2 * W_LRU), layer),
            pl.BlockSpec((None, 1, W_LRU), lambda bb, i: (bb, 0, 0)),
        ] + cast_specs,
        out_specs=[
            pl.BlockSpec((None, ts, D_MIX_IN), tile),
            pl.BlockSpec((None, ts, 3 * W_LRU), tile),
            pl.BlockSpec((None, 1, W_LRU), lambda bb, i: (bb, 0, 0)),
        ] + cast_out_specs,
        scratch_shapes=[
            pltpu.VMEM((ts + 2 * HALO, W_LRU), F32),
            pltpu.VMEM((HALO, W_LRU), F32),
            pltpu.VMEM((ts, W_LRU), F32),
            pltpu.VMEM((ts, W_LRU), F32),
            pltpu.VMEM((1, W_LRU), F32),
        ],
        compiler_params=_params("arbitrary", "arbitrary"),
        name="proj_lru",
    )(x, x, pos_e, g1, mod5, mod5, w_in, conv_w, conv_b, wg, bg, lam, h0, *[w for w, _ in cast])
    return out[0], out[1], out[2], tuple(out[3:])


def _pool_windows(shape):
    grp = lax.broadcasted_iota(jnp.int32, shape, 1) // POOL_GROUP
    win = jnp.full(shape, POOL_WINDOWS[-1], jnp.int32)
    for g in range(len(POOL_WINDOWS) - 2, -1, -1):
        win = jnp.where(grp == g, POOL_WINDOWS[g], win)
    return win


def _pool_icnt_kernel(o_ref, *, ts, seq):
    win = _pool_windows((ts, W_POOL))
    lo = win // 2
    hi = win - lo - 1
    tpos = pl.program_id(0) * ts + lax.broadcasted_iota(jnp.int32, (ts, W_POOL), 0)
    cnt = (jnp.minimum(tpos + hi + 1, seq) - jnp.maximum(tpos - lo, 0)).astype(F32)
    o_ref[...] = 1.0 / cnt


def _pool_icnt(seq, ts):
    return pl.pallas_call(
        functools.partial(_pool_icnt_kernel, ts=ts, seq=seq),
        out_shape=jax.ShapeDtypeStruct((seq, W_POOL), F32),
        grid=(seq // ts,),
        out_specs=pl.BlockSpec((ts, W_POOL), lambda i: (i, 0)),
        compiler_params=_params("arbitrary"),
        name="pool_icnt",
    )()


def _pool_bands():
    t = jnp.arange(POOL_ROWS)[:, None]
    tok = jnp.arange(POOL_ROWS + 2 * HALO)[None, :] - HALO
    bands = []
    for w in POOL_WINDOWS:
        lo = w // 2
        hi = w - lo - 1
        bands.append((tok >= t - lo) & (tok <= t + hi))
    return jnp.stack(bands).astype(BF16)


def _mix_kernel(x_ref, e_ref, px_ref, prev_ref, next_ref, lru_ref, h0_ref, gt_ref,
                icnt_ref, pb_ref, pw_ref, ps_ref, slg_ref, slb_ref, sw_ref, sb_ref,
                dw_ref, db_ref, dlg_ref, dlb_ref, wo_ref,
                o_ref, hfirst_ref,
                hb_scr, carry_scr, zext_scr, yext_scr, shift_scr, ycat_scr, *, ts, add_pos):
    i = pl.program_id(1)
    n_t = pl.num_programs(1)
    tile = n_t - 1 - i
    c_gate, c_pool, c_u, c_v, c_cv, c_cg = (W_LRU * k for k in range(6))
    hf_ref = lru_ref.at[:, :W_LRU]
    ab_ref = lru_ref.at[:, W_LRU:]
    has_prev = tile > 0
    has_next = tile < n_t - 1

    @pl.when(i == 0)
    def _():
        carry_scr[...] = h0_ref[...]

    carry = _scan_tile(ab_ref.at[:, :W_LRU], ab_ref.at[:, W_LRU:], hb_scr, carry_scr[...], ts,
                       reverse=True, unroll=True)
    carry_scr[...] = carry
    hfirst_ref[...] = carry
    y_a = (hf_ref[...] + hb_scr[...]) * _gelu_tanh(px_ref[:, c_gate:c_gate + W_LRU].astype(F32))
    ycat_scr[:, :W_LRU] = y_a.astype(BF16)

    zext_scr[0:HALO, :] = jnp.where(has_prev, prev_ref[:, c_pool:c_pool + W_POOL], 0.0)
    zext_scr[HALO:HALO + ts, :] = px_ref[:, c_pool:c_pool + W_POOL]
    zext_scr[HALO + ts:, :] = jnp.where(has_next, next_ref[:, c_pool:c_pool + W_POOL], 0.0)
    grp = lax.broadcasted_iota(jnp.int32, (POOL_ROWS, W_POOL), 1) // POOL_GROUP
    for blk in range(ts // POOL_ROWS):
        r0 = blk * POOL_ROWS
        zb = zext_scr[r0:r0 + POOL_ROWS + 2 * HALO, :]
        wsum = jnp.dot(pb_ref[len(POOL_WINDOWS) - 1], zb, preferred_element_type=F32)
        for g in range(len(POOL_WINDOWS) - 2, -1, -1):
            wsum = jnp.where(grp == g, jnp.dot(pb_ref[g], zb, preferred_element_type=F32), wsum)
        rows = slice(r0, r0 + POOL_ROWS)
        pooled = wsum * icnt_ref[rows, :] - px_ref[rows, c_pool:c_pool + W_POOL].astype(F32)
        y_b = jnp.dot(pooled.astype(BF16), pw_ref[...], preferred_element_type=F32) * ps_ref[...]
        ycat_scr[rows, W_LRU:W_LRU + W_POOL] = y_b.astype(BF16)

    vn = _layer_norm(px_ref[:, c_v:c_v + W_SGU].astype(F32), slg_ref[...], slb_ref[...]).astype(BF16)
    head = lax.broadcasted_iota(jnp.int32, (SGU_CHUNK, W_SGU), 1) // SGU_HEAD_DIM
    zero = jnp.zeros((SGU_CHUNK, W_SGU), BF16)
    for c in range(ts // SGU_CHUNK):
        rows = slice(c * SGU_CHUNK, (c + 1) * SGU_CHUNK)
        vh = jnp.concatenate([jnp.where(head == h, vn[rows, :], zero) for h in range(SGU_HEADS)],
                             axis=0)
        zc = jnp.dot(sw_ref[...], vh, preferred_element_type=F32) + sb_ref[...]
        y_c = px_ref[rows, c_u:c_u + W_SGU].astype(F32) * zc
        ycat_scr[rows, W_LRU + W_POOL:W_LRU + W_POOL + W_SGU] = y_c.astype(BF16)

    def gated(ref):
        return (ref[:, c_cv:c_cv + W_CONV].astype(F32)
                * _sigmoid(ref[:, c_cg:c_cg + W_CONV].astype(F32)))

    yext_scr[0:HALO, :] = jnp.where(has_prev, gated(prev_ref), 0.0)
    yext_scr[HALO:HALO + ts, :] = gated(px_ref)
    yext_scr[HALO + ts:, :] = jnp.where(has_next, gated(next_ref), 0.0)
    n_shift = ts + 2 * HALO - SUBLANES
    for r in range(1, SUBLANES):
        shift_scr[r - 1] = yext_scr[r:r + n_shift, :]
    for blk in range(ts // CONV_ROWS):
        r0 = blk * CONV_ROWS
        halves = []
        for c0 in range(0, W_CONV, CONV_LANES):
            lanes = slice(c0, c0 + CONV_LANES)
            acc = jnp.broadcast_to(db_ref[:, lanes], (CONV_ROWS, CONV_LANES))
            for k in range(CONF_CONV_W):
                off = HALO + k - CONF_PAD
                src = r0 + (off // SUBLANES) * SUBLANES
                if off % SUBLANES:
                    tap = shift_scr[off % SUBLANES - 1, src:src + CONV_ROWS, lanes]
                else:
                    tap = yext_scr[src:src + CONV_ROWS, lanes]
                acc = acc + dw_ref[k:k + 1, lanes] * tap
            halves.append(acc)
        acc = jnp.concatenate(halves, axis=1)
        y_d = _silu(_layer_norm(acc, dlg_ref[...], dlb_ref[...]))
        ycat_scr[r0:r0 + CONV_ROWS, W_LRU + W_POOL + W_SGU:] = y_d.astype(BF16)

    x = x_ref[...]
    if add_pos:
        x = x + _pos_rows(e_ref, tile, ts)
    o_ref[...] = x + gt_ref[...] * jnp.dot(ycat_scr[...], wo_ref[...], preferred_element_type=F32)


def _mix(x, pos_e, px, lru, h0b, mod5, icnt, pool_band, pool_bd, pool_scale, sgu_ln_g, sgu_ln_b,
         sgu_ws, sgu_bias, conv_d_w, conv_d_b, conv_ln_g, conv_ln_b, w_out, layer, ctx_row, ts,
         add_pos):
    b, s, d = x.shape
    n_t = s // ts
    nh = ts // HALO
    lasth = s // HALO - 1
    kernel = functools.partial(_mix_kernel, ts=ts, add_pos=add_pos)

    def rev(bb, i):
        return (bb, n_t - 1 - i, 0)

    def icnt_map(bb, i):
        tile = n_t - 1 - i
        return (jnp.where(tile == 0, 0, jnp.where(tile == n_t - 1, 2, 1)), 0)

    return pl.pallas_call(
        kernel,
        out_shape=(
            jax.ShapeDtypeStruct((b, s, d), F32),
            jax.ShapeDtypeStruct((b, 1, W_LRU), F32),
        ),
        grid=(b, n_t),
        in_specs=[
            pl.BlockSpec((None, ts, d), rev),
            _pos_spec(),
            pl.BlockSpec((None, ts, D_MIX_IN), rev),
            pl.BlockSpec((None, HALO, D_MIX_IN),
                         lambda bb, i: (bb, jnp.maximum((n_t - 1 - i) * nh - 1, 0), 0)),
            pl.BlockSpec((None, HALO, D_MIX_IN),
                         lambda bb, i: (bb, jnp.minimum((n_t - i) * nh, lasth), 0)),
            pl.BlockSpec((None, ts, 3 * W_LRU), rev),
            pl.BlockSpec((None, 1, W_LRU), lambda bb, i: (bb, 0, 0)),
            _mod_spec(layer, 2, ctx_row),
            pl.BlockSpec((ts, W_POOL), icnt_map),
            pl.BlockSpec(pool_band.shape, lambda bb, i: (0, 0, 0)),
            _layer_spec((W_POOL, W_POOL), layer),
            _layer_spec((1, W_POOL), layer),
            _layer_spec((1, W_SGU), layer),
            _layer_spec((1, W_SGU), layer),
            _layer_spec((SGU_CHUNK, SGU_HEADS * SGU_CHUNK), layer),
            _layer_spec((SGU_CHUNK, W_SGU), layer),
            _layer_spec((CONF_CONV_W, W_CONV), layer),
            _layer_spec((1, W_CONV), layer),
            _layer_spec((1, W_CONV), layer),
            _layer_spec((1, W_CONV), layer),
            _layer_spec((d, d), 0),
        ],
        out_specs=(
            pl.BlockSpec((None, ts, d), rev),
            pl.BlockSpec((None, 1, W_LRU), lambda bb, i: (bb, 0, 0)),
        ),
        scratch_shapes=[
            pltpu.VMEM((ts, W_LRU), F32),
            pltpu.VMEM((1, W_LRU), F32),
            pltpu.VMEM((ts + 2 * HALO, W_POOL), BF16),
            pltpu.VMEM((ts + 2 * HALO, W_CONV), F32),
            pltpu.VMEM((SUBLANES - 1, ts + 2 * HALO - SUBLANES, W_CONV), F32),
            pltpu.VMEM((ts, d), BF16),
        ],
        compiler_params=_params("arbitrary", "arbitrary"),
        name="mix",
    )(x, pos_e, px, px, px, lru, h0b, mod5, icnt, pool_band, pool_bd, pool_scale, sgu_ln_g,
      sgu_ln_b, sgu_ws, sgu_bias, conv_d_w, conv_d_b, conv_ln_g, conv_ln_b, w_out)


def _bwd_state_kernel(lru_ref, o_ref, hb_scr, *, rows):
    zero = jnp.zeros((1, W_LRU), F32)
    o_ref[...] = _scan_tile(lru_ref.at[:, W_LRU:2 * W_LRU], lru_ref.at[:, 2 * W_LRU:], hb_scr, zero,
                            rows, reverse=True)


def _bwd_state(lru):
    b, s, _ = lru.shape
    return pl.pallas_call(
        functools.partial(_bwd_state_kernel, rows=s),
        out_shape=jax.ShapeDtypeStruct((b, 1, W_LRU), F32),
        grid=(b,),
        in_specs=[pl.BlockSpec((None, s, 3 * W_LRU), lambda bb: (bb, 0, 0))],
        out_specs=pl.BlockSpec((None, 1, W_LRU), lambda bb: (bb, 0, 0)),
        scratch_shapes=[pltpu.VMEM((s, W_LRU), F32)],
        compiler_params=_params("arbitrary"),
        name="bwd_state",
    )(lru)


def _mlp_kernel(x_ref, g_ref, sh_ref, sc_ref, gt_ref, w1_ref, w2_ref, gf_ref, *rest, final, chunks):
    n_cast = len(chunks)
    cast_in, o_ref, cast_out, acc_scr = rest[:n_cast], rest[n_cast], rest[n_cast + 1:-1], rest[-1]
    _cast_chunks(cast_in, cast_out, chunks)
    x = x_ref[...]
    h = _rms_mod(x, g_ref[...], sh_ref[...], sc_ref[...]).astype(BF16)
    for j in range(D_FF // FF_CHUNK):
        cols = slice(j * FF_CHUNK, (j + 1) * FF_CHUNK)
        u = jnp.dot(h, w1_ref[:, cols], preferred_element_type=F32)
        u = jnp.square(jnp.maximum(u, 0.0)).astype(BF16)
        part = jnp.dot(u, w2_ref[cols, :], preferred_element_type=F32)
        if j == 0:
            acc_scr[...] = part
        else:
            acc_scr[...] += part
    y = x + gt_ref[...] * acc_scr[...]
    if final:
        ms = jnp.mean(y * y, axis=-1, keepdims=True)
        y = y * lax.rsqrt(ms + EPS) * gf_ref[...]
    o_ref[...] = y


def _cast_plan(cast, src_layer, n_b, n_i):
    n_steps = n_b * n_i
    in_specs, out_specs, out_shapes, chunks = [], [], [], []
    for w, axis in cast:
        shape = w.shape[1:]
        assert shape[axis] % CAST_CHUNK == 0
        units = shape[axis] // CAST_CHUNK
        n_chunks = max(k for k in range(1, units + 1) if units % k == 0 and k <= n_steps)
        block = tuple(n // n_chunks if a == axis else n for a, n in enumerate(shape))

        def index(lead, n_chunks=n_chunks, axis=axis):
            def index_map(bb, i):
                c = jnp.minimum(bb * n_i + i, n_chunks - 1)
                return (lead,) + tuple(c if a == axis else 0 for a in range(2))
            return index_map

        in_specs.append(pl.BlockSpec((None,) + block, index(src_layer)))
        out_specs.append(pl.BlockSpec((None,) + block, index(0)))
        out_shapes.append(jax.ShapeDtypeStruct((1,) + shape, BF16))
        chunks.append(n_chunks)
    return in_specs, out_specs, out_shapes, tuple(chunks)


def _cast_chunks(cast_in, cast_out, chunks):
    step = pl.program_id(0) * pl.num_programs(1) + pl.program_id(1)
    for src, dst, n_chunks in zip(cast_in, cast_out, chunks):
        @pl.when(step < n_chunks)
        def _(src=src, dst=dst):
            dst[...] = src[...].astype(BF16)


def _mlp(x, mod5, g2, w1, w2, g_final, layer, ctx_row, tm, final, cast_next=()):
    b, s, d = x.shape
    n_i = s // tm
    cast_specs, cast_out_specs, cast_out_shapes, chunks = _cast_plan(cast_next, layer + 1, b, n_i)
    out = pl.pallas_call(
        functools.partial(_mlp_kernel, final=final, chunks=chunks),
        out_shape=[jax.ShapeDtypeStruct(x.shape, F32)] + cast_out_shapes,
        grid=(b, n_i),
        in_specs=[
            pl.BlockSpec((None, tm, d), lambda bb, i: (bb, i, 0)),
            _layer_spec((1, d), layer),
            _mod_spec(layer, 3, ctx_row),
            _mod_spec(layer, 4, ctx_row),
            _mod_spec(layer, 5, ctx_row),
            _layer_spec((d, D_FF), 0, single_buffer=True),
            _layer_spec((D_FF, d), 0, single_buffer=True),
            pl.BlockSpec((1, d), lambda bb, i: (0, 0)),
        ] + cast_specs,
        out_specs=[pl.BlockSpec((None, tm, d), lambda bb, i: (bb, i, 0))] + cast_out_specs,
        scratch_shapes=[pltpu.VMEM((tm, d), F32)],
        compiler_params=_params("arbitrary", "arbitrary"),
        name="mlp",
    )(x, g2, mod5, mod5, mod5, w1, w2, g_final, *[w for w, _ in cast_next])
    return out[0], tuple(out[1:])


def _block_diag(w):
    *lead, h, n, _ = w.shape
    eye = jnp.eye(h, dtype=w.dtype)
    out = jnp.einsum('...hij,hg->...higj', w, eye)
    return out.reshape(*lead, h * n, h * n)


def _edge_tiles(table, ts):
    n_t = table.shape[0] // ts
    mid = min(1, n_t - 1)
    return jnp.concatenate([table[:ts], table[mid * ts:(mid + 1) * ts], table[-ts:]], axis=0)


def _tile_rows(s, want):
    t = min(s, want)
    while s % t:
        t //= 2
    return t


def kernel(x, c, ctx, c_ctx, w_mod, b_mod, g_norm1, g_norm2, w_in, conv_a_w, conv_a_b, lru_wa, lru_ba, lru_wx, lru_bx, lru_lambda, pool_w, pool_scale, sgu_ln_g, sgu_ln_b, sgu_w, sgu_b, conv_d_w, conv_d_b, conv_ln_g, conv_ln_b, w_out, w_mlp1, w_mlp2, g_final):
    depth = w_in.shape[0]
    bsz, seq, d = x.shape
    ctx_len = ctx.shape[1]
    assert d == D_MODEL and bsz < MOD_ROWS
    assert seq % SGU_CHUNK == 0 and ctx_len % SGU_CHUNK == 0
    ctx_row = bsz

    big = ((w_mlp1, 1), (w_mlp2, 0), (w_in, 1), (w_out, 1))
    w_in_l, w_out_l = (w[0:1].astype(BF16) for w in (w_in, w_out))
    w1_l = w2_l = None
    wa_bd = _block_diag(lru_wa)
    wx_bd = _block_diag(lru_wx)
    wg = jnp.concatenate([wa_bd[:, 0], wx_bd[:, 0], wa_bd[:, 1], wx_bd[:, 1]], axis=-1).astype(BF16)
    bg = jnp.concatenate([lru_ba[:, 0], lru_bx[:, 0], lru_ba[:, 1], lru_bx[:, 1]], axis=-1)[:, None, :]
    lam = lru_lambda.reshape(depth, 1, 2 * W_LRU)
    pool_bd = _block_diag(pool_w).astype(BF16)
    sgu_ws = jnp.swapaxes(sgu_w, 1, 2).reshape(depth, SGU_CHUNK, SGU_HEADS * SGU_CHUNK).astype(BF16)
    sgu_bias = jnp.repeat(jnp.swapaxes(sgu_b, 1, 2), SGU_HEAD_DIM, axis=2)
    row = lambda v: v[:, None, :]

    cs = jnp.zeros((MOD_ROWS, d), F32).at[:bsz].set(c).at[ctx_row].set(c_ctx)
    mod = _modulation(cs, w_mod, b_mod)
    mod5 = mod.reshape(depth, MOD_ROWS, N_MOD, 1, d)

    ts = _tile_rows(seq, 1024)
    ts_c = _tile_rows(ctx_len, 512)
    tm = _tile_rows(seq, 1024)
    tm_c = _tile_rows(bsz * ctx_len, 1024)
    assert ts % GRID_W == 0 and ts % CONV_ROWS == 0 and ts_c % CONV_ROWS == 0
    pos_e = _pos_table()
    xc = ctx
    zeros_h = jnp.zeros((bsz, 1, W_LRU), F32)
    g_fin = g_final.reshape(1, d)
    mix_w = (_pool_bands(), pool_bd, row(pool_scale), row(sgu_ln_g), row(sgu_ln_b), sgu_ws,
             sgu_bias, conv_d_w, row(conv_d_b), row(conv_ln_g), row(conv_ln_b))
    icnt = _edge_tiles(_pool_icnt(seq, ts), ts)
    icnt_c = _edge_tiles(_pool_icnt(ctx_len, ts_c), ts_c)

    for l in range(depth):
        last = l == depth - 1
        first = l == 0
        pc, lru_c, hcf_last, _ = _proj_lru(xc, pos_e, mod5, row(g_norm1), w_in_l, conv_a_w,
                                           row(conv_a_b), wg, bg, lam, zeros_h, l, ctx_row, ts_c,
                                           False)
        px, lru, _, cast0 = _proj_lru(x, pos_e, mod5, row(g_norm1), w_in_l, conv_a_w, row(conv_a_b),
                                      wg, bg, lam, hcf_last, l, None, ts, first,
                                      cast=big[:2] if first else ())
        if first:
            w1_l, w2_l = cast0
        if last:
            hcb_first = _bwd_state(lru_c)
        else:
            xc, hcb_first = _mix(xc, pos_e, pc, lru_c, zeros_h, mod5, icnt_c, *mix_w, w_out_l, l,
                                 ctx_row, ts_c, False)
            xc, _ = _mlp(xc.reshape(1, -1, d), mod5, row(g_norm2), w1_l, w2_l, g_fin, l, ctx_row,
                         tm_c, False)
            xc = xc.reshape(bsz, ctx_len, d)
        x, _ = _mix(x, pos_e, px, lru, hcb_first, mod5, icnt, *mix_w, w_out_l, l, None, ts, first)
        x, nxt = _mlp(x, mod5, row(g_norm2), w1_l, w2_l, g_fin, l, None, tm, last,
                      cast_next=() if last else big)
        if not last:
            w1_l, w2_l, w_in_l, w_out_l = nxt
    return x
```

```python
import functools
import math

import jax
import jax.numpy as jnp
from jax import lax
from jax.experimental import pallas as pl
from jax.experimental.pallas import tpu as pltpu

F32 = jnp.float32
BF16 = jnp.bfloat16

D_MODEL = 1024
GRID_W = 64
W_LRU = 256
W_POOL = 256
W_SGU = 256
W_CONV = 256
LRU_HEADS = 4
LRU_CONV_W = 4
LRU_PAD_L = 2
LRU_C = 8.0
POOL_WINDOWS = (2, 4, 8, 16)
POOL_GROUP = W_POOL // len(POOL_WINDOWS)
SGU_CHUNK = 128
SGU_HEADS = 4
SGU_HEAD_DIM = W_SGU // SGU_HEADS
CONF_CONV_W = 31
CONF_PAD = 15
D_FF = 4 * D_MODEL
D_IN = 2 * W_LRU + W_POOL + 2 * W_SGU + 2 * W_CONV
D_MIX_IN = D_IN - W_LRU
N_MOD = 6
EPS = 1e-6
POS_BASE = 10000.0

SUBLANES = 8
HALO = 16
VMEM_LIMIT = 56 * 1024 * 1024
MOD_ROWS = 8
FF_CHUNK = 512
CONV_ROWS = 128
CONV_LANES = 128
POOL_ROWS = 128
INPROJ_CHUNKS = 2
CAST_CHUNK = 256


def _params(*sem):
    return pltpu.CompilerParams(dimension_semantics=sem, vmem_limit_bytes=VMEM_LIMIT)


def _sigmoid(x):
    return 0.5 * jnp.tanh(0.5 * x) + 0.5


def _silu(x):
    return x * _sigmoid(x)


def _gelu_tanh(x):
    return 0.5 * x * (1.0 + jnp.tanh(0.7978845608028654 * (x + 0.044715 * (x * x * x))))


def _rms_mod(x, g, shift, scale):
    ms = jnp.mean(x * x, axis=-1, keepdims=True)
    y = x * lax.rsqrt(ms + EPS) * g
    return y * (1.0 + scale) + shift


def _layer_norm(x, g, b):
    mu = jnp.mean(x, axis=-1, keepdims=True)
    xc = x - mu
    var = jnp.mean(xc * xc, axis=-1, keepdims=True)
    return xc * lax.rsqrt(var + EPS) * g + b


def _mod_kernel(c_ref, w_ref, b_ref, o_ref):
    s = _silu(c_ref[...]).astype(BF16)
    o_ref[...] = jnp.dot(s, w_ref[...].astype(BF16), preferred_element_type=F32) + b_ref[...]


def _modulation(cs, w_mod, b_mod):
    depth, _, n = w_mod.shape
    tn = 1536
    return pl.pallas_call(
        _mod_kernel,
        out_shape=jax.ShapeDtypeStruct((depth, MOD_ROWS, n), F32),
        grid=(depth, n // tn),
        in_specs=[
            pl.BlockSpec((MOD_ROWS, D_MODEL), lambda l, j: (0, 0)),
            pl.BlockSpec((None, D_MODEL, tn), lambda l, j: (l, 0, j)),
            pl.BlockSpec((None, 1, tn), lambda l, j: (l, 0, j)),
        ],
        out_specs=pl.BlockSpec((None, MOD_ROWS, tn), lambda l, j: (l, 0, j)),
        compiler_params=_params("arbitrary", "arbitrary"),
        name="modulation",
    )(cs, w_mod, b_mod.reshape(depth, 1, n))


def _pos_table_kernel(o_ref):
    quarter = D_MODEL // 4
    pos = lax.broadcasted_iota(jnp.int32, (GRID_W, quarter), 0).astype(F32)
    j = lax.broadcasted_iota(jnp.int32, (GRID_W, quarter), 1).astype(F32)
    omega = 1.0 / jnp.exp(j * (math.log(POS_BASE) / quarter))
    ang = pos * omega
    o_ref[:, :quarter] = jnp.sin(ang)
    o_ref[:, quarter:] = jnp.cos(ang)


def _pos_table():
    return pl.pallas_call(
        _pos_table_kernel,
        out_shape=jax.ShapeDtypeStruct((GRID_W, D_MODEL // 2), F32),
        name="pos_table",
    )()


def _pos_rows(e_ref, tile, ts):
    reps = ts // GRID_W
    col = jnp.concatenate([e_ref[...]] * reps, axis=0)
    r0 = tile * reps
    rows = [jnp.broadcast_to(e_ref[pl.ds(r0 + k, 1), :], (GRID_W, D_MODEL // 2)) for k in range(reps)]
    return jnp.concatenate([jnp.concatenate(rows, axis=0), col], axis=1)


def _mod_spec(layer, k, ctx_row):
    if ctx_row is None:
        return pl.BlockSpec((None, None, None, 1, D_MODEL), lambda b, i: (layer, b, k, 0, 0))
    return pl.BlockSpec((None, None, None, 1, D_MODEL), lambda b, i: (layer, ctx_row, k, 0, 0))


def _layer_spec(shape, layer, single_buffer=False):
    zeros = (0,) * len(shape)
    kwargs = dict(pipeline_mode=pl.Buffered(1)) if single_buffer else {}
    return pl.BlockSpec((None,) + tuple(shape), lambda b, i: (layer,) + zeros, **kwargs)


def _pos_spec():
    return pl.BlockSpec((GRID_W, D_MODEL // 2), lambda b, i: (0, 0))


def _scan_tile(a_ref, b_ref, out_ref, carry0, n_rows, reverse, unroll=False):
    width = a_ref.shape[-1]
    n_groups = n_rows // SUBLANES
    row = lax.broadcasted_iota(jnp.int32, (SUBLANES, width), 0)

    def group(gi, carry):
        g = (n_groups - 1 - gi) if reverse else gi
        start = g * SUBLANES if unroll else pl.multiple_of(g * SUBLANES, SUBLANES)
        a = a_ref[pl.ds(start, SUBLANES), :]
        h = b_ref[pl.ds(start, SUBLANES), :]
        for s in (1, 2, 4):
            if reverse:
                keep = row < SUBLANES - s
                shift = SUBLANES - s
            else:
                keep = row >= s
                shift = s
            h_sh = jnp.where(keep, pltpu.roll(h, shift, 0), 0.0)
            a_sh = jnp.where(keep, pltpu.roll(a, shift, 0), 1.0)
            h = h + a * h_sh
            a = a * a_sh
        h = h + a * carry
        out_ref[pl.ds(start, SUBLANES), :] = h
        edge = h[0:1, :] if reverse else h[SUBLANES - 1:SUBLANES, :]
        return jnp.broadcast_to(edge, (SUBLANES, width))

    carry = jnp.broadcast_to(carry0, (SUBLANES, width))
    if unroll:
        for gi in range(n_groups):
            carry = group(gi, carry)
        return carry[0:1, :]
    return lax.fori_loop(0, n_groups, group, carry, unroll=2)[0:1, :]


def _proj_lru_kernel(x_ref, xn_ref, e_ref, g_ref, sh_ref, sc_ref, w_ref,
                     cw_ref, cb_ref, wg_ref, bg_ref, lam_ref, h0_ref, *rest, ts, add_pos, chunks):
    n_cast = len(chunks)
    cast_in, rest = rest[:n_cast], rest[n_cast:]
    px_ref, lru_ref, hlast_ref = rest[:3]
    cast_out, (ext_scr, tail_scr, a_scr, b_scr, carry_scr) = rest[3:3 + n_cast], rest[3 + n_cast:]
    _cast_chunks(cast_in, cast_out, chunks)
    i = pl.program_id(1)
    n_t = pl.num_programs(1)
    reps = ts // GRID_W

    @pl.when(i == 0)
    def _():
        carry_scr[...] = h0_ref[...]
        tail_scr[...] = jnp.zeros(tail_scr.shape, F32)

    ext_scr[0:HALO, :] = tail_scr[...]

    x = x_ref[...]
    if add_pos:
        x = x + _pos_rows(e_ref, i, ts)
    rows = ts // INPROJ_CHUNKS
    for c in range(INPROJ_CHUNKS):
        sl = slice(c * rows, (c + 1) * rows)
        h = _rms_mod(x[sl, :], g_ref[...], sh_ref[...], sc_ref[...])
        res = jnp.dot(h.astype(BF16), w_ref[...], preferred_element_type=F32)
        px_ref[sl, :] = res[:, W_LRU:].astype(px_ref.dtype)
        ext_scr[HALO + c * rows:HALO + (c + 1) * rows, :] = res[:, :W_LRU]

    xn = xn_ref[...]
    if add_pos:
        half = D_MODEL // 2
        pos_n = jnp.concatenate(
            [jnp.broadcast_to(e_ref[pl.ds(jnp.minimum((i + 1) * reps, GRID_W - 1), 1), :],
                              (SUBLANES, half)),
             e_ref[0:SUBLANES, :]], axis=1)
        xn = xn + pos_n
    hn = _rms_mod(xn, g_ref[...], sh_ref[...], sc_ref[...])
    pn = jnp.dot(hn.astype(BF16), w_ref[:, :W_LRU], preferred_element_type=F32)
    ext_scr[HALO + ts:HALO + ts + SUBLANES, :] = jnp.where(i < n_t - 1, pn, 0.0)
    tail_scr[...] = ext_scr[ts:ts + HALO, :]

    cx = jnp.broadcast_to(cb_ref[...], (ts, W_LRU))
    for k in range(LRU_CONV_W):
        off = HALO + k - LRU_PAD_L
        cx = cx + cw_ref[k:k + 1, :] * ext_scr[off:off + ts, :]

    gates = jnp.dot(cx.astype(BF16), wg_ref[...], preferred_element_type=F32) + bg_ref[...]
    neg_lam = -lam_ref[...]
    sp = jnp.maximum(neg_lam, 0.0) + jnp.log1p(jnp.exp(-jnp.abs(neg_lam)))
    lam_c = -LRU_C * sp
    for d in range(2):
        r = _sigmoid(gates[:, (2 * d) * W_LRU:(2 * d + 1) * W_LRU])
        ig = _sigmoid(gates[:, (2 * d + 1) * W_LRU:(2 * d + 2) * W_LRU])
        log_a = lam_c[:, d * W_LRU:(d + 1) * W_LRU] * r
        a = jnp.exp(log_a)
        t = jnp.tanh(log_a)
        z = -2.0 * t / (1.0 - t)
        bmul = jnp.where(z > 0.0, z * lax.rsqrt(z), 0.0)
        bb = bmul * (ig * cx)
        if d == 0:
            a_scr[...] = a
            b_scr[...] = bb
        else:
            lru_ref[:, W_LRU:2 * W_LRU] = a
            lru_ref[:, 2 * W_LRU:] = bb

    carry = _scan_tile(a_scr, b_scr, lru_ref.at[:, :W_LRU], carry_scr[...], ts, reverse=False,
                       unroll=True)
    carry_scr[...] = carry
    hlast_ref[...] = carry


def _proj_lru(x, pos_e, mod5, g1, w_in, conv_w, conv_b, wg, bg, lam, h0, layer, ctx_row, ts,
              add_pos, cast=()):
    b, s, d = x.shape
    n8 = ts // SUBLANES
    last8 = s // SUBLANES - 1
    tile = lambda bb, i: (bb, i, 0)
    cast_specs, cast_out_specs, cast_out_shapes, chunks = _cast_plan(cast, layer, b, s // ts)
    out = pl.pallas_call(
        functools.partial(_proj_lru_kernel, ts=ts, add_pos=add_pos, chunks=chunks),
        out_shape=[
            jax.ShapeDtypeStruct((b, s, D_MIX_IN), BF16),
            jax.ShapeDtypeStruct((b, s, 3 * W_LRU), F32),
            jax.ShapeDtypeStruct((b, 1, W_LRU), F32),
        ] + cast_out_shapes,
        grid=(b, s // ts),
        in_specs=[
            pl.BlockSpec((None, ts, d), tile),
            pl.BlockSpec((None, SUBLANES, d), lambda bb, i: (bb, jnp.minimum((i + 1) * n8, last8), 0)),
            _pos_spec(),
            _layer_spec((1, d), layer),
            _mod_spec(layer, 0, ctx_row),
            _mod_spec(layer, 1, ctx_row),
            _layer_spec((d, D_IN), 0),
            _layer_spec((LRU_CONV_W, W_LRU), layer),
            _layer_spec((1, W_LRU), layer),
            _layer_spec((W_LRU, 4 * W_LRU), layer),
            _layer_spec((1, 4 * W_LRU), layer),
            _layer_spec((1, 2 * W_LRU), layer),
            pl.BlockSpec((None, 1, W_LRU), lambda bb, i: (bb, 0, 0)),
        ] + cast_specs,
        out_specs=[
            pl.BlockSpec((None, ts, D_MIX_IN), tile),
            pl.BlockSpec((None, ts, 3 * W_LRU), tile),
            pl.BlockSpec((None, 1, W_LRU), lambda bb, i: (bb, 0, 0)),
        ] + cast_out_specs,
        scratch_shapes=[
            pltpu.VMEM((ts + 2 * HALO, W_LRU), F32),
            pltpu.VMEM((HALO, W_LRU), F32),
            pltpu.VMEM((ts, W_LRU), F32),
            pltpu.VMEM((ts, W_LRU), F32),
            pltpu.VMEM((1, W_LRU), F32),
        ],
        compiler_params=_params("arbitrary", "arbitrary"),
        name="proj_lru",
    )(x, x, pos_e, g1, mod5, mod5, w_in, conv_w, conv_b, wg, bg, lam, h0, *[w for w, _ in cast])
    return out[0], out[1], out[2], tuple(out[3:])


def _pool_windows(shape):
    grp = lax.broadcasted_iota(jnp.int32, shape, 1) // POOL_GROUP
    win = jnp.full(shape, POOL_WINDOWS[-1], jnp.int32)
    for g in range(len(POOL_WINDOWS) - 2, -1, -1):
        win = jnp.where(grp == g, POOL_WINDOWS[g], win)
    return win


def _pool_icnt_kernel(o_ref, *, ts, seq):
    win = _pool_windows((ts, W_POOL))
    lo = win // 2
    hi = win - lo - 1
    tpos = pl.program_id(0) * ts + lax.broadcasted_iota(jnp.int32, (ts, W_POOL), 0)
    cnt = (jnp.minimum(tpos + hi + 1, seq) - jnp.maximum(tpos - lo, 0)).astype(F32)
    o_ref[...] = 1.0 / cnt


def _pool_icnt(seq, ts):
    return pl.pallas_call(
        functools.partial(_pool_icnt_kernel, ts=ts, seq=seq),
        out_shape=jax.ShapeDtypeStruct((seq, W_POOL), F32),
        grid=(seq // ts,),
        out_specs=pl.BlockSpec((ts, W_POOL), lambda i: (i, 0)),
        compiler_params=_params("arbitrary"),
        name="pool_icnt",
    )()


def _pool_bands():
    t = jnp.arange(POOL_ROWS)[:, None]
    tok = jnp.arange(POOL_ROWS + 2 * HALO)[None, :] - HALO
    bands = []
    for w in POOL_WINDOWS:
        lo = w // 2
        hi = w - lo - 1
        bands.append((tok >= t - lo) & (tok <= t + hi))
    return jnp.stack(bands).astype(BF16)


def _mix_kernel(x_ref, e_ref, px_ref, prev_ref, next_ref, lru_ref, h0_ref, gt_ref,
                icnt_ref, pb_ref, pw_ref, ps_ref, slg_ref, slb_ref, sw_ref, sb_ref,
                dw_ref, db_ref, dlg_ref, dlb_ref, wo_ref,
                o_ref, hfirst_ref,
                hb_scr, carry_scr, zext_scr, yext_scr, shift_scr, ycat_scr, *, ts, add_pos):
    i = pl.program_id(1)
    n_t = pl.num_programs(1)
    tile = n_t - 1 - i
    c_gate, c_pool, c_u, c_v, c_cv, c_cg = (W_LRU * k for k in range(6))
    hf_ref = lru_ref.at[:, :W_LRU]
    ab_ref = lru_ref.at[:, W_LRU:]
    has_prev = tile > 0
    has_next = tile < n_t - 1

    @pl.when(i == 0)
    def _():
        carry_scr[...] = h0_ref[...]

    carry = _scan_tile(ab_ref.at[:, :W_LRU], ab_ref.at[:, W_LRU:], hb_scr, carry_scr[...], ts,
                       reverse=True, unroll=True)
    carry_scr[...] = carry
    hfirst_ref[...] = carry
    y_a = (hf_ref[...] + hb_scr[...]) * _gelu_tanh(px_ref[:, c_gate:c_gate + W_LRU].astype(F32))
    ycat_scr[:, :W_LRU] = y_a.astype(BF16)

    zext_scr[0:HALO, :] = jnp.where(has_prev, prev_ref[:, c_pool:c_pool + W_POOL], 0.0)
    zext_scr[HALO:HALO + ts, :] = px_ref[:, c_pool:c_pool + W_POOL]
    zext_scr[HALO + ts:, :] = jnp.where(has_next, next_ref[:, c_pool:c_pool + W_POOL], 0.0)
    grp = lax.broadcasted_iota(jnp.int32, (POOL_ROWS, W_POOL), 1) // POOL_GROUP
    for blk in range(ts // POOL_ROWS):
        r0 = blk * POOL_ROWS
        zb = zext_scr[r0:r0 + POOL_ROWS + 2 * HALO, :]
        wsum = jnp.dot(pb_ref[len(POOL_WINDOWS) - 1], zb, preferred_element_type=F32)
        for g in range(len(POOL_WINDOWS) - 2, -1, -1):
            wsum = jnp.where(grp == g, jnp.dot(pb_ref[g], zb, preferred_element_type=F32), wsum)
        rows = slice(r0, r0 + POOL_ROWS)
        pooled = wsum * icnt_ref[rows, :] - px_ref[rows, c_pool:c_pool + W_POOL].astype(F32)
        y_b = jnp.dot(pooled.astype(BF16), pw_ref[...], preferred_element_type=F32) * ps_ref[...]
        ycat_scr[rows, W_LRU:W_LRU + W_POOL] = y_b.astype(BF16)

    vn = _layer_norm(px_ref[:, c_v:c_v + W_SGU].astype(F32), slg_ref[...], slb_ref[...]).astype(BF16)
    head = lax.broadcasted_iota(jnp.int32, (SGU_CHUNK, W_SGU), 1) // SGU_HEAD_DIM
    zero = jnp.zeros((SGU_CHUNK, W_SGU), BF16)
    for c in range(ts // SGU_CHUNK):
        rows = slice(c * SGU_CHUNK, (c + 1) * SGU_CHUNK)
        vh = jnp.concatenate([jnp.where(head == h, vn[rows, :], zero) for h in range(SGU_HEADS)],
                             axis=0)
        zc = jnp.dot(sw_ref[...], vh, preferred_element_type=F32) + sb_ref[...]
        y_c = px_ref[rows, c_u:c_u + W_SGU].astype(F32) * zc
        ycat_scr[rows, W_LRU + W_POOL:W_LRU + W_POOL + W_SGU] = y_c.astype(BF16)

    def gated(ref):
        return (ref[:, c_cv:c_cv + W_CONV].astype(F32)
                * _sigmoid(ref[:, c_cg:c_cg + W_CONV].astype(F32)))

    yext_scr[0:HALO, :] = jnp.where(has_prev, gated(prev_ref), 0.0)
    yext_scr[HALO:HALO + ts, :] = gated(px_ref)
    yext_scr[HALO + ts:, :] = jnp.where(has_next, gated(next_ref), 0.0)
    n_shift = ts + 2 * HALO - SUBLANES
    for r in range(1, SUBLANES):
        shift_scr[r - 1] = yext_scr[r:r + n_shift, :]
    for blk in range(ts // CONV_ROWS):
        r0 = blk * CONV_ROWS
        halves = []
        for c0 in range(0, W_CONV, CONV_LANES):
            lanes = slice(c0, c0 + CONV_LANES)
            acc = jnp.broadcast_to(db_ref[:, lanes], (CONV_ROWS, CONV_LANES))
            for k in range(CONF_CONV_W):
                off = HALO + k - CONF_PAD
                src = r0 + (off // SUBLANES) * SUBLANES
                if off % SUBLANES:
                    tap = shift_scr[off % SUBLANES - 1, src:src + CONV_ROWS, lanes]
                else:
                    tap = yext_scr[src:src + CONV_ROWS, lanes]
                acc = acc + dw_ref[k:k + 1, lanes] * tap
            halves.append(acc)
        acc = jnp.concatenate(halves, axis=1)
        y_d = _silu(_layer_norm(acc, dlg_ref[...], dlb_ref[...]))
        ycat_scr[r0:r0 + CONV_ROWS, W_LRU + W_POOL + W_SGU:] = y_d.astype(BF16)

    x = x_ref[...]
    if add_pos:
        x = x + _pos_rows(e_ref, tile, ts)
    o_ref[...] = x + gt_ref[...] * jnp.dot(ycat_scr[...], wo_ref[...], preferred_element_type=F32)


def _mix(x, pos_e, px, lru, h0b, mod5, icnt, pool_band, pool_bd, pool_scale, sgu_ln_g, sgu_ln_b,
         sgu_ws, sgu_bias, conv_d_w, conv_d_b, conv_ln_g, conv_ln_b, w_out, layer, ctx_row, ts,
         add_pos):
    b, s, d = x.shape
    n_t = s // ts
    nh = ts // HALO
    lasth = s // HALO - 1
    kernel = functools.partial(_mix_kernel, ts=ts, add_pos=add_pos)

    def rev(bb, i):
        return (bb, n_t - 1 - i, 0)

    def icnt_map(bb, i):
        tile = n_t - 1 - i
        return (jnp.where(tile == 0, 0, jnp.where(tile == n_t - 1, 2, 1)), 0)

    return pl.pallas_call(
        kernel,
        out_shape=(
            jax.ShapeDtypeStruct((b, s, d), F32),
            jax.ShapeDtypeStruct((b, 1, W_LRU), F32),
        ),
        grid=(b, n_t),
        in_specs=[
            pl.BlockSpec((None, ts, d), rev),
            _pos_spec(),
            pl.BlockSpec((None, ts, D_MIX_IN), rev),
            pl.BlockSpec((None, HALO, D_MIX_IN),
                         lambda bb, i: (bb, jnp.maximum((n_t - 1 - i) * nh - 1, 0), 0)),
            pl.BlockSpec((None, HALO, D_MIX_IN),
                         lambda bb, i: (bb, jnp.minimum((n_t - i) * nh, lasth), 0)),
            pl.BlockSpec((None, ts, 3 * W_LRU), rev),
            pl.BlockSpec((None, 1, W_LRU), lambda bb, i: (bb, 0, 0)),
            _mod_spec(layer, 2, ctx_row),
            pl.BlockSpec((ts, W_POOL), icnt_map),
            pl.BlockSpec(pool_band.shape, lambda bb, i: (0, 0, 0)),
            _layer_spec((W_POOL, W_POOL), layer),
            _layer_spec((1, W_POOL), layer),
            _layer_spec((1, W_SGU), layer),
            _layer_spec((1, W_SGU), layer),
            _layer_spec((SGU_CHUNK, SGU_HEADS * SGU_CHUNK), layer),
            _layer_spec((SGU_CHUNK, W_SGU), layer),
            _layer_spec((CONF_CONV_W, W_CONV), layer),
            _layer_spec((1, W_CONV), layer),
            _layer_spec((1, W_CONV), layer),
            _layer_spec((1, W_CONV), layer),
            _layer_spec((d, d), 0),
        ],
        out_specs=(
            pl.BlockSpec((None, ts, d), rev),
            pl.BlockSpec((None, 1, W_LRU), lambda bb, i: (bb, 0, 0)),
        ),
        scratch_shapes=[
            pltpu.VMEM((ts, W_LRU), F32),
            pltpu.VMEM((1, W_LRU), F32),
            pltpu.VMEM((ts + 2 * HALO, W_POOL), BF16),
            pltpu.VMEM((ts + 2 * HALO, W_CONV), F32),
            pltpu.VMEM((SUBLANES - 1, ts + 2 * HALO - SUBLANES, W_CONV), F32),
            pltpu.VMEM((ts, d), BF16),
        ],
        compiler_params=_params("arbitrary", "arbitrary"),
        name="mix",
    )(x, pos_e, px, px, px, lru, h0b, mod5, icnt, pool_band, pool_bd, pool_scale, sgu_ln_g,
      sgu_ln_b, sgu_ws, sgu_bias, conv_d_w, conv_d_b, conv_ln_g, conv_ln_b, w_out)


def _bwd_state_kernel(lru_ref, o_ref, hb_scr, *, rows):
    zero = jnp.zeros((1, W_LRU), F32)
    o_ref[...] = _scan_tile(lru_ref.at[:, W_LRU:2 * W_LRU], lru_ref.at[:, 2 * W_LRU:], hb_scr, zero,
                            rows, reverse=True)


def _bwd_state(lru):
    b, s, _ = lru.shape
    return pl.pallas_call(
        functools.partial(_bwd_state_kernel, rows=s),
        out_shape=jax.ShapeDtypeStruct((b, 1, W_LRU), F32),
        grid=(b,),
        in_specs=[pl.BlockSpec((None, s, 3 * W_LRU), lambda bb: (bb, 0, 0))],
        out_specs=pl.BlockSpec((None, 1, W_LRU), lambda bb: (bb, 0, 0)),
        scratch_shapes=[pltpu.VMEM((s, W_LRU), F32)],
        compiler_params=_params("arbitrary"),
        name="bwd_state",
    )(lru)


def _mlp_kernel(x_ref, g_ref, sh_ref, sc_ref, gt_ref, w1_ref, w2_ref, gf_ref, *rest, final, chunks):
    n_cast = len(chunks)
    cast_in, o_ref, cast_out, acc_scr = rest[:n_cast], rest[n_cast], rest[n_cast + 1:-1], rest[-1]
    _cast_chunks(cast_in, cast_out, chunks)
    x = x_ref[...]
    h = _rms_mod(x, g_ref[...], sh_ref[...], sc_ref[...]).astype(BF16)
    for j in range(D_FF // FF_CHUNK):
        cols = slice(j * FF_CHUNK, (j + 1) * FF_CHUNK)
        u = jnp.dot(h, w1_ref[:, cols], preferred_element_type=F32)
        u = jnp.square(jnp.maximum(u, 0.0)).astype(BF16)
        part = jnp.dot(u, w2_ref[cols, :], preferred_element_type=F32)
        if j == 0:
            acc_scr[...] = part
        else:
            acc_scr[...] += part
    y = x + gt_ref[...] * acc_scr[...]
    if final:
        ms = jnp.mean(y * y, axis=-1, keepdims=True)
        y = y * lax.rsqrt(ms + EPS) * gf_ref[...]
    o_ref[...] = y


def _cast_plan(cast, src_layer, n_b, n_i):
    n_steps = n_b * n_i
    in_specs, out_specs, out_shapes, chunks = [], [], [], []
    for w, axis in cast:
        shape = w.shape[1:]
        assert shape[axis] % CAST_CHUNK == 0
        units = shape[axis] // CAST_CHUNK
        n_chunks = max(k for k in range(1, units + 1) if units % k == 0 and k <= n_steps)
        block = tuple(n // n_chunks if a == axis else n for a, n in enumerate(shape))

        def index(lead, n_chunks=n_chunks, axis=axis):
            def index_map(bb, i):
                c = jnp.minimum(bb * n_i + i, n_chunks - 1)
                return (lead,) + tuple(c if a == axis else 0 for a in range(2))
            return index_map

        in_specs.append(pl.BlockSpec((None,) + block, index(src_layer)))
        out_specs.append(pl.BlockSpec((None,) + block, index(0)))
        out_shapes.append(jax.ShapeDtypeStruct((1,) + shape, BF16))
        chunks.append(n_chunks)
    return in_specs, out_specs, out_shapes, tuple(chunks)


def _cast_chunks(cast_in, cast_out, chunks):
    step = pl.program_id(0) * pl.num_programs(1) + pl.program_id(1)
    for src, dst, n_chunks in zip(cast_in, cast_out, chunks):
        @pl.when(step < n_chunks)
        def _(src=src, dst=dst):
            dst[...] = src[...].astype(BF16)


def _mlp(x, mod5, g2, w1, w2, g_final, layer, ctx_row, tm, final, cast_next=()):
    b, s, d = x.shape
    n_i = s // tm
    cast_specs, cast_out_specs, cast_out_shapes, chunks = _cast_plan(cast_next, layer + 1, b, n_i)
    out = pl.pallas_call(
        functools.partial(_mlp_kernel, final=final, chunks=chunks),
        out_shape=[jax.ShapeDtypeStruct(x.shape, F32)] + cast_out_shapes,
        grid=(b, n_i),
        in_specs=[
            pl.BlockSpec((None, tm, d), lambda bb, i: (bb, i, 0)),
            _layer_spec((1, d), layer),
            _mod_spec(layer, 3, ctx_row),
            _mod_spec(layer, 4, ctx_row),
            _mod_spec(layer, 5, ctx_row),
            _layer_spec((d, D_FF), 0, single_buffer=True),
            _layer_spec((D_FF, d), 0, single_buffer=True),
            pl.BlockSpec((1, d), lambda bb, i: (0, 0)),
        ] + cast_specs,
        out_specs=[pl.BlockSpec((None, tm, d), lambda bb, i: (bb, i, 0))] + cast_out_specs,
        scratch_shapes=[pltpu.VMEM((tm, d), F32)],
        compiler_params=_params("arbitrary", "arbitrary"),
        name="mlp",
    )(x, g2, mod5, mod5, mod5, w1, w2, g_final, *[w for w, _ in cast_next])
    return out[0], tuple(out[1:])


def _block_diag(w):
    *lead, h, n, _ = w.shape
    eye = jnp.eye(h, dtype=w.dtype)
    out = jnp.einsum('...hij,hg->...higj', w, eye)
    return out.reshape(*lead, h * n, h * n)


def _edge_tiles(table, ts):
    n_t = table.shape[0] // ts
    mid = min(1, n_t - 1)
    return jnp.concatenate([table[:ts], table[mid * ts:(mid + 1) * ts], table[-ts:]], axis=0)


def _tile_rows(s, want):
    t = min(s, want)
    while s % t:
        t //= 2
    return t


def kernel(x, c, ctx, c_ctx, w_mod, b_mod, g_norm1, g_norm2, w_in, conv_a_w, conv_a_b, lru_wa, lru_ba, lru_wx, lru_bx, lru_lambda, pool_w, pool_scale, sgu_ln_g, sgu_ln_b, sgu_w, sgu_b, conv_d_w, conv_d_b, conv_ln_g, conv_ln_b, w_out, w_mlp1, w_mlp2, g_final):
    depth = w_in.shape[0]
    bsz, seq, d = x.shape
    ctx_len = ctx.shape[1]
    assert d == D_MODEL and bsz < MOD_ROWS
    assert seq % SGU_CHUNK == 0 and ctx_len % SGU_CHUNK == 0
    ctx_row = bsz

    big = ((w_mlp1, 1), (w_mlp2, 0), (w_in, 1), (w_out, 1))
    w_in_l, w_out_l = (w[0:1].astype(BF16) for w in (w_in, w_out))
    w1_l = w2_l = None
    wa_bd = _block_diag(lru_wa)
    wx_bd = _block_diag(lru_wx)
    wg = jnp.concatenate([wa_bd[:, 0], wx_bd[:, 0], wa_bd[:, 1], wx_bd[:, 1]], axis=-1).astype(BF16)
    bg = jnp.concatenate([lru_ba[:, 0], lru_bx[:, 0], lru_ba[:, 1], lru_bx[:, 1]], axis=-1)[:, None, :]
    lam = lru_lambda.reshape(depth, 1, 2 * W_LRU)
    pool_bd = _block_diag(pool_w).astype(BF16)
    sgu_ws = jnp.swapaxes(sgu_w, 1, 2).reshape(depth, SGU_CHUNK, SGU_HEADS * SGU_CHUNK).astype(BF16)
    sgu_bias = jnp.repeat(jnp.swapaxes(sgu_b, 1, 2), SGU_HEAD_DIM, axis=2)
    row = lambda v: v[:, None, :]

    cs = jnp.zeros((MOD_ROWS, d), F32).at[:bsz].set(c).at[ctx_row].set(c_ctx)
    mod = _modulation(cs, w_mod, b_mod)
    mod5 = mod.reshape(depth, MOD_ROWS, N_MOD, 1, d)

    ts = _tile_rows(seq, 1024)
    ts_c = _tile_rows(ctx_len, 512)
    tm = _tile_rows(seq, 1024)
    tm_c = _tile_rows(bsz * ctx_len, 1024)
    assert ts % GRID_W == 0 and ts % CONV_ROWS == 0 and ts_c % CONV_ROWS == 0
    pos_e = _pos_table()
    xc = ctx
    zeros_h = jnp.zeros((bsz, 1, W_LRU), F32)
    g_fin = g_final.reshape(1, d)
    mix_w = (_pool_bands(), pool_bd, row(pool_scale), row(sgu_ln_g), row(sgu_ln_b), sgu_ws,
             sgu_bias, conv_d_w, row(conv_d_b), row(conv_ln_g), row(conv_ln_b))
    icnt = _edge_tiles(_pool_icnt(seq, ts), ts)
    icnt_c = _edge_tiles(_pool_icnt(ctx_len, ts_c), ts_c)

    for l in range(depth):
        last = l == depth - 1
        first = l == 0
        pc, lru_c, hcf_last, _ = _proj_lru(xc, pos_e, mod5, row(g_norm1), w_in_l, conv_a_w,
                                           row(conv_a_b), wg, bg, lam, zeros_h, l, ctx_row, ts_c,
                                           False)
        px, lru, _, cast0 = _proj_lru(x, pos_e, mod5, row(g_norm1), w_in_l, conv_a_w, row(conv_a_b),
                                      wg, bg, lam, hcf_last, l, None, ts, first,
                                      cast=big[:2] if first else ())
        if first:
            w1_l, w2_l = cast0
        if last:
            hcb_first = _bwd_state(lru_c)
        else:
            xc, hcb_first = _mix(xc, pos_e, pc, lru_c, zeros_h, mod5, icnt_c, *mix_w, w_out_l, l,
                                 ctx_row, ts_c, False)
            xc, _ = _mlp(xc.reshape(1, -1, d), mod5, row(g_norm2), w1_l, w2_l, g_fin, l, ctx_row,
                         tm_c, False)
            xc = xc.reshape(bsz, ctx_len, d)
        x, _ = _mix(x, pos_e, px, lru, hcb_first, mod5, icnt, *mix_w, w_out_l, l, None, ts, first)
        x, nxt = _mlp(x, mod5, row(g_norm2), w1_l, w2_l, g_fin, l, None, tm, last,
                      cast_next=() if last else big)
        if not last:
            w1_l, w2_l, w_in_l, w_out_l = nxt
    return x
```
